```python
import math
import jax, jax.numpy as jnp
from jax import lax
import numpy as np

D_MODEL = 1024
BATCH = 8
SEQ = 2048
DEPTH = 2

CHUNK = 64
Q_BLOCK = 128
HEAD_DIM = 64
FOX_HEADS = 12
FOX_WIDTH = FOX_HEADS * HEAD_DIM
DIFF_HEADS = 6
DIFF_QK_WIDTH = DIFF_HEADS * 2 * HEAD_DIM
DIFF_V_DIM = 2 * HEAD_DIM
DIFF_V_WIDTH = DIFF_HEADS * DIFF_V_DIM
MEM_HEADS = 4
MEM_WIDTH = MEM_HEADS * HEAD_DIM
MEM_TOKENS = 256
MIX_WIDTH = FOX_WIDTH + MEM_WIDTH
ROPE_THETA = 500000.0
ROT_DIM = HEAD_DIM // 4
D_FF = 256 * ((8 * D_MODEL // 3 + 255) // 256)
N_EXPERTS = 8
TOP_K = 2
EXPERT_FF = 7 * D_MODEL // 2
N_A = DEPTH // 2
N_B = DEPTH - N_A
N_DENSE = (DEPTH + 1) // 2
N_MOE = DEPTH // 2
EPS = 1e-5
MAX_START = 4096

kernel_name = "yoco_fox_diffattn_mem_moe_trunk"


def rmsnorm(x, g):
    xf = x.astype(jnp.float32)
    y = xf * lax.rsqrt(jnp.mean(xf * xf, axis=-1, keepdims=True) + EPS)
    return (y * g.astype(jnp.float32)).astype(x.dtype)


def rotary_partial(x, positions):
    half = ROT_DIM // 2
    inv_freq = ROPE_THETA ** (-(jnp.arange(half, dtype=jnp.float32) * 2.0 / ROT_DIM))
    ang = positions.astype(jnp.float32)[:, :, None] * inv_freq
    cos = jnp.cos(ang)[:, :, None, :]
    sin = jnp.sin(ang)[:, :, None, :]
    xf = x.astype(jnp.float32)
    x1, x2 = xf[..., :half], xf[..., half:ROT_DIM]
    out = jnp.concatenate([x1 * cos - x2 * sin, x2 * cos + x1 * sin, xf[..., ROT_DIM:]], axis=-1)
    return out.astype(x.dtype)


def masked_softmax(s, mask):
    return jax.nn.softmax(jnp.where(mask, s, -jnp.inf), axis=-1)


def forgetting_attention(q, k, v, log_f):
    S = q.shape[1]
    scale = HEAD_DIM ** -0.5
    cum = jnp.cumsum(log_f, axis=1).transpose(0, 2, 1)
    pos = jnp.arange(S)
    outs = []
    for i in range(S // Q_BLOCK):
        lo, hi = i * Q_BLOCK, (i + 1) * Q_BLOCK
        s = jnp.einsum('bqhd,bkhd->bhqk', q[:, lo:hi], k[:, :hi]).astype(jnp.float32) * scale
        s = s + cum[:, :, lo:hi, None] - cum[:, :, None, :hi]
        mask = pos[None, :hi] <= pos[lo:hi, None]
        p = masked_softmax(s, mask)
        outs.append(jnp.einsum('bhqk,bkhd->bqhd', p.astype(v.dtype), v[:, :hi]))
    return jnp.concatenate(outs, axis=1)


def differential_attention(q1, q2, k1, k2, v, lam):
    S = q1.shape[1]
    scale = HEAD_DIM ** -0.5
    chunk_id = jnp.arange(S) // CHUNK
    outs = []
    for i in range(S // Q_BLOCK):
        lo, hi = i * Q_BLOCK, (i + 1) * Q_BLOCK
        mask = chunk_id[None, :hi] <= chunk_id[lo:hi, None]
        s1 = jnp.einsum('bqhd,bkhd->bhqk', q1[:, lo:hi], k1[:, :hi]).astype(jnp.float32) * scale
        s2 = jnp.einsum('bqhd,bkhd->bhqk', q2[:, lo:hi], k2[:, :hi]).astype(jnp.float32) * scale
        a = masked_softmax(s1, mask) - lam * masked_softmax(s2, mask)
        outs.append(jnp.einsum('bhqk,bkhd->bqhd', a.astype(v.dtype), v[:, :hi]))
    return jnp.concatenate(outs, axis=1)


def memory_attention(q, mk, mv):
    s = jnp.einsum('bqhd,bmhd->bhqm', q, mk).astype(jnp.float32) * (HEAD_DIM ** -0.5)
    p = jax.nn.softmax(s, axis=-1)
    return jnp.einsum('bhqm,bmhd->bqhd', p.astype(mv.dtype), mv)


def swiglu(h, w_gate_up, w_down):
    g, u = jnp.split(h @ w_gate_up, 2, axis=-1)
    return (jax.nn.silu(g) * u) @ w_down


def moe_swiglu(h, w_router, w_gate_up, w_down):
    B, S, D = h.shape
    t = h.reshape(B * S, D)
    logits = (t @ w_router).astype(jnp.float32)
    top_val, top_idx = lax.top_k(logits, TOP_K)
    gates = jax.nn.softmax(top_val, axis=-1)
    combine = jnp.sum(jax.nn.one_hot(top_idx, N_EXPERTS, dtype=jnp.float32) * gates[..., None], axis=1)
    y = jnp.zeros_like(t)
    for e in range(N_EXPERTS):
        y = y + combine[:, e:e + 1].astype(t.dtype) * swiglu(t, w_gate_up[e], w_down[e])
    return y.reshape(B, S, D)


def setup_inputs(seed: int = 0) -> dict:
    key = jax.random.key(seed)
    ks = jax.random.split(key, 26)
    f32 = jnp.float32

    def w(k, shape, fan_in):
        return jax.random.normal(k, shape, f32) * (fan_in ** -0.5)

    def gain(k, shape):
        return 1.0 + 0.02 * jax.random.normal(k, shape, f32)

    start = jax.random.randint(ks[2], (BATCH, 1), 0, MAX_START, dtype=jnp.int32)
    positions = start + jnp.arange(SEQ, dtype=jnp.int32)[None, :]
    a_cols = 3 * FOX_WIDTH + FOX_HEADS + MEM_WIDTH
    return {
        "x": jax.random.normal(ks[0], (BATCH, SEQ, D_MODEL), f32),
        "mem": jax.random.normal(ks[1], (BATCH, MEM_TOKENS, D_MODEL), f32),
        "positions": positions,
        "mix_norm_g": gain(ks[3], (DEPTH, D_MODEL)),
        "ffn_norm_g": gain(ks[4], (DEPTH, D_MODEL)),
        "mem_norm_g": gain(ks[5], (DEPTH, D_MODEL)),
        "w_mem_kv": w(ks[6], (DEPTH, D_MODEL, 2 * MEM_WIDTH), D_MODEL),
        "w_out": w(ks[7], (DEPTH, MIX_WIDTH, D_MODEL), MIX_WIDTH),
        "w_in_a": w(ks[8], (N_A, D_MODEL, a_cols), D_MODEL),
        "b_forget": 0.1 * jax.random.normal(ks[9], (N_A, FOX_HEADS), f32),
        "w_q_b": w(ks[10], (N_B, D_MODEL, DIFF_QK_WIDTH + MEM_WIDTH), D_MODEL),
        "lambda_q1": 0.1 * jax.random.normal(ks[11], (N_B, HEAD_DIM), f32),
        "lambda_k1": 0.1 * jax.random.normal(ks[12], (N_B, HEAD_DIM), f32),
        "lambda_q2": 0.1 * jax.random.normal(ks[13], (N_B, HEAD_DIM), f32),
        "lambda_k2": 0.1 * jax.random.normal(ks[14], (N_B, HEAD_DIM), f32),
        "subln_g": gain(ks[15], (N_B, DIFF_V_DIM)),
        "kv_norm_g": gain(ks[16], (D_MODEL,)),
        "w_kv_shared": w(ks[17], (D_MODEL, DIFF_QK_WIDTH + DIFF_V_WIDTH), D_MODEL),
        "w_gate_up_dense": w(ks[18], (N_DENSE, D_MODEL, 2 * D_FF), D_MODEL),
        "w_down_dense": w(ks[19], (N_DENSE, D_FF, D_MODEL), D_FF),
        "w_router": w(ks[20], (N_MOE, D_MODEL, N_EXPERTS), D_MODEL),
        "w_gate_up_moe": w(ks[21], (N_MOE, N_EXPERTS, D_MODEL, 2 * EXPERT_FF), D_MODEL),
        "w_down_moe": w(ks[22], (N_MOE, N_EXPERTS, EXPERT_FF, D_MODEL), EXPERT_FF),
        "final_norm_g": gain(ks[23], (D_MODEL,)),
    }


def reference(x, mem, positions, mix_norm_g, ffn_norm_g, mem_norm_g, w_mem_kv, w_out,
              w_in_a, b_forget, w_q_b, lambda_q1, lambda_k1, lambda_q2, lambda_k2, subln_g,
              kv_norm_g, w_kv_shared, w_gate_up_dense, w_down_dense, w_router,
              w_gate_up_moe, w_down_moe, final_norm_g):
    B, S, _ = x.shape
    f32 = jnp.float32
    shared_k1 = shared_k2 = shared_v = None
    for l in range(DEPTH):
        h = rmsnorm(x, mix_norm_g[l])
        m = rmsnorm(mem, mem_norm_g[l])
        mkv = (m @ w_mem_kv[l]).reshape(B, MEM_TOKENS, 2, MEM_HEADS, HEAD_DIM)
        mk, mv = mkv[:, :, 0], mkv[:, :, 1]
        if l < N_A:
            proj = h @ w_in_a[l]
            q, k, v, fl, qm = jnp.split(
                proj, [FOX_WIDTH, 2 * FOX_WIDTH, 3 * FOX_WIDTH, 3 * FOX_WIDTH + FOX_HEADS], axis=-1)
            log_f = jax.nn.log_sigmoid(fl.astype(f32) + b_forget[l].astype(f32))
            y_self = forgetting_attention(
                q.reshape(B, S, FOX_HEADS, HEAD_DIM),
                k.reshape(B, S, FOX_HEADS, HEAD_DIM),
                v.reshape(B, S, FOX_HEADS, HEAD_DIM), log_f).reshape(B, S, FOX_WIDTH)
        else:
            j = l - N_A
            proj = h @ w_q_b[j]
            qd, qm = jnp.split(proj, [DIFF_QK_WIDTH], axis=-1)
            qd = qd.reshape(B, S, DIFF_HEADS, 2, HEAD_DIM)
            q1 = rotary_partial(qd[:, :, :, 0], positions)
            q2 = rotary_partial(qd[:, :, :, 1], positions)
            lam_init = 0.8 - 0.6 * math.exp(-0.3 * l)
            lam = (jnp.exp(jnp.sum(lambda_q1[j].astype(f32) * lambda_k1[j].astype(f32)))
                   - jnp.exp(jnp.sum(lambda_q2[j].astype(f32) * lambda_k2[j].astype(f32)))
                   + lam_init)
            o = differential_attention(q1, q2, shared_k1, shared_k2, shared_v, lam)
            o = rmsnorm(o, subln_g[j]) * (1.0 - lam_init)
            y_self = o.reshape(B, S, DIFF_V_WIDTH)
        y_mem = memory_attention(qm.reshape(B, S, MEM_HEADS, HEAD_DIM), mk, mv).reshape(B, S, MEM_WIDTH)
        x = x + jnp.concatenate([y_self, y_mem], axis=-1) @ w_out[l]

        h = rmsnorm(x, ffn_norm_g[l])
        if l % 2 == 0:
            x = x + swiglu(h, w_gate_up_dense[l // 2], w_down_dense[l // 2])
        else:
            x = x + moe_swiglu(h, w_router[l // 2], w_gate_up_moe[l // 2], w_down_moe[l // 2])

        if l == N_A - 1:
            kvs = rmsnorm(x, kv_norm_g) @ w_kv_shared
            kd, vs = jnp.split(kvs, [DIFF_QK_WIDTH], axis=-1)
            kd = kd.reshape(B, S, DIFF_HEADS, 2, HEAD_DIM)
            shared_k1 = rotary_partial(kd[:, :, :, 0], positions)
            shared_k2 = rotary_partial(kd[:, :, :, 1], positions)
            shared_v = vs.reshape(B, S, DIFF_HEADS, DIFF_V_DIM)
    return rmsnorm(x, final_norm_g)
```

```python
import functools
import math

import jax
import jax.numpy as jnp
from jax import lax
from jax.experimental import pallas as pl
from jax.experimental.pallas import tpu as pltpu

F32 = jnp.float32
BF16 = jnp.bfloat16

HEAD_DIM = 64
LANES = 128
CHUNK_SHIFT = 6
FOX_HEADS = 12
DIFF_HEADS = 6
MEM_HEADS = 4
MEM_WIDTH = MEM_HEADS * HEAD_DIM
ROPE_THETA = 500000.0
ROT_DIM = HEAD_DIM // 4
N_EXPERTS = 8
EPS = 1e-5
NEG = -1e30
VMEM_LIMIT = 48 * 1024 * 1024

ROW_TILE = 512
FFN_ROW_TILE = 1024
FFN_COL_TILE = 256
ATT_TILE = 256
MEM_Q_TILE = 512
MOE_ROW_TILE = 512
GATHER_TILE = 256


def _params(*sem):
    return pltpu.CompilerParams(dimension_semantics=sem, vmem_limit_bytes=VMEM_LIMIT)


def _rms(x, g):
    ms = jnp.mean(x * x, axis=-1, keepdims=True)
    return x * lax.rsqrt(ms + EPS) * g


def _nt_dot(a, b):
    return lax.dot_general(a, b, (((1,), (1,)), ((), ())), preferred_element_type=F32)


def _dot(a, b):
    return jnp.dot(a, b, preferred_element_type=F32)


def _proj_a_kernel(x_ref, g_ref, w_ref, b_ref, o_ref, lf_ref, *, n_main, col_chunk):
    h = _rms(x_ref[...], g_ref[...]).astype(BF16)
    for c in range(n_main // col_chunk):
        cs = slice(c * col_chunk, (c + 1) * col_chunk)
        o_ref[:, cs] = _dot(h, w_ref[:, cs]).astype(BF16)
    z = _dot(h, w_ref[:, n_main:]) + b_ref[...]
    lf_ref[...] = jnp.minimum(z, 0.0) - jnp.log1p(jnp.exp(-jnp.abs(z)))


def _proj_a(x, g, w, b_pad):
    t, d = x.shape
    n_all = w.shape[1]
    n_main = n_all - LANES
    return pl.pallas_call(
        functools.partial(_proj_a_kernel, n_main=n_main, col_chunk=512),
        grid=(t // ROW_TILE,),
        in_specs=[
            pl.BlockSpec((ROW_TILE, d), lambda i: (i, 0)),
            pl.BlockSpec((1, d), lambda i: (0, 0)),
            pl.BlockSpec((d, n_all), lambda i: (0, 0)),
            pl.BlockSpec((1, LANES), lambda i: (0, 0)),
        ],
        out_specs=[
            pl.BlockSpec((ROW_TILE, n_main), lambda i: (i, 0)),
            pl.BlockSpec((ROW_TILE, LANES), lambda i: (i, 0)),
        ],
        out_shape=[
            jax.ShapeDtypeStruct((t, n_main), BF16),
            jax.ShapeDtypeStruct((t, LANES), F32),
        ],
        compiler_params=_params("parallel"),
        name="proj_a",
    )(x, g, w, b_pad)


def _cumsum_kernel(lf_ref, c_ref, ct_ref):
    x = lf_ref[...]
    s = x.shape[0]
    row = lax.broadcasted_iota(jnp.int32, x.shape, 0)
    sh = 1
    while sh < s:
        x = x + jnp.where(row >= sh, pltpu.roll(x, sh, 0), 0.0)
        sh *= 2
    c_ref[...] = x
    ct_ref[0] = x.T


def _cumsum(lf, batch, seq):
    return pl.pallas_call(
        _cumsum_kernel,
        grid=(batch,),
        in_specs=[pl.BlockSpec((seq, LANES), lambda b: (b, 0))],
        out_specs=[
            pl.BlockSpec((seq, LANES), lambda b: (b, 0)),
            pl.BlockSpec((1, LANES, seq), lambda b: (b, 0, 0)),
        ],
        out_shape=[
            jax.ShapeDtypeStruct((batch * seq, LANES), F32),
            jax.ShapeDtypeStruct((batch, LANES, seq), F32),
        ],
        compiler_params=_params("parallel"),
        name="cumsum",
    )(lf)


def _online_update(s, v, m, l, acc):
    m_new = jnp.maximum(m, jnp.max(s, axis=1, keepdims=True))
    alpha = jnp.exp(m - m_new)
    p = jnp.exp(s - m_new)
    l = alpha * l + jnp.sum(p, axis=1, keepdims=True)
    acc = alpha * acc + _dot(p.astype(BF16), v)
    return m_new, l, acc


def _fox_kernel(q_ref, k_ref, v_ref, c_ref, ct_ref, o_ref, *, tile):
    pair = pl.program_id(1)
    qi = pl.program_id(2)
    q = q_ref[...]
    lane = lax.broadcasted_iota(jnp.int32, (1, LANES), 1)
    low = lane < HEAD_DIM
    zero = jnp.zeros_like(q)
    qh = (jnp.where(low, q, zero), jnp.where(low, zero, q))
    cq = c_ref[...]
    lane_t = lax.broadcasted_iota(jnp.int32, cq.shape, 1)
    ci = tuple(
        jnp.sum(jnp.where(lane_t == 2 * pair + hh, cq, 0.0), axis=1, keepdims=True)
        for hh in range(2))
    sub = (2 * pair) & 7

    def scores(j, hh):
        ks = pl.ds(pl.multiple_of(j * tile, tile), tile)
        cj = ct_ref[0, pl.ds(sub + hh, 1), ks]
        return _nt_dot(qh[hh], k_ref[ks, :]) + (ci[hh] - cj)

    def body(j, carry):
        ks = pl.ds(pl.multiple_of(j * tile, tile), tile)
        v = v_ref[ks, :]
        out = []
        for hh in range(2):
            out.extend(_online_update(scores(j, hh), v, *carry[3 * hh:3 * hh + 3]))
        return tuple(out)

    init = []
    for _ in range(2):
        init += [jnp.full((tile, 1), NEG, F32), jnp.zeros((tile, 1), F32),
                 jnp.zeros((tile, LANES), F32)]
    carry = lax.fori_loop(0, qi, body, tuple(init))

    ri = lax.broadcasted_iota(jnp.int32, (tile, tile), 0)
    cj_ = lax.broadcasted_iota(jnp.int32, (tile, tile), 1)
    keep = cj_ <= ri
    ks = pl.ds(pl.multiple_of(qi * tile, tile), tile)
    v = v_ref[ks, :]
    res = []
    for hh in range(2):
        s = jnp.where(keep, scores(qi, hh), NEG)
        _, l, acc = _online_update(s, v, *carry[3 * hh:3 * hh + 3])
        res.append(acc / l)
    o_ref[...] = jnp.where(low, res[0], res[1]).astype(BF16)


def _fox_attention(qkvm, cum, cum_t, batch, seq):
    t = batch * seq
    nq = seq // ATT_TILE
    npair = FOX_HEADS // 2
    return pl.pallas_call(
        functools.partial(_fox_kernel, tile=ATT_TILE),
        grid=(batch, npair, nq),
        in_specs=[
            pl.BlockSpec((ATT_TILE, LANES), lambda b, p, i: (b * nq + i, p)),
            pl.BlockSpec((seq, LANES), lambda b, p, i: (b, npair + p)),
            pl.BlockSpec((seq, LANES), lambda b, p, i: (b, 2 * npair + p)),
            pl.BlockSpec((ATT_TILE, LANES), lambda b, p, i: (b * nq + i, 0)),
            pl.BlockSpec((1, 8, seq), lambda b, p, i: (b, p // 4, 0)),
        ],
        out_specs=pl.BlockSpec((ATT_TILE, LANES), lambda b, p, i: (b * nq + i, p)),
        out_shape=jax.ShapeDtypeStruct((t, npair * LANES), BF16),
        compiler_params=_params("parallel", "parallel", "arbitrary"),
        name="fox_attention",
    )(qkvm, qkvm, qkvm, cum, cum_t)


def _mem_kernel(q_ref, mk_ref, mv_ref, o_ref):
    q = q_ref[...]
    mk = mk_ref[...]
    mv = mv_ref[...]
    lane = lax.broadcasted_iota(jnp.int32, (1, MEM_WIDTH), 1)
    zero = jnp.zeros_like(q)
    out = jnp.zeros(q.shape, F32)
    for h in range(MEM_HEADS):
        hm = (lane >= h * HEAD_DIM) & (lane < (h + 1) * HEAD_DIM)
        s = _nt_dot(jnp.where(hm, q, zero), mk)
        p = jnp.exp(s - jnp.max(s, axis=1, keepdims=True))
        l = jnp.sum(p, axis=1, keepdims=True)
        out = jnp.where(hm, _dot(p.astype(BF16), mv) / l, out)
    o_ref[...] = out.astype(BF16)


def _mem_attention(q_arr, q_col_block, mkv, batch, seq, mem_tokens):
    nq = seq // MEM_Q_TILE
    return pl.pallas_call(
        _mem_kernel,
        grid=(batch, nq),
        in_specs=[
            pl.BlockSpec((MEM_Q_TILE, MEM_WIDTH), lambda b, i: (b * nq + i, q_col_block)),
            pl.BlockSpec((mem_tokens, MEM_WIDTH), lambda b, i: (b, 0)),
            pl.BlockSpec((mem_tokens, MEM_WIDTH), lambda b, i: (b, 1)),
        ],
        out_specs=pl.BlockSpec((MEM_Q_TILE, MEM_WIDTH), lambda b, i: (b * nq + i, 0)),
        out_shape=jax.ShapeDtypeStruct((batch * seq, MEM_WIDTH), BF16),
        compiler_params=_params("parallel", "parallel"),
        name="mem_attention",
    )(q_arr, mkv, mkv)


def _norm_matmul_kernel(x_ref, g_ref, w_ref, o_ref):
    h = _rms(x_ref[...], g_ref[...]).astype(BF16)
    o_ref[...] = _dot(h, w_ref[...]).astype(o_ref.dtype)


def _norm_matmul(x, g, w, name):
    t, d = x.shape
    n = w.shape[1]
    return pl.pallas_call(
        _norm_matmul_kernel,
        grid=(t // ROW_TILE,),
        in_specs=[
            pl.BlockSpec((ROW_TILE, d), lambda i: (i, 0)),
            pl.BlockSpec((1, d), lambda i: (0, 0)),
            pl.BlockSpec((d, n), lambda i: (0, 0)),
        ],
        out_specs=pl.BlockSpec((ROW_TILE, n), lambda i: (i, 0)),
        out_shape=jax.ShapeDtypeStruct((t, n), BF16),
        compiler_params=_params("parallel"),
        name=name,
    )(x, g, w)


def _out_proj_kernel(x_ref, ys_ref, ym_ref, w_ref, o_ref, *, self_width):
    acc = _dot(ys_ref[...], w_ref[:self_width, :]) + _dot(ym_ref[...], w_ref[self_width:, :])
    o_ref[...] = x_ref[...] + acc


def _out_proj(x, y_self, y_mem, w):
    t, d = x.shape
    sw = y_self.shape[1]
    return pl.pallas_call(
        functools.partial(_out_proj_kernel, self_width=sw),
        grid=(t // ROW_TILE,),
        in_specs=[
            pl.BlockSpec((ROW_TILE, d), lambda i: (i, 0)),
            pl.BlockSpec((ROW_TILE, sw), lambda i: (i, 0)),
            pl.BlockSpec((ROW_TILE, MEM_WIDTH), lambda i: (i, 0)),
            pl.BlockSpec(w.shape, lambda i: (0, 0)),
        ],
        out_specs=pl.BlockSpec((ROW_TILE, d), lambda i: (i, 0)),
        out_shape=jax.ShapeDtypeStruct((t, d), F32),
        compiler_params=_params("parallel"),
        name="out_proj",
    )(x, y_self, y_mem, w)


def _swiglu_hidden(g, u):
    return (g * jax.nn.sigmoid(g) * u).astype(BF16)


def _ffn_kernel(x_ref, g_ref, wg_ref, wu_ref, wd_ref, o_ref, h_sc):
    @pl.when(pl.program_id(1) == 0)
    def _():
        x = x_ref[...]
        h_sc[...] = _rms(x, g_ref[...]).astype(BF16)
        o_ref[...] = x

    h = h_sc[...]
    hid = _swiglu_hidden(_dot(h, wg_ref[...]), _dot(h, wu_ref[...]))
    o_ref[...] += _dot(hid, wd_ref[...])


def _dense_ffn(x, g, w_gate_up, w_down):
    t, d = x.shape
    ff = w_down.shape[0]
    nf = ff // FFN_COL_TILE
    return pl.pallas_call(
        _ffn_kernel,
        grid=(t // FFN_ROW_TILE, nf),
        in_specs=[
            pl.BlockSpec((FFN_ROW_TILE, d), lambda i, f: (i, 0)),
            pl.BlockSpec((1, d), lambda i, f: (0, 0)),
            pl.BlockSpec((d, FFN_COL_TILE), lambda i, f: (0, f)),
            pl.BlockSpec((d, FFN_COL_TILE), lambda i, f: (0, nf + f)),
            pl.BlockSpec((FFN_COL_TILE, d), lambda i, f: (f, 0)),
        ],
        out_specs=pl.BlockSpec((FFN_ROW_TILE, d), lambda i, f: (i, 0)),
        out_shape=jax.ShapeDtypeStruct((t, d), F32),
        scratch_shapes=[pltpu.VMEM((FFN_ROW_TILE, d), BF16)],
        compiler_params=_params("parallel", "arbitrary"),
        name="dense_ffn",
    )(x, g, w_gate_up, w_gate_up, w_down)


def _rope_kernel(pos_ref, inv_ref, cos_ref, sin_ref):
    pos = pos_ref[...].astype(F32)
    ang = pos * inv_ref[...]
    lane = lax.broadcasted_iota(jnp.int32, ang.shape, 1)
    rot = (lane & (HEAD_DIM - 1)) < ROT_DIM
    first = (lane & (ROT_DIM - 1)) < (ROT_DIM // 2)
    sn = jnp.sin(ang)
    cos_ref[...] = jnp.where(rot, jnp.cos(ang), 1.0)
    sin_ref[...] = jnp.where(rot, jnp.where(first, -sn, sn), 0.0)


def _rope_tables(pos_col, inv_lane):
    t = pos_col.shape[0]
    return pl.pallas_call(
        _rope_kernel,
        grid=(t // ROW_TILE,),
        in_specs=[
            pl.BlockSpec((ROW_TILE, 1), lambda i: (i, 0)),
            pl.BlockSpec((1, LANES), lambda i: (0, 0)),
        ],
        out_specs=[pl.BlockSpec((ROW_TILE, LANES), lambda i: (i, 0))] * 2,
        out_shape=[jax.ShapeDtypeStruct((t, LANES), F32)] * 2,
        compiler_params=_params("parallel"),
        name="rope_tables",
    )(pos_col, inv_lane)


def _proj_rope_kernel(x_ref, g_ref, w_ref, cos_ref, sin_ref, o_ref, *, n_rope, col_chunk):
    h = _rms(x_ref[...], g_ref[...]).astype(BF16)
    cos = cos_ref[...]
    sin = sin_ref[...]
    lane = lax.broadcasted_iota(jnp.int32, (1, LANES), 1)
    first = (lane & (ROT_DIM - 1)) < (ROT_DIM // 2)
    half = ROT_DIM // 2
    n = w_ref.shape[1]
    for c in range(n // col_chunk):
        a = _dot(h, w_ref[:, c * col_chunk:(c + 1) * col_chunk])
        for s in range(col_chunk // LANES):
            col = c * col_chunk + s * LANES
            blk = a[:, s * LANES:(s + 1) * LANES]
            if col < n_rope:
                partner = jnp.where(first, pltpu.roll(blk, LANES - half, 1),
                                    pltpu.roll(blk, half, 1))
                blk = blk * cos + partner * sin
            o_ref[:, col:col + LANES] = blk.astype(BF16)


def _proj_rope(x, g, w, cos, sin, n_rope, name):
    t, d = x.shape
    n = w.shape[1]
    return pl.pallas_call(
        functools.partial(_proj_rope_kernel, n_rope=n_rope, col_chunk=256),
        grid=(t // ROW_TILE,),
        in_specs=[
            pl.BlockSpec((ROW_TILE, d), lambda i: (i, 0)),
            pl.BlockSpec((1, d), lambda i: (0, 0)),
            pl.BlockSpec((d, n), lambda i: (0, 0)),
            pl.BlockSpec((ROW_TILE, LANES), lambda i: (i, 0)),
            pl.BlockSpec((ROW_TILE, LANES), lambda i: (i, 0)),
        ],
        out_specs=pl.BlockSpec((ROW_TILE, n), lambda i: (i, 0)),
        out_shape=jax.ShapeDtypeStruct((t, n), BF16),
        compiler_params=_params("parallel"),
        name=name,
    )(x, g, w, cos, sin)


def _diff_kernel(lam_ref, sg_ref, q_ref, k_ref, v_ref, o_ref, *, tile, lam_init):
    qi = pl.program_id(2)
    lp = lam_ref[...]
    lam = (jnp.exp(jnp.sum(lp[0:1] * lp[1:2], axis=1, keepdims=True))
           - jnp.exp(jnp.sum(lp[2:3] * lp[3:4], axis=1, keepdims=True)) + lam_init)
    q = q_ref[...]
    lane = lax.broadcasted_iota(jnp.int32, (1, LANES), 1)
    low = lane < HEAD_DIM
    zero = jnp.zeros_like(q)
    qh = (jnp.where(low, q, zero), jnp.where(low, zero, q))

    def body(j, carry):
        ks = pl.ds(pl.multiple_of(j * tile, tile), tile)
        k = k_ref[ks, :]
        v = v_ref[ks, :]
        out = []
        for hh in range(2):
            out.extend(_online_update(_nt_dot(qh[hh], k), v, *carry[3 * hh:3 * hh + 3]))
        return tuple(out)

    init = []
    for _ in range(2):
        init += [jnp.full((tile, 1), NEG, F32), jnp.zeros((tile, 1), F32),
                 jnp.zeros((tile, LANES), F32)]
    carry = lax.fori_loop(0, qi, body, tuple(init))

    ri = lax.broadcasted_iota(jnp.int32, (tile, tile), 0)
    cj = lax.broadcasted_iota(jnp.int32, (tile, tile), 1)
    keep = (cj >> CHUNK_SHIFT) <= (ri >> CHUNK_SHIFT)
    ks = pl.ds(pl.multiple_of(qi * tile, tile), tile)
    k = k_ref[ks, :]
    v = v_ref[ks, :]
    res = []
    for hh in range(2):
        s = jnp.where(keep, _nt_dot(qh[hh], k), NEG)
        _, l, acc = _online_update(s, v, *carry[3 * hh:3 * hh + 3])
        res.append(acc / l)
    o = res[0] - lam * res[1]
    o = _rms(o, sg_ref[...]) * (1.0 - lam_init)
    o_ref[...] = o.astype(BF16)


def _diff_attention(lam_rows, subln_g, qb, kv, batch, seq, lam_init):
    t = batch * seq
    nq = seq // ATT_TILE
    return pl.pallas_call(
        functools.partial(_diff_kernel, tile=ATT_TILE, lam_init=lam_init),
        grid=(batch, DIFF_HEADS, nq),
        in_specs=[
            pl.BlockSpec((8, LANES), lambda b, h, i: (0, 0)),
            pl.BlockSpec((1, LANES), lambda b, h, i: (0, 0)),
            pl.BlockSpec((ATT_TILE, LANES), lambda b, h, i: (b * nq + i, h)),
            pl.BlockSpec((seq, LANES), lambda b, h, i: (b, h)),
            pl.BlockSpec((seq, LANES), lambda b, h, i: (b, DIFF_HEADS + h)),
        ],
        out_specs=pl.BlockSpec((ATT_TILE, LANES), lambda b, h, i: (b * nq + i, h)),
        out_shape=jax.ShapeDtypeStruct((t, DIFF_HEADS * LANES), BF16),
        compiler_params=_params("parallel", "parallel", "arbitrary"),
        name="diff_attention",
    )(lam_rows, subln_g, qb, kv, kv)


def _moe_pre_kernel(x_ref, g_ref, wr_ref, tri_ref, hp_ref, route_ref, cnt_ref, run_sc):
    @pl.when(pl.program_id(0) == 0)
    def _():
        run_sc[...] = jnp.zeros_like(run_sc)

    hf = _rms(x_ref[...], g_ref[...])
    d = hf.shape[1]
    hb = hf.astype(BF16).astype(F32)
    bits = pltpu.bitcast(hb, jnp.uint32)
    hp_ref[...] = bits[:, :d // 2] | (bits[:, d // 2:] >> 16)

    logits = jnp.dot(hf, wr_ref[...], preferred_element_type=F32,
                     precision=lax.Precision.HIGHEST)
    lane = lax.broadcasted_iota(jnp.int32, logits.shape, 1)
    lanef = lane.astype(F32)
    lg = jnp.where(lane < N_EXPERTS, logits, NEG)
    v1 = jnp.max(lg, axis=1, keepdims=True)
    i1 = jnp.min(jnp.where(lg == v1, lanef, float(LANES)), axis=1, keepdims=True)
    lg2 = jnp.where(lanef == i1, NEG, lg)
    v2 = jnp.max(lg2, axis=1, keepdims=True)
    i2 = jnp.min(jnp.where(lg2 == v2, lanef, float(LANES)), axis=1, keepdims=True)
    e = jnp.exp(v2 - v1)
    g1 = 1.0 / (1.0 + e)
    g2 = e / (1.0 + e)

    oh1 = lanef == i1
    oh2 = lanef == i2
    oh = jnp.where(oh1 | oh2, 1.0, 0.0)
    before = _dot(tri_ref[...], oh.astype(BF16)) + run_sc[...]
    r1 = jnp.sum(jnp.where(oh1, before, 0.0), axis=1, keepdims=True)
    r2 = jnp.sum(jnp.where(oh2, before, 0.0), axis=1, keepdims=True)
    run_sc[...] += jnp.sum(oh, axis=0, keepdims=True)
    cnt_ref[...] = run_sc[...]

    route = jnp.where(lane == 0, i1, 0.0)
    for ln, val in ((1, i2), (2, r1), (3, r2), (4, g1), (5, g2)):
        route = jnp.where(lane == ln, val, route)
    route_ref[...] = route


def _moe_pre(x, g, w_router_pad, tri):
    t, d = x.shape
    return pl.pallas_call(
        _moe_pre_kernel,
        grid=(t // ROW_TILE,),
        in_specs=[
            pl.BlockSpec((ROW_TILE, d), lambda i: (i, 0)),
            pl.BlockSpec((1, d), lambda i: (0, 0)),
            pl.BlockSpec((d, LANES), lambda i: (0, 0)),
            pl.BlockSpec((ROW_TILE, ROW_TILE), lambda i: (0, 0)),
        ],
        out_specs=[
            pl.BlockSpec((ROW_TILE, d // 2), lambda i: (i, 0)),
            pl.BlockSpec((ROW_TILE, LANES), lambda i: (i, 0)),
            pl.BlockSpec((1, LANES), lambda i: (0, 0)),
        ],
        out_shape=[
            jax.ShapeDtypeStruct((t, d // 2), jnp.uint32),
            jax.ShapeDtypeStruct((t, LANES), F32),
            jax.ShapeDtypeStruct((1, LANES), F32),
        ],
        scratch_shapes=[pltpu.VMEM((1, LANES), F32)],
        compiler_params=_params("arbitrary"),
        name="moe_pre",
    )(x, g, w_router_pad, tri)


def _dispatch_kernel(pos_ref, hp_ref, init_ref, out_ref, sem, *, tile):
    del init_ref
    base = pl.program_id(0) * tile * 2

    def row_copy(r, k):
        dst = pos_ref[base + 2 * r + k]
        return pltpu.make_async_copy(hp_ref.at[pl.ds(r, 1)], out_ref.at[pl.ds(dst, 1)], sem)

    def start(r, c):
        row_copy(r, 0).start()
        row_copy(r, 1).start()
        return c

    def wait(r, c):
        row_copy(r, 0).wait()
        row_copy(r, 1).wait()
        return c

    lax.fori_loop(0, tile, start, 0, unroll=8)
    lax.fori_loop(0, tile, wait, 0, unroll=8)


def _dispatch(pos, hp, n_slots):
    t, w = hp.shape
    init = jnp.zeros((n_slots, w), hp.dtype)
    return pl.pallas_call(
        functools.partial(_dispatch_kernel, tile=GATHER_TILE),
        grid_spec=pltpu.PrefetchScalarGridSpec(
            num_scalar_prefetch=1,
            grid=(t // GATHER_TILE,),
            in_specs=[
                pl.BlockSpec((GATHER_TILE, w), lambda i, pos: (i, 0)),
                pl.BlockSpec(memory_space=pl.ANY),
            ],
            out_specs=pl.BlockSpec(memory_space=pl.ANY),
            scratch_shapes=[pltpu.SemaphoreType.DMA(())],
        ),
        out_shape=jax.ShapeDtypeStruct((n_slots, w), hp.dtype),
        input_output_aliases={2: 0},
        compiler_params=_params("arbitrary"),
        name="moe_dispatch",
    )(pos, hp, init)


def _moe_kernel(te_ref, na_ref, xp_ref, wg_ref, wu_ref, wd_ref, o_ref, xa_sc, xb_sc):
    del te_ref
    i = pl.program_id(0)
    f = pl.program_id(1)

    @pl.when(f == 0)
    def _():
        o_ref[...] = jnp.zeros_like(o_ref)

    @pl.when(i < na_ref[0])
    def _():
        @pl.when(f == 0)
        def _():
            pk = xp_ref[...]
            xa_sc[...] = pltpu.bitcast(pk & jnp.uint32(0xFFFF0000), F32).astype(BF16)
            xb_sc[...] = pltpu.bitcast(pk << 16, F32).astype(BF16)

        xa = xa_sc[...]
        xb = xb_sc[...]
        half = xa.shape[1]
        wg = wg_ref[0]
        wu = wu_ref[0]
        g = _dot(xa, wg[:half]) + _dot(xb, wg[half:])
        u = _dot(xa, wu[:half]) + _dot(xb, wu[half:])
        o_ref[...] += _dot(_swiglu_hidden(g, u), wd_ref[0])


def _moe_experts(tile_expert, n_active, xp, w_gate_up, w_down, n_tiles):
    d = w_down.shape[2]
    ff = w_down.shape[1]
    nf = ff // FFN_COL_TILE
    tm = MOE_ROW_TILE

    def row(i, f, te, na):
        return (jnp.minimum(i, na[0] - 1), 0)

    def col(i, f, na):
        return jnp.where(i < na[0], f, nf - 1)

    return pl.pallas_call(
        _moe_kernel,
        grid_spec=pltpu.PrefetchScalarGridSpec(
            num_scalar_prefetch=2,
            grid=(n_tiles, nf),
            in_specs=[
                pl.BlockSpec((tm, d // 2), row),
                pl.BlockSpec((1, d, FFN_COL_TILE), lambda i, f, te, na: (te[i], 0, col(i, f, na))),
                pl.BlockSpec((1, d, FFN_COL_TILE),
                             lambda i, f, te, na: (te[i], 0, nf + col(i, f, na))),
                pl.BlockSpec((1, FFN_COL_TILE, d), lambda i, f, te, na: (te[i], col(i, f, na), 0)),
            ],
            out_specs=pl.BlockSpec((tm, d), lambda i, f, te, na: (i, 0)),
            scratch_shapes=[pltpu.VMEM((tm, d // 2), BF16), pltpu.VMEM((tm, d // 2), BF16)],
        ),
        out_shape=jax.ShapeDtypeStruct((n_tiles * tm, d), F32),
        compiler_params=_params("arbitrary", "arbitrary"),
        name="moe_experts",
    )(tile_expert, n_active, xp, w_gate_up, w_gate_up, w_down)


def _combine_kernel(pos_ref, x_ref, route_ref, g_ref, y_ref, o_ref, buf, sem, *, tile):
    base = pl.program_id(0) * tile * 2

    def row_copy(r, k):
        src = pos_ref[base + 2 * r + k]
        return pltpu.make_async_copy(y_ref.at[pl.ds(src, 1)], buf.at[k, pl.ds(r, 1)], sem)

    def start(r, c):
        row_copy(r, 0).start()
        row_copy(r, 1).start()
        return c

    def wait(r, c):
        row_copy(r, 0).wait()
        row_copy(r, 1).wait()
        return c

    lax.fori_loop(0, tile, start, 0, unroll=8)
    lax.fori_loop(0, tile, wait, 0, unroll=8)
    route = route_ref[...]
    y = x_ref[...] + route[:, 4:5] * buf[0] + route[:, 5:6] * buf[1]
    o_ref[...] = _rms(y, g_ref[...])


def _combine(pos, x, route, g, y_sorted):
    t, d = x.shape
    return pl.pallas_call(
        functools.partial(_combine_kernel, tile=GATHER_TILE),
        grid_spec=pltpu.PrefetchScalarGridSpec(
            num_scalar_prefetch=1,
            grid=(t // GATHER_TILE,),
            in_specs=[
                pl.BlockSpec((GATHER_TILE, d), lambda i, pos: (i, 0)),
                pl.BlockSpec((GATHER_TILE, LANES), lambda i, pos: (i, 0)),
                pl.BlockSpec((1, d), lambda i, pos: (0, 0)),
                pl.BlockSpec(memory_space=pl.ANY),
            ],
            out_specs=pl.BlockSpec((GATHER_TILE, d), lambda i, pos: (i, 0)),
            scratch_shapes=[pltpu.VMEM((2, GATHER_TILE, d), F32), pltpu.SemaphoreType.DMA(())],
        ),
        out_shape=jax.ShapeDtypeStruct((t, d), F32),
        compiler_params=_params("arbitrary"),
        name="moe_combine",
    )(pos, x, route, g, y_sorted)


def _row(v):
    return v.reshape(1, -1).astype(F32)


def _pad_lanes(v):
    return jnp.pad(v.astype(F32), (0, LANES - v.shape[0])).reshape(1, LANES)


def kernel(x, mem, positions, mix_norm_g, ffn_norm_g, mem_norm_g, w_mem_kv, w_out, w_in_a, b_forget, w_q_b, lambda_q1, lambda_k1, lambda_q2, lambda_k2, subln_g, kv_norm_g, w_kv_shared, w_gate_up_dense, w_down_dense, w_router, w_gate_up_moe, w_down_moe, final_norm_g):
    batch, seq, d = x.shape
    mem_tokens = mem.shape[1]
    t = batch * seq
    assert w_in_a.shape[0] == 1 and w_q_b.shape[0] == 1 and w_out.shape[0] == 2
    fox_w = FOX_HEADS * HEAD_DIM
    diff_w = DIFF_HEADS * 2 * HEAD_DIM
    scale = HEAD_DIM ** -0.5

    xf = x.reshape(t, d)
    memf = mem.reshape(batch * mem_tokens, d)

    wa = w_in_a[0]
    w_fl = jnp.pad(wa[:, 3 * fox_w:3 * fox_w + FOX_HEADS], ((0, 0), (0, LANES - FOX_HEADS)))
    w_a = jnp.concatenate(
        [wa[:, :fox_w] * scale, wa[:, fox_w:3 * fox_w], wa[:, 3 * fox_w + FOX_HEADS:] * scale, w_fl],
        axis=1).astype(BF16)
    w_b = (w_q_b[0] * scale).astype(BF16)
    w_kv = w_kv_shared.astype(BF16)
    w_o = w_out.astype(BF16)
    w_mkv = w_mem_kv.astype(BF16)
    w_gu_d = w_gate_up_dense[0].astype(BF16)
    w_dn_d = w_down_dense[0].astype(BF16)
    w_gu_m = w_gate_up_moe[0].astype(BF16)
    w_dn_m = w_down_moe[0].astype(BF16)
    w_r = jnp.pad(w_router[0].astype(F32), ((0, 0), (0, LANES - N_EXPERTS)))

    qkvm, log_f = _proj_a(xf, _row(mix_norm_g[0]), w_a, _pad_lanes(b_forget[0]))
    cum, cum_t = _cumsum(log_f, batch, seq)
    y_self = _fox_attention(qkvm, cum, cum_t, batch, seq)
    mkv0 = _norm_matmul(memf, _row(mem_norm_g[0]), w_mkv[0], "mem_kv0")
    y_mem = _mem_attention(qkvm, 3 * fox_w // MEM_WIDTH, mkv0, batch, seq, mem_tokens)
    x1 = _out_proj(xf, y_self, y_mem, w_o[0])
    x2 = _dense_ffn(x1, _row(ffn_norm_g[0]), w_gu_d, w_dn_d)

    half = ROT_DIM // 2
    inv_freq = ROPE_THETA ** (-(jnp.arange(half, dtype=F32) * 2.0 / ROT_DIM))
    inv_lane = jnp.tile(inv_freq, LANES // half).reshape(1, LANES)
    cos_t, sin_t = _rope_tables(positions.reshape(t, 1).astype(jnp.int32), inv_lane)
    kv = _proj_rope(x2, _row(kv_norm_g), w_kv, cos_t, sin_t, diff_w, "proj_kv")
    qb = _proj_rope(x2, _row(mix_norm_g[1]), w_b, cos_t, sin_t, diff_w, "proj_b")

    lam_init = 0.8 - 0.6 * math.exp(-0.3 * 1)
    lam_rows = jnp.concatenate(
        [jnp.pad(v[0].astype(F32), (0, LANES - HEAD_DIM)).reshape(1, LANES)
         for v in (lambda_q1, lambda_k1, lambda_q2, lambda_k2)]
        + [jnp.zeros((4, LANES), F32)], axis=0)
    y_self = _diff_attention(lam_rows, _row(subln_g[0]), qb, kv, batch, seq, lam_init)
    mkv1 = _norm_matmul(memf, _row(mem_norm_g[1]), w_mkv[1], "mem_kv1")
    y_mem = _mem_attention(qb, diff_w // MEM_WIDTH, mkv1, batch, seq, mem_tokens)
    x3 = _out_proj(x2, y_self, y_mem, w_o[1])

    tri = jnp.tril(jnp.ones((ROW_TILE, ROW_TILE), BF16), -1)
    hp, route, cnt = _moe_pre(x3, _row(ffn_norm_g[1]), w_r, tri)
    tm = MOE_ROW_TILE
    n_tiles = 2 * t // tm + N_EXPERTS
    counts = cnt[0, :N_EXPERTS].astype(jnp.int32)
    tiles_per = (counts + tm - 1) // tm
    tile_end = jnp.cumsum(tiles_per)
    group_off = (tile_end - tiles_per) * tm
    n_active = tile_end[-1:]
    tile_id = jnp.minimum(jnp.arange(n_tiles, dtype=jnp.int32), n_active - 1)
    tile_expert = jnp.sum(tile_id[:, None] >= tile_end[None, :], axis=1).astype(jnp.int32)
    idx = route[:, 0:2].astype(jnp.int32)
    rank = route[:, 2:4].astype(jnp.int32)
    pos = (group_off[idx] + rank).reshape(2 * t)

    xp = _dispatch(pos, hp, n_tiles * tm)
    y_sorted = _moe_experts(tile_expert, n_active.astype(jnp.int32), xp, w_gu_m, w_dn_m, n_tiles)
    out = _combine(pos, x3, route, _row(final_norm_g), y_sorted)
    return out.reshape(batch, seq, d)
```

```python
import functools
import math

import jax
import jax.numpy as jnp
from jax import lax
from jax.experimental import pallas as pl
from jax.experimental.pallas import tpu as pltpu

F32 = jnp.float32
BF16 = jnp.bfloat16

HEAD_DIM = 64
LANES = 128
CHUNK_SHIFT = 6
FOX_HEADS = 12
DIFF_HEADS = 6
MEM_HEADS = 4
MEM_WIDTH = MEM_HEADS * HEAD_DIM
ROPE_THETA = 500000.0
ROT_DIM = HEAD_DIM // 4
N_EXPERTS = 8
EPS = 1e-5
NEG = -1e30
VMEM_LIMIT = 48 * 1024 * 1024

ROW_TILE = 512
FFN_ROW_TILE = 1024
FFN_COL_TILE = 256
ATT_TILE = 256
ROW_CHUNK = 16
MEM_Q_TILE = 512
MOE_ROW_TILE = 1024
MOE_COL_TILE = 512
GATHER_TILE = 256


def _params(*sem):
    return pltpu.CompilerParams(dimension_semantics=sem, vmem_limit_bytes=VMEM_LIMIT)


def _rms(x, g):
    ms = jnp.mean(x * x, axis=-1, keepdims=True)
    return x * lax.rsqrt(ms + EPS) * g


def _nt_dot(a, b):
    return lax.dot_general(a, b, (((1,), (1,)), ((), ())), preferred_element_type=F32)


def _dot(a, b):
    return jnp.dot(a, b, preferred_element_type=F32)


def _proj_a_kernel(x_ref, g_ref, w_ref, b_ref, o_ref, lf_ref, *, n_main, col_chunk):
    h = _rms(x_ref[...], g_ref[...]).astype(BF16)
    for c in range(n_main // col_chunk):
        cs = slice(c * col_chunk, (c + 1) * col_chunk)
        o_ref[:, cs] = _dot(h, w_ref[:, cs]).astype(BF16)
    z = _dot(h, w_ref[:, n_main:]) + b_ref[...]
    lf_ref[...] = jnp.minimum(z, 0.0) - jnp.log1p(jnp.exp(-jnp.abs(z)))


def _proj_a(x, g, w, b_pad):
    t, d = x.shape
    n_all = w.shape[1]
    n_main = n_all - LANES
    return pl.pallas_call(
        functools.partial(_proj_a_kernel, n_main=n_main, col_chunk=512),
        grid=(t // ROW_TILE,),
        in_specs=[
            pl.BlockSpec((ROW_TILE, d), lambda i: (i, 0)),
            pl.BlockSpec((1, d), lambda i: (0, 0)),
            pl.BlockSpec((d, n_all), lambda i: (0, 0)),
            pl.BlockSpec((1, LANES), lambda i: (0, 0)),
        ],
        out_specs=[
            pl.BlockSpec((ROW_TILE, n_main), lambda i: (i, 0)),
            pl.BlockSpec((ROW_TILE, LANES), lambda i: (i, 0)),
        ],
        out_shape=[
            jax.ShapeDtypeStruct((t, n_main), BF16),
            jax.ShapeDtypeStruct((t, LANES), F32),
        ],
        compiler_params=_params("parallel"),
        name="proj_a",
    )(x, g, w, b_pad)


def _cumsum_kernel(lf_ref, ct_ref):
    x = lf_ref[...]
    s = x.shape[0]
    row = lax.broadcasted_iota(jnp.int32, x.shape, 0)
    sh = 1
    while sh < s:
        x = x + jnp.where(row >= sh, pltpu.roll(x, sh, 0), 0.0)
        sh *= 2
    ct_ref[0] = x.T


def _cumsum(lf, batch, seq):
    return pl.pallas_call(
        _cumsum_kernel,
        grid=(batch,),
        in_specs=[pl.BlockSpec((seq, LANES), lambda b: (b, 0))],
        out_specs=pl.BlockSpec((1, LANES, seq), lambda b: (b, 0, 0)),
        out_shape=jax.ShapeDtypeStruct((batch, LANES, seq), F32),
        compiler_params=_params("parallel"),
        name="cumsum",
    )(lf)


def _softmax_rows(s_ref, p_ref, m_ref, l_ref, hi, tq, col_bias, keep_fn):
    lo = hi - tq
    shape = (ROW_CHUNK, LANES)

    def visibility(r, c):
        r0, c0 = r * ROW_CHUNK, c * LANES - lo
        if c0 < 0 or keep_fn(r0, c0 + LANES - 1):
            return "all"
        if not keep_fn(r0 + ROW_CHUNK - 1, c0):
            return "none"
        ri = lax.broadcasted_iota(jnp.int32, shape, 0) + r0
        ci = lax.broadcasted_iota(jnp.int32, shape, 1) + c0
        return keep_fn(ri, ci)

    for r in range(tq // ROW_CHUNK):
        rows = slice(r * ROW_CHUNK, (r + 1) * ROW_CHUNK)
        m_acc = None
        for c in range(hi // LANES):
            cols = slice(c * LANES, (c + 1) * LANES)
            vis = visibility(r, c)
            if isinstance(vis, str) and vis == "none":
                continue
            t = s_ref[rows, cols]
            if col_bias is not None:
                t = t - col_bias[:, cols]
                s_ref[rows, cols] = t
            if not isinstance(vis, str):
                t = jnp.where(vis, t, NEG)
            m_acc = t if m_acc is None else jnp.maximum(m_acc, t)
        m_ref[rows, :] = jnp.broadcast_to(jnp.max(m_acc, axis=1, keepdims=True), shape)
    for r in range(tq // ROW_CHUNK):
        rows = slice(r * ROW_CHUNK, (r + 1) * ROW_CHUNK)
        m = m_ref[rows, :]
        l_acc = jnp.zeros(shape, F32)
        for c in range(hi // LANES):
            cols = slice(c * LANES, (c + 1) * LANES)
            vis = visibility(r, c)
            if isinstance(vis, str) and vis == "none":
                p_ref[rows, cols] = jnp.zeros(shape, BF16)
                continue
            t = s_ref[rows, cols]
            if not isinstance(vis, str):
                t = jnp.where(vis, t, NEG)
            p = jnp.exp(t - m)
            l_acc = l_acc + p
            p_ref[rows, cols] = p.astype(BF16)
        l_ref[rows, :] = l_acc


def _split_heads(q):
    low = lax.broadcasted_iota(jnp.int32, (1, LANES), 1) < HEAD_DIM
    zero = jnp.zeros_like(q)
    return low, (jnp.where(low, q, zero), jnp.where(low, zero, q))


def _fox_kernel(q_ref, k_ref, v_ref, ct_ref, o_ref, s_sc, p_sc, m_sc, l_sc, *, tq):
    seq = q_ref.shape[0]
    sub = (2 * pl.program_id(1)) & 7

    def keep_fn(row, col):
        return col <= row

    for qi in range(seq // tq):
        hi = (qi + 1) * tq
        low, qh = _split_heads(q_ref[qi * tq:hi, :])
        res = []
        for hh in range(2):
            s_sc[hh, :, :hi] = _nt_dot(qh[hh], k_ref[:hi, :])
            cj = ct_ref[0, pl.ds(sub + hh, 1), :hi]
            _softmax_rows(s_sc.at[hh], p_sc.at[hh], m_sc.at[hh], l_sc.at[hh], hi, tq, cj, keep_fn)
            l = jnp.sum(l_sc[hh], axis=1, keepdims=True)
            res.append(_dot(p_sc[hh, :, :hi], v_ref[:hi, :]) / l)
        o_ref[qi * tq:hi, :] = jnp.where(low, res[0], res[1]).astype(BF16)


def _attention_scratch(tq, seq):
    return [pltpu.VMEM((2, tq, seq), F32), pltpu.VMEM((2, tq, seq), BF16),
            pltpu.VMEM((2, tq, LANES), F32), pltpu.VMEM((2, tq, LANES), F32)]


def _fox_attention(qkvm, cum_t, batch, seq):
    t = batch * seq
    npair = FOX_HEADS // 2
    return pl.pallas_call(
        functools.partial(_fox_kernel, tq=ATT_TILE),
        grid=(batch, npair),
        in_specs=[
            pl.BlockSpec((seq, LANES), lambda b, p: (b, p)),
            pl.BlockSpec((seq, LANES), lambda b, p: (b, npair + p)),
            pl.BlockSpec((seq, LANES), lambda b, p: (b, 2 * npair + p)),
            pl.BlockSpec((1, 8, seq), lambda b, p: (b, p // 4, 0)),
        ],
        out_specs=pl.BlockSpec((seq, LANES), lambda b, p: (b, p)),
        out_shape=jax.ShapeDtypeStruct((t, npair * LANES), BF16),
        scratch_shapes=_attention_scratch(ATT_TILE, seq),
        compiler_params=_params("parallel", "parallel"),
        name="fox_attention",
    )(qkvm, qkvm, qkvm, cum_t)


def _mem_kernel(q_ref, mk_ref, mv_ref, o_ref):
    q = q_ref[...]
    mk = mk_ref[...]
    mv = mv_ref[...]
    lane = lax.broadcasted_iota(jnp.int32, (1, MEM_WIDTH), 1)
    zero = jnp.zeros_like(q)
    out = jnp.zeros(q.shape, F32)
    for h in range(MEM_HEADS):
        hm = (lane >= h * HEAD_DIM) & (lane < (h + 1) * HEAD_DIM)
        s = _nt_dot(jnp.where(hm, q, zero), mk)
        p = jnp.exp(s - jnp.max(s, axis=1, keepdims=True))
        l = jnp.sum(p, axis=1, keepdims=True)
        out = jnp.where(hm, _dot(p.astype(BF16), mv) / l, out)
    o_ref[...] = out.astype(BF16)


def _mem_attention(q_arr, q_col_block, mkv, batch, seq, mem_tokens):
    nq = seq // MEM_Q_TILE
    return pl.pallas_call(
        _mem_kernel,
        grid=(batch, nq),
        in_specs=[
            pl.BlockSpec((MEM_Q_TILE, MEM_WIDTH), lambda b, i: (b * nq + i, q_col_block)),
            pl.BlockSpec((mem_tokens, MEM_WIDTH), lambda b, i: (b, 0)),
            pl.BlockSpec((mem_tokens, MEM_WIDTH), lambda b, i: (b, 1)),
        ],
        out_specs=pl.BlockSpec((MEM_Q_TILE, MEM_WIDTH), lambda b, i: (b * nq + i, 0)),
        out_shape=jax.ShapeDtypeStruct((batch * seq, MEM_WIDTH), BF16),
        compiler_params=_params("parallel", "parallel"),
        name="mem_attention",
    )(q_arr, mkv, mkv)


def _norm_matmul_kernel(x_ref, g_ref, w_ref, o_ref):
    h = _rms(x_ref[...], g_ref[...]).astype(BF16)
    o_ref[...] = _dot(h, w_ref[...]).astype(o_ref.dtype)


def _norm_matmul(x, g, w, name):
    t, d = x.shape
    n = w.shape[1]
    return pl.pallas_call(
        _norm_matmul_kernel,
        grid=(t // ROW_TILE,),
        in_specs=[
            pl.BlockSpec((ROW_TILE, d), lambda i: (i, 0)),
            pl.BlockSpec((1, d), lambda i: (0, 0)),
            pl.BlockSpec((d, n), lambda i: (0, 0)),
        ],
        out_specs=pl.BlockSpec((ROW_TILE, n), lambda i: (i, 0)),
        out_shape=jax.ShapeDtypeStruct((t, n), BF16),
        compiler_params=_params("parallel"),
        name=name,
    )(x, g, w)


def _out_proj_kernel(x_ref, ys_ref, ym_ref, w_ref, o_ref, *, self_width):
    acc = _dot(ys_ref[...], w_ref[:self_width, :]) + _dot(ym_ref[...], w_ref[self_width:, :])
    o_ref[...] = x_ref[...] + acc


def _out_proj(x, y_self, y_mem, w):
    t, d = x.shape
    sw = y_self.shape[1]
    return pl.pallas_call(
        functools.partial(_out_proj_kernel, self_width=sw),
        grid=(t // ROW_TILE,),
        in_specs=[
            pl.BlockSpec((ROW_TILE, d), lambda i: (i, 0)),
            pl.BlockSpec((ROW_TILE, sw), lambda i: (i, 0)),
            pl.BlockSpec((ROW_TILE, MEM_WIDTH), lambda i: (i, 0)),
            pl.BlockSpec(w.shape, lambda i: (0, 0)),
        ],
        out_specs=pl.BlockSpec((ROW_TILE, d), lambda i: (i, 0)),
        out_shape=jax.ShapeDtypeStruct((t, d), F32),
        compiler_params=_params("parallel"),
        name="out_proj",
    )(x, y_self, y_mem, w)


def _swiglu_hidden(g, u):
    return (g * jax.nn.sigmoid(g) * u).astype(BF16)


def _ffn_kernel(x_ref, g_ref, wg_ref, wu_ref, wd_ref, o_ref, h_sc):
    @pl.when(pl.program_id(1) == 0)
    def _():
        x = x_ref[...]
        h_sc[...] = _rms(x, g_ref[...]).astype(BF16)
        o_ref[...] = x

    h = h_sc[...]
    hid = _swiglu_hidden(_dot(h, wg_ref[...]), _dot(h, wu_ref[...]))
    o_ref[...] += _dot(hid, wd_ref[...])


def _dense_ffn(x, g, w_gate_up, w_down):
    t, d = x.shape
    ff = w_down.shape[0]
    nf = ff // FFN_COL_TILE
    return pl.pallas_call(
        _ffn_kernel,
        grid=(t // FFN_ROW_TILE, nf),
        in_specs=[
            pl.BlockSpec((FFN_ROW_TILE, d), lambda i, f: (i, 0)),
            pl.BlockSpec((1, d), lambda i, f: (0, 0)),
            pl.BlockSpec((d, FFN_COL_TILE), lambda i, f: (0, f)),
            pl.BlockSpec((d, FFN_COL_TILE), lambda i, f: (0, nf + f)),
            pl.BlockSpec((FFN_COL_TILE, d), lambda i, f: (f, 0)),
        ],
        out_specs=pl.BlockSpec((FFN_ROW_TILE, d), lambda i, f: (i, 0)),
        out_shape=jax.ShapeDtypeStruct((t, d), F32),
        scratch_shapes=[pltpu.VMEM((FFN_ROW_TILE, d), BF16)],
        compiler_params=_params("parallel", "arbitrary"),
        name="dense_ffn",
    )(x, g, w_gate_up, w_gate_up, w_down)


def _rope_kernel(pos_ref, inv_ref, cos_ref, sin_ref):
    pos = pos_ref[...].astype(F32)
    ang = pos * inv_ref[...]
    lane = lax.broadcasted_iota(jnp.int32, ang.shape, 1)
    rot = (lane & (HEAD_DIM - 1)) < ROT_DIM
    first = (lane & (ROT_DIM - 1)) < (ROT_DIM // 2)
    sn = jnp.sin(ang)
    cos_ref[...] = jnp.where(rot, jnp.cos(ang), 1.0)
    sin_ref[...] = jnp.where(rot, jnp.where(first, -sn, sn), 0.0)


def _rope_tables(pos_col, inv_lane):
    t = pos_col.shape[0]
    return pl.pallas_call(
        _rope_kernel,
        grid=(t // ROW_TILE,),
        in_specs=[
            pl.BlockSpec((ROW_TILE, 1), lambda i: (i, 0)),
            pl.BlockSpec((1, LANES), lambda i: (0, 0)),
        ],
        out_specs=[pl.BlockSpec((ROW_TILE, LANES), lambda i: (i, 0))] * 2,
        out_shape=[jax.ShapeDtypeStruct((t, LANES), F32)] * 2,
        compiler_params=_params("parallel"),
        name="rope_tables",
    )(pos_col, inv_lane)


def _proj_rope_kernel(x_ref, g_ref, w_ref, cos_ref, sin_ref, o_ref, *, n_rope, col_chunk):
    h = _rms(x_ref[...], g_ref[...]).astype(BF16)
    cos = cos_ref[...]
    sin = sin_ref[...]
    lane = lax.broadcasted_iota(jnp.int32, (1, LANES), 1)
    first = (lane & (ROT_DIM - 1)) < (ROT_DIM // 2)
    half = ROT_DIM // 2
    n = w_ref.shape[1]
    for c in range(n // col_chunk):
        a = _dot(h, w_ref[:, c * col_chunk:(c + 1) * col_chunk])
        for s in range(col_chunk // LANES):
            col = c * col_chunk + s * LANES
            blk = a[:, s * LANES:(s + 1) * LANES]
            if col < n_rope:
                partner = jnp.where(first, pltpu.roll(blk, LANES - half, 1),
                                    pltpu.roll(blk, half, 1))
                blk = blk * cos + partner * sin
            o_ref[:, col:col + LANES] = blk.astype(BF16)


def _proj_rope(x, g, w, cos, sin, n_rope, name):
    t, d = x.shape
    n = w.shape[1]
    return pl.pallas_call(
        functools.partial(_proj_rope_kernel, n_rope=n_rope, col_chunk=256),
        grid=(t // ROW_TILE,),
        in_specs=[
            pl.BlockSpec((ROW_TILE, d), lambda i: (i, 0)),
            pl.BlockSpec((1, d), lambda i: (0, 0)),
            pl.BlockSpec((d, n), lambda i: (0, 0)),
            pl.BlockSpec((ROW_TILE, LANES), lambda i: (i, 0)),
            pl.BlockSpec((ROW_TILE, LANES), lambda i: (i, 0)),
        ],
        out_specs=pl.BlockSpec((ROW_TILE, n), lambda i: (i, 0)),
        out_shape=jax.ShapeDtypeStruct((t, n), BF16),
        compiler_params=_params("parallel"),
        name=name,
    )(x, g, w, cos, sin)


def _diff_kernel(lam_ref, sg_ref, q_ref, k_ref, v_ref, o_ref, s_sc, p_sc, m_sc, l_sc, *, tq,
                 lam_init):
    seq = q_ref.shape[0]
    lp = lam_ref[...]
    lam = (jnp.exp(jnp.sum(lp[0:1] * lp[1:2], axis=1, keepdims=True))
           - jnp.exp(jnp.sum(lp[2:3] * lp[3:4], axis=1, keepdims=True)) + lam_init)

    def keep_fn(row, col):
        return (col >> CHUNK_SHIFT) <= (row >> CHUNK_SHIFT)

    for qi in range(seq // tq):
        hi = (qi + 1) * tq
        _, qh = _split_heads(q_ref[qi * tq:hi, :])
        res = []
        for hh in range(2):
            s_sc[hh, :, :hi] = _nt_dot(qh[hh], k_ref[:hi, :])
            _softmax_rows(s_sc.at[hh], p_sc.at[hh], m_sc.at[hh], l_sc.at[hh], hi, tq, None, keep_fn)
            l = jnp.sum(l_sc[hh], axis=1, keepdims=True)
            res.append(_dot(p_sc[hh, :, :hi], v_ref[:hi, :]) / l)
        o = res[0] - lam * res[1]
        o = _rms(o, sg_ref[...]) * (1.0 - lam_init)
        o_ref[qi * tq:hi, :] = o.astype(BF16)


def _diff_attention(lam_rows, subln_g, qb, kv, batch, seq, lam_init):
    t = batch * seq
    return pl.pallas_call(
        functools.partial(_diff_kernel, tq=ATT_TILE, lam_init=lam_init),
        grid=(batch, DIFF_HEADS),
        in_specs=[
            pl.BlockSpec((8, LANES), lambda b, h: (0, 0)),
            pl.BlockSpec((1, LANES), lambda b, h: (0, 0)),
            pl.BlockSpec((seq, LANES), lambda b, h: (b, h)),
            pl.BlockSpec((seq, LANES), lambda b, h: (b, h)),
            pl.BlockSpec((seq, LANES), lambda b, h: (b, DIFF_HEADS + h)),
        ],
        out_specs=pl.BlockSpec((seq, LANES), lambda b, h: (b, h)),
        out_shape=jax.ShapeDtypeStruct((t, DIFF_HEADS * LANES), BF16),
        scratch_shapes=_attention_scratch(ATT_TILE, seq),
        compiler_params=_params("parallel", "parallel"),
        name="diff_attention",
    )(lam_rows, subln_g, qb, kv, kv)


def _moe_pre_kernel(x_ref, g_ref, wr_ref, tri_ref, hp_ref, route_ref, cnt_ref, run_sc):
    @pl.when(pl.program_id(0) == 0)
    def _():
        run_sc[...] = jnp.zeros_like(run_sc)

    hf = _rms(x_ref[...], g_ref[...])
    d = hf.shape[1]
    hb = hf.astype(BF16).astype(F32)
    bits = pltpu.bitcast(hb, jnp.uint32)
    hp_ref[...] = bits[:, :d // 2] | (bits[:, d // 2:] >> 16)

    logits = jnp.dot(hf, wr_ref[...], preferred_element_type=F32,
                     precision=lax.Precision.HIGHEST)
    lane = lax.broadcasted_iota(jnp.int32, logits.shape, 1)
    lanef = lane.astype(F32)
    lg = jnp.where(lane < N_EXPERTS, logits, NEG)
    v1 = jnp.max(lg, axis=1, keepdims=True)
    i1 = jnp.min(jnp.where(lg == v1, lanef, float(LANES)), axis=1, keepdims=True)
    lg2 = jnp.where(lanef == i1, NEG, lg)
    v2 = jnp.max(lg2, axis=1, keepdims=True)
    i2 = jnp.min(jnp.where(lg2 == v2, lanef, float(LANES)), axis=1, keepdims=True)
    e = jnp.exp(v2 - v1)
    g1 = 1.0 / (1.0 + e)
    g2 = e / (1.0 + e)

    oh1 = lanef == i1
    oh2 = lanef == i2
    oh = jnp.where(oh1 | oh2, 1.0, 0.0)
    before = _dot(tri_ref[...], oh.astype(BF16)) + run_sc[...]
    r1 = jnp.sum(jnp.where(oh1, before, 0.0), axis=1, keepdims=True)
    r2 = jnp.sum(jnp.where(oh2, before, 0.0), axis=1, keepdims=True)
    run_sc[...] += jnp.sum(oh, axis=0, keepdims=True)
    cnt_ref[...] = run_sc[...]

    route = jnp.where(lane == 0, i1, 0.0)
    for ln, val in ((1, i2), (2, r1), (3, r2), (4, g1), (5, g2)):
        route = jnp.where(lane == ln, val, route)
    route_ref[...] = route


def _moe_pre(x, g, w_router_pad, tri):
    t, d = x.shape
    return pl.pallas_call(
        _moe_pre_kernel,
        grid=(t // ROW_TILE,),
        in_specs=[
            pl.BlockSpec((ROW_TILE, d), lambda i: (i, 0)),
            pl.BlockSpec((1, d), lambda i: (0, 0)),
            pl.BlockSpec((d, LANES), lambda i: (0, 0)),
            pl.BlockSpec((ROW_TILE, ROW_TILE), lambda i: (0, 0)),
        ],
        out_specs=[
            pl.BlockSpec((ROW_TILE, d // 2), lambda i: (i, 0)),
            pl.BlockSpec((ROW_TILE, LANES), lambda i: (i, 0)),
            pl.BlockSpec((1, LANES), lambda i: (0, 0)),
        ],
        out_shape=[
            jax.ShapeDtypeStruct((t, d // 2), jnp.uint32),
            jax.ShapeDtypeStruct((t, LANES), F32),
            jax.ShapeDtypeStruct((1, LANES), F32),
        ],
        scratch_shapes=[pltpu.VMEM((1, LANES), F32)],
        compiler_params=_params("arbitrary"),
        name="moe_pre",
    )(x, g, w_router_pad, tri)


def _dispatch_kernel(pos_ref, hp_ref, init_ref, out_ref, sem, *, tile):
    del init_ref
    base = pl.program_id(0) * tile * 2

    def row_copy(r, k):
        dst = pos_ref[base + 2 * r + k]
        return pltpu.make_async_copy(hp_ref.at[pl.ds(r, 1)], out_ref.at[pl.ds(dst, 1)], sem)

    def start(r, c):
        row_copy(r, 0).start(priority=0)
        row_copy(r, 1).start(priority=1)
        return c

    def wait(r, c):
        row_copy(r, 0).wait()
        row_copy(r, 1).wait()
        return c

    lax.fori_loop(0, tile, start, 0, unroll=8)
    lax.fori_loop(0, tile, wait, 0, unroll=8)


def _dispatch(pos, hp, n_slots):
    t, w = hp.shape
    init = jnp.zeros((n_slots, w), hp.dtype)
    return pl.pallas_call(
        functools.partial(_dispatch_kernel, tile=GATHER_TILE),
        grid_spec=pltpu.PrefetchScalarGridSpec(
            num_scalar_prefetch=1,
            grid=(t // GATHER_TILE,),
            in_specs=[
                pl.BlockSpec((GATHER_TILE, w), lambda i, pos: (i, 0)),
                pl.BlockSpec(memory_space=pl.ANY),
            ],
            out_specs=pl.BlockSpec(memory_space=pl.ANY),
            scratch_shapes=[pltpu.SemaphoreType.DMA(())],
        ),
        out_shape=jax.ShapeDtypeStruct((n_slots, w), hp.dtype),
        input_output_aliases={2: 0},
        compiler_params=_params("arbitrary"),
        name="moe_dispatch",
    )(pos, hp, init)


def _moe_kernel(te_ref, na_ref, xp_ref, wg_ref, wu_ref, wd_ref, o_ref, xa_sc, xb_sc):
    del te_ref
    i = pl.program_id(0)
    f = pl.program_id(1)

    @pl.when(f == 0)
    def _():
        o_ref[...] = jnp.zeros_like(o_ref)

    @pl.when(i < na_ref[0])
    def _():
        @pl.when(f == 0)
        def _():
            pk = xp_ref[...]
            xa_sc[...] = pltpu.bitcast(pk & jnp.uint32(0xFFFF0000), F32).astype(BF16)
            xb_sc[...] = pltpu.bitcast(pk << 16, F32).astype(BF16)

        xa = xa_sc[...]
        xb = xb_sc[...]
        half = xa.shape[1]
        wg = wg_ref[0].astype(BF16)
        wu = wu_ref[0].astype(BF16)
        g = _dot(xa, wg[:half]) + _dot(xb, wg[half:])
        u = _dot(xa, wu[:half]) + _dot(xb, wu[half:])
        o_ref[...] += _dot(_swiglu_hidden(g, u), wd_ref[0].astype(BF16))


def _moe_experts(tile_expert, n_active, xp, w_gate_up, w_down, n_tiles):
    d = w_down.shape[2]
    ff = w_down.shape[1]
    ct = MOE_COL_TILE
    nf = ff // ct
    tm = MOE_ROW_TILE

    def row(i, f, te, na):
        return (jnp.minimum(i, na[0] - 1), 0)

    def col(i, f, na):
        return jnp.where(i < na[0], f, nf - 1)

    return pl.pallas_call(
        _moe_kernel,
        grid_spec=pltpu.PrefetchScalarGridSpec(
            num_scalar_prefetch=2,
            grid=(n_tiles, nf),
            in_specs=[
                pl.BlockSpec((tm, d // 2), row),
                pl.BlockSpec((1, d, ct), lambda i, f, te, na: (te[i], 0, col(i, f, na))),
                pl.BlockSpec((1, d, ct), lambda i, f, te, na: (te[i], 0, nf + col(i, f, na))),
                pl.BlockSpec((1, ct, d), lambda i, f, te, na: (te[i], col(i, f, na), 0)),
            ],
            out_specs=pl.BlockSpec((tm, d), lambda i, f, te, na: (i, 0)),
            scratch_shapes=[pltpu.VMEM((tm, d // 2), BF16), pltpu.VMEM((tm, d // 2), BF16)],
        ),
        out_shape=jax.ShapeDtypeStruct((n_tiles * tm, d), F32),
        compiler_params=_params("arbitrary", "arbitrary"),
        name="moe_experts",
    )(tile_expert, n_active, xp, w_gate_up, w_gate_up, w_down)


def _combine_kernel(pos_ref, x_ref, route_ref, g_ref, y_ref, o_ref, buf, sem, *, tile):
    base = pl.program_id(0) * tile * 2

    def row_copy(r, k):
        src = pos_ref[base + 2 * r + k]
        return pltpu.make_async_copy(y_ref.at[pl.ds(src, 1)], buf.at[k, pl.ds(r, 1)], sem)

    def start(r, c):
        row_copy(r, 0).start(priority=0)
        row_copy(r, 1).start(priority=1)
        return c

    def wait(r, c):
        row_copy(r, 0).wait()
        row_copy(r, 1).wait()
        return c

    lax.fori_loop(0, tile, start, 0, unroll=8)
    lax.fori_loop(0, tile, wait, 0, unroll=8)
    route = route_ref[...]
    y = x_ref[...] + route[:, 4:5] * buf[0] + route[:, 5:6] * buf[1]
    o_ref[...] = _rms(y, g_ref[...])


def _combine(pos, x, route, g, y_sorted):
    t, d = x.shape
    return pl.pallas_call(
        functools.partial(_combine_kernel, tile=GATHER_TILE),
        grid_spec=pltpu.PrefetchScalarGridSpec(
            num_scalar_prefetch=1,
            grid=(t // GATHER_TILE,),
            in_specs=[
                pl.BlockSpec((GATHER_TILE, d), lambda i, pos: (i, 0)),
                pl.BlockSpec((GATHER_TILE, LANES), lambda i, pos: (i, 0)),
                pl.BlockSpec((1, d), lambda i, pos: (0, 0)),
                pl.BlockSpec(memory_space=pl.ANY),
            ],
            out_specs=pl.BlockSpec((GATHER_TILE, d), lambda i, pos: (i, 0)),
            scratch_shapes=[pltpu.VMEM((2, GATHER_TILE, d), F32), pltpu.SemaphoreType.DMA(())],
        ),
        out_shape=jax.ShapeDtypeStruct((t, d), F32),
        compiler_params=_params("arbitrary"),
        name="moe_combine",
    )(pos, x, route, g, y_sorted)


def _row(v):
    return v.reshape(1, -1).astype(F32)


def _pad_lanes(v):
    return jnp.pad(v.astype(F32), (0, LANES - v.shape[0])).reshape(1, LANES)


def kernel(x, mem, positions, mix_norm_g, ffn_norm_g, mem_norm_g, w_mem_kv, w_out, w_in_a, b_forget, w_q_b, lambda_q1, lambda_k1, lambda_q2, lambda_k2, subln_g, kv_norm_g, w_kv_shared, w_gate_up_dense, w_down_dense, w_router, w_gate_up_moe, w_down_moe, final_norm_g):
    batch, seq, d = x.shape
    mem_tokens = mem.shape[1]
    t = batch * seq
    assert w_in_a.shape[0] == 1 and w_q_b.shape[0] == 1 and w_out.shape[0] == 2
    fox_w = FOX_HEADS * HEAD_DIM
    diff_w = DIFF_HEADS * 2 * HEAD_DIM
    scale = HEAD_DIM ** -0.5

    xf = x.reshape(t, d)
    memf = mem.reshape(batch * mem_tokens, d)

    wa = w_in_a[0]
    w_fl = jnp.pad(wa[:, 3 * fox_w:3 * fox_w + FOX_HEADS], ((0, 0), (0, LANES - FOX_HEADS)))
    w_a = jnp.concatenate(
        [wa[:, :fox_w] * scale, wa[:, fox_w:3 * fox_w], wa[:, 3 * fox_w + FOX_HEADS:] * scale, w_fl],
        axis=1).astype(BF16)
    w_b = (w_q_b[0] * scale).astype(BF16)
    w_kv = w_kv_shared.astype(BF16)
    w_o = w_out.astype(BF16)
    w_mkv = w_mem_kv.astype(BF16)
    w_gu_d = w_gate_up_dense[0].astype(BF16)
    w_dn_d = w_down_dense[0].astype(BF16)
    w_gu_m = w_gate_up_moe[0]
    w_dn_m = w_down_moe[0]
    w_r = jnp.pad(w_router[0].astype(F32), ((0, 0), (0, LANES - N_EXPERTS)))

    qkvm, log_f = _proj_a(xf, _row(mix_norm_g[0]), w_a, _pad_lanes(b_forget[0]))
    cum_t = _cumsum(log_f, batch, seq)
    y_self = _fox_attention(qkvm, cum_t, batch, seq)
    mkv0 = _norm_matmul(memf, _row(mem_norm_g[0]), w_mkv[0], "mem_kv0")
    y_mem = _mem_attention(qkvm, 3 * fox_w // MEM_WIDTH, mkv0, batch, seq, mem_tokens)
    x1 = _out_proj(xf, y_self, y_mem, w_o[0])
    x2 = _dense_ffn(x1, _row(ffn_norm_g[0]), w_gu_d, w_dn_d)

    half = ROT_DIM // 2
    inv_freq = ROPE_THETA ** (-(jnp.arange(half, dtype=F32) * 2.0 / ROT_DIM))
    inv_lane = jnp.tile(inv_freq, LANES // half).reshape(1, LANES)
    cos_t, sin_t = _rope_tables(positions.reshape(t, 1).astype(jnp.int32), inv_lane)
    kv = _proj_rope(x2, _row(kv_norm_g), w_kv, cos_t, sin_t, diff_w, "proj_kv")
    qb = _proj_rope(x2, _row(mix_norm_g[1]), w_b, cos_t, sin_t, diff_w, "proj_b")

    lam_init = 0.8 - 0.6 * math.exp(-0.3 * 1)
    lam_rows = jnp.concatenate(
        [jnp.pad(v[0].astype(F32), (0, LANES - HEAD_DIM)).reshape(1, LANES)
         for v in (lambda_q1, lambda_k1, lambda_q2, lambda_k2)]
        + [jnp.zeros((4, LANES), F32)], axis=0)
    y_self = _diff_attention(lam_rows, _row(subln_g[0]), qb, kv, batch, seq, lam_init)
    mkv1 = _norm_matmul(memf, _row(mem_norm_g[1]), w_mkv[1], "mem_kv1")
    y_mem = _mem_attention(qb, diff_w // MEM_WIDTH, mkv1, batch, seq, mem_tokens)
    x3 = _out_proj(x2, y_self, y_mem, w_o[1])

    tri = jnp.tril(jnp.ones((ROW_TILE, ROW_TILE), BF16), -1)
    hp, route, cnt = _moe_pre(x3, _row(ffn_norm_g[1]), w_r, tri)
    tm = MOE_ROW_TILE
    n_tiles = 2 * t // tm + N_EXPERTS
    counts = cnt[0, :N_EXPERTS].astype(jnp.int32)
    tiles_per = (counts + tm - 1) // tm
    tile_end = jnp.cumsum(tiles_per)
    group_off = (tile_end - tiles_per) * tm
    n_active = tile_end[-1:]
    tile_id = jnp.minimum(jnp.arange(n_tiles, dtype=jnp.int32), n_active - 1)
    tile_expert = jnp.sum(tile_id[:, None] >= tile_end[None, :], axis=1).astype(jnp.int32)
    idx = route[:, 0:2].astype(jnp.int32)
    rank = route[:, 2:4].astype(jnp.int32)
    pos = (group_off[idx] + rank).reshape(2 * t)

    xp = _dispatch(pos, hp, n_tiles * tm)
    y_sorted = _moe_experts(tile_expert, n_active.astype(jnp.int32), xp, w_gu_m, w_dn_m, n_tiles)
    out = _combine(pos, x3, route, _row(final_norm_g), y_sorted)
    return out.reshape(batch, seq, d)
```

```python
import functools
import math

import jax
import jax.numpy as jnp
from jax import lax
from jax.experimental import pallas as pl
from jax.experimental.pallas import tpu as pltpu

F32 = jnp.float32
BF16 = jnp.bfloat16

HEAD_DIM = 64
LANES = 128
CHUNK_SHIFT = 6
FOX_HEADS = 12
DIFF_HEADS = 6
MEM_HEADS = 4
MEM_WIDTH = MEM_HEADS * HEAD_DIM
ROPE_THETA = 500000.0
ROT_DIM = HEAD_DIM // 4
N_EXPERTS = 8
EPS = 1e-5
NEG = -1e30
VMEM_LIMIT = 48 * 1024 * 1024

ROW_TILE = 512
FFN_ROW_TILE = 1024
FFN_COL_TILE = 256
ATT_TILE = 256
ROW_CHUNK = 16
ATT_SLOTS = 2
MEM_Q_TILE = 512
MOE_ROW_TILE = 1024
MOE_COL_TILE = 512
GATHER_TILE = 256


def _params(*sem):
    return pltpu.CompilerParams(dimension_semantics=sem, vmem_limit_bytes=VMEM_LIMIT)


def _rms(x, g):
    ms = jnp.mean(x * x, axis=-1, keepdims=True)
    return x * lax.rsqrt(ms + EPS) * g


def _nt_dot(a, b):
    return lax.dot_general(a, b, (((1,), (1,)), ((), ())), preferred_element_type=F32)


def _dot(a, b):
    return jnp.dot(a, b, preferred_element_type=F32)


def _proj_a_kernel(x_ref, g_ref, w_ref, b_ref, o_ref, lf_ref, *, n_main, col_chunk):
    h = _rms(x_ref[...], g_ref[...]).astype(BF16)
    for c in range(n_main // col_chunk):
        cs = slice(c * col_chunk, (c + 1) * col_chunk)
        o_ref[:, cs] = _dot(h, w_ref[:, cs]).astype(BF16)
    z = _dot(h, w_ref[:, n_main:]) + b_ref[...]
    lf_ref[...] = jnp.minimum(z, 0.0) - jnp.log1p(jnp.exp(-jnp.abs(z)))


def _proj_a(x, g, w, b_pad):
    t, d = x.shape
    n_all = w.shape[1]
    n_main = n_all - LANES
    return pl.pallas_call(
        functools.partial(_proj_a_kernel, n_main=n_main, col_chunk=512),
        grid=(t // ROW_TILE,),
        in_specs=[
            pl.BlockSpec((ROW_TILE, d), lambda i: (i, 0)),
            pl.BlockSpec((1, d), lambda i: (0, 0)),
            pl.BlockSpec((d, n_all), lambda i: (0, 0)),
            pl.BlockSpec((1, LANES), lambda i: (0, 0)),
        ],
        out_specs=[
            pl.BlockSpec((ROW_TILE, n_main), lambda i: (i, 0)),
            pl.BlockSpec((ROW_TILE, LANES), lambda i: (i, 0)),
        ],
        out_shape=[
            jax.ShapeDtypeStruct((t, n_main), BF16),
            jax.ShapeDtypeStruct((t, LANES), F32),
        ],
        compiler_params=_params("parallel"),
        name="proj_a",
    )(x, g, w, b_pad)


def _cumsum_kernel(lf_ref, ct_ref):
    x = lf_ref[...]
    s = x.shape[0]
    row = lax.broadcasted_iota(jnp.int32, x.shape, 0)
    sh = 1
    while sh < s:
        x = x + jnp.where(row >= sh, pltpu.roll(x, sh, 0), 0.0)
        sh *= 2
    ct_ref[0] = x.T


def _cumsum(lf, batch, seq):
    return pl.pallas_call(
        _cumsum_kernel,
        grid=(batch,),
        in_specs=[pl.BlockSpec((seq, LANES), lambda b: (b, 0))],
        out_specs=pl.BlockSpec((1, LANES, seq), lambda b: (b, 0, 0)),
        out_shape=jax.ShapeDtypeStruct((batch, LANES, seq), F32),
        compiler_params=_params("parallel"),
        name="cumsum",
    )(lf)


def _softmax_rows(s_ref, p_ref, m_ref, l_ref, hi, tq, col_bias, keep_fn):
    lo = hi - tq
    shape = (ROW_CHUNK, LANES)

    def visibility(r, c):
        r0, c0 = r * ROW_CHUNK, c * LANES - lo
        if c0 < 0 or keep_fn(r0, c0 + LANES - 1):
            return "all"
        if not keep_fn(r0 + ROW_CHUNK - 1, c0):
            return "none"
        ri = lax.broadcasted_iota(jnp.int32, shape, 0) + r0
        ci = lax.broadcasted_iota(jnp.int32, shape, 1) + c0
        return keep_fn(ri, ci)

    for r in range(tq // ROW_CHUNK):
        rows = slice(r * ROW_CHUNK, (r + 1) * ROW_CHUNK)
        m_acc = None
        for c in range(hi // LANES):
            cols = slice(c * LANES, (c + 1) * LANES)
            vis = visibility(r, c)
            if isinstance(vis, str) and vis == "none":
                continue
            t = s_ref[rows, cols]
            if col_bias is not None:
                t = t - col_bias[:, cols]
                s_ref[rows, cols] = t
            if not isinstance(vis, str):
                t = jnp.where(vis, t, NEG)
            m_acc = t if m_acc is None else jnp.maximum(m_acc, t)
        m_ref[rows, :] = jnp.broadcast_to(jnp.max(m_acc, axis=1, keepdims=True), shape)
    for r in range(tq // ROW_CHUNK):
        rows = slice(r * ROW_CHUNK, (r + 1) * ROW_CHUNK)
        m = m_ref[rows, :]
        l_acc = jnp.zeros(shape, F32)
        for c in range(hi // LANES):
            cols = slice(c * LANES, (c + 1) * LANES)
            vis = visibility(r, c)
            if isinstance(vis, str) and vis == "none":
                p_ref[rows, cols] = jnp.zeros(shape, BF16)
                continue
            t = s_ref[rows, cols]
            if not isinstance(vis, str):
                t = jnp.where(vis, t, NEG)
            p = jnp.exp(t - m)
            l_acc = l_acc + p
            p_ref[rows, cols] = p.astype(BF16)
        l_ref[rows, :] = l_acc


def _attention_sweep(q_ref, k_ref, v_ref, scratch, tq, col_bias_fn, keep_fn, emit):
    seq = q_ref.shape[0]
    n_items = 2 * (seq // tq)
    n_slots = scratch[0].shape[0]
    qh, res = {}, {}

    def bufs(n):
        return [sc.at[n % n_slots] for sc in scratch]

    def score(n):
        qi, hh = divmod(n, 2)
        hi = (qi + 1) * tq
        if hh == 0:
            qh[qi] = _split_heads(q_ref[qi * tq:hi, :])
        bufs(n)[0][:, :hi] = _nt_dot(qh[qi][hh], k_ref[:hi, :])

    def softmax(n):
        qi, hh = divmod(n, 2)
        hi = (qi + 1) * tq
        bias = None if col_bias_fn is None else col_bias_fn(hh, hi)
        _softmax_rows(*bufs(n), hi, tq, bias, keep_fn)

    def values(n):
        qi, hh = divmod(n, 2)
        hi = (qi + 1) * tq
        _, p_ref, _, l_ref = bufs(n)
        l = jnp.sum(l_ref[...], axis=1, keepdims=True)
        res[hh] = _dot(p_ref[:, :hi], v_ref[:hi, :]) / l
        if hh == 1:
            emit(qi, res[0], res[1])

    score(0)
    for n in range(n_items):
        if n + 1 < n_items:
            score(n + 1)
        softmax(n)
        if n:
            values(n - 1)
    values(n_items - 1)


def _split_heads(q):
    low = lax.broadcasted_iota(jnp.int32, (1, LANES), 1) < HEAD_DIM
    zero = jnp.zeros_like(q)
    return jnp.where(low, q, zero), jnp.where(low, zero, q)


def _fox_kernel(q_ref, k_ref, v_ref, ct_ref, o_ref, s_sc, p_sc, m_sc, l_sc, *, tq):
    sub = (2 * pl.program_id(1)) & 7
    low = lax.broadcasted_iota(jnp.int32, (1, LANES), 1) < HEAD_DIM

    def keep_fn(row, col):
        return col <= row

    def col_bias(hh, hi):
        return ct_ref[0, pl.ds(sub + hh, 1), :hi]

    def emit(qi, out0, out1):
        o_ref[qi * tq:(qi + 1) * tq, :] = jnp.where(low, out0, out1).astype(BF16)

    _attention_sweep(q_ref, k_ref, v_ref, (s_sc, p_sc, m_sc, l_sc), tq, col_bias, keep_fn, emit)


def _attention_scratch(tq, seq):
    n = 2 * ATT_SLOTS
    return [pltpu.VMEM((n, tq, seq), F32), pltpu.VMEM((n, tq, seq), BF16),
            pltpu.VMEM((n, tq, LANES), F32), pltpu.VMEM((n, tq, LANES), F32)]


def _fox_attention(qkvm, cum_t, batch, seq):
    t = batch * seq
    npair = FOX_HEADS // 2
    return pl.pallas_call(
        functools.partial(_fox_kernel, tq=ATT_TILE),
        grid=(batch, npair),
        in_specs=[
            pl.BlockSpec((seq, LANES), lambda b, p: (b, p)),
            pl.BlockSpec((seq, LANES), lambda b, p: (b, npair + p)),
            pl.BlockSpec((seq, LANES), lambda b, p: (b, 2 * npair + p)),
            pl.BlockSpec((1, 8, seq), lambda b, p: (b, p // 4, 0)),
        ],
        out_specs=pl.BlockSpec((seq, LANES), lambda b, p: (b, p)),
        out_shape=jax.ShapeDtypeStruct((t, npair * LANES), BF16),
        scratch_shapes=_attention_scratch(ATT_TILE, seq),
        compiler_params=_params("parallel", "parallel"),
        name="fox_attention",
    )(qkvm, qkvm, qkvm, cum_t)


def _mem_kernel(q_ref, mk_ref, mv_ref, o_ref):
    q = q_ref[...]
    mk = mk_ref[...]
    mv = mv_ref[...]
    lane = lax.broadcasted_iota(jnp.int32, (1, MEM_WIDTH), 1)
    zero = jnp.zeros_like(q)
    out = jnp.zeros(q.shape, F32)
    for h in range(MEM_HEADS):
        hm = (lane >= h * HEAD_DIM) & (lane < (h + 1) * HEAD_DIM)
        s = _nt_dot(jnp.where(hm, q, zero), mk)
        p = jnp.exp(s - jnp.max(s, axis=1, keepdims=True))
        l = jnp.sum(p, axis=1, keepdims=True)
        out = jnp.where(hm, _dot(p.astype(BF16), mv) / l, out)
    o_ref[...] = out.astype(BF16)


def _mem_attention(q_arr, q_col_block, mkv, batch, seq, mem_tokens):
    nq = seq // MEM_Q_TILE
    return pl.pallas_call(
        _mem_kernel,
        grid=(batch, nq),
        in_specs=[
            pl.BlockSpec((MEM_Q_TILE, MEM_WIDTH), lambda b, i: (b * nq + i, q_col_block)),
            pl.BlockSpec((mem_tokens, MEM_WIDTH), lambda b, i: (b, 0)),
            pl.BlockSpec((mem_tokens, MEM_WIDTH), lambda b, i: (b, 1)),
        ],
        out_specs=pl.BlockSpec((MEM_Q_TILE, MEM_WIDTH), lambda b, i: (b * nq + i, 0)),
        out_shape=jax.ShapeDtypeStruct((batch * seq, MEM_WIDTH), BF16),
        compiler_params=_params("parallel", "parallel"),
        name="mem_attention",
    )(q_arr, mkv, mkv)


def _norm_matmul_kernel(x_ref, g_ref, w_ref, o_ref):
    h = _rms(x_ref[...], g_ref[...]).astype(BF16)
    o_ref[...] = _dot(h, w_ref[...]).astype(o_ref.dtype)


def _norm_matmul(x, g, w, name):
    t, d = x.shape
    n = w.shape[1]
    return pl.pallas_call(
        _norm_matmul_kernel,
        grid=(t // ROW_TILE,),
        in_specs=[
            pl.BlockSpec((ROW_TILE, d), lambda i: (i, 0)),
            pl.BlockSpec((1, d), lambda i: (0, 0)),
            pl.BlockSpec((d, n), lambda i: (0, 0)),
        ],
        out_specs=pl.BlockSpec((ROW_TILE, n), lambda i: (i, 0)),
        out_shape=jax.ShapeDtypeStruct((t, n), BF16),
        compiler_params=_params("parallel"),
        name=name,
    )(x, g, w)


def _out_proj_kernel(x_ref, ys_ref, ym_ref, w_ref, o_ref, *, self_width):
    acc = _dot(ys_ref[...], w_ref[:self_width, :]) + _dot(ym_ref[...], w_ref[self_width:, :])
    o_ref[...] = x_ref[...] + acc


def _out_proj(x, y_self, y_mem, w):
    t, d = x.shape
    sw = y_self.shape[1]
    return pl.pallas_call(
        functools.partial(_out_proj_kernel, self_width=sw),
        grid=(t // ROW_TILE,),
        in_specs=[
            pl.BlockSpec((ROW_TILE, d), lambda i: (i, 0)),
            pl.BlockSpec((ROW_TILE, sw), lambda i: (i, 0)),
            pl.BlockSpec((ROW_TILE, MEM_WIDTH), lambda i: (i, 0)),
            pl.BlockSpec(w.shape, lambda i: (0, 0)),
        ],
        out_specs=pl.BlockSpec((ROW_TILE, d), lambda i: (i, 0)),
        out_shape=jax.ShapeDtypeStruct((t, d), F32),
        compiler_params=_params("parallel"),
        name="out_proj",
    )(x, y_self, y_mem, w)


def _swiglu_hidden(g, u):
    return (g * jax.nn.sigmoid(g) * u).astype(BF16)


def _ffn_kernel(x_ref, g_ref, wg_ref, wu_ref, wd_ref, o_ref, h_sc):
    @pl.when(pl.program_id(1) == 0)
    def _():
        x = x_ref[...]
        h_sc[...] = _rms(x, g_ref[...]).astype(BF16)
        o_ref[...] = x

    h = h_sc[...]
    hid = _swiglu_hidden(_dot(h, wg_ref[...]), _dot(h, wu_ref[...]))
    o_ref[...] += _dot(hid, wd_ref[...])


def _dense_ffn(x, g, w_gate_up, w_down):
    t, d = x.shape
    ff = w_down.shape[0]
    nf = ff // FFN_COL_TILE
    return pl.pallas_call(
        _ffn_kernel,
        grid=(t // FFN_ROW_TILE, nf),
        in_specs=[
            pl.BlockSpec((FFN_ROW_TILE, d), lambda i, f: (i, 0)),
            pl.BlockSpec((1, d), lambda i, f: (0, 0)),
            pl.BlockSpec((d, FFN_COL_TILE), lambda i, f: (0, f)),
            pl.BlockSpec((d, FFN_COL_TILE), lambda i, f: (0, nf + f)),
            pl.BlockSpec((FFN_COL_TILE, d), lambda i, f: (f, 0)),
        ],
        out_specs=pl.BlockSpec((FFN_ROW_TILE, d), lambda i, f: (i, 0)),
        out_shape=jax.ShapeDtypeStruct((t, d), F32),
        scratch_shapes=[pltpu.VMEM((FFN_ROW_TILE, d), BF16)],
        compiler_params=_params("parallel", "arbitrary"),
        name="dense_ffn",
    )(x, g, w_gate_up, w_gate_up, w_down)


def _rope_kernel(pos_ref, inv_ref, cos_ref, sin_ref):
    pos = pos_ref[...].astype(F32)
    ang = pos * inv_ref[...]
    lane = lax.broadcasted_iota(jnp.int32, ang.shape, 1)
    rot = (lane & (HEAD_DIM - 1)) < ROT_DIM
    first = (lane & (ROT_DIM - 1)) < (ROT_DIM // 2)
    sn = jnp.sin(ang)
    cos_ref[...] = jnp.where(rot, jnp.cos(ang), 1.0)
    sin_ref[...] = jnp.where(rot, jnp.where(first, -sn, sn), 0.0)


def _rope_tables(pos_col, inv_lane):
    t = pos_col.shape[0]
    return pl.pallas_call(
        _rope_kernel,
        grid=(t // ROW_TILE,),
        in_specs=[
            pl.BlockSpec((ROW_TILE, 1), lambda i: (i, 0)),
            pl.BlockSpec((1, LANES), lambda i: (0, 0)),
        ],
        out_specs=[pl.BlockSpec((ROW_TILE, LANES), lambda i: (i, 0))] * 2,
        out_shape=[jax.ShapeDtypeStruct((t, LANES), F32)] * 2,
        compiler_params=_params("parallel"),
        name="rope_tables",
    )(pos_col, inv_lane)


def _proj_rope_kernel(x_ref, g_ref, w_ref, cos_ref, sin_ref, o_ref, *, n_rope, col_chunk):
    h = _rms(x_ref[...], g_ref[...]).astype(BF16)
    cos = cos_ref[...]
    sin = sin_ref[...]
    lane = lax.broadcasted_iota(jnp.int32, (1, LANES), 1)
    first = (lane & (ROT_DIM - 1)) < (ROT_DIM // 2)
    half = ROT_DIM // 2
    n = w_ref.shape[1]
    for c in range(n // col_chunk):
        a = _dot(h, w_ref[:, c * col_chunk:(c + 1) * col_chunk])
        for s in range(col_chunk // LANES):
            col = c * col_chunk + s * LANES
            blk = a[:, s * LANES:(s + 1) * LANES]
            if col < n_rope:
                partner = jnp.where(first, pltpu.roll(blk, LANES - half, 1),
                                    pltpu.roll(blk, half, 1))
                blk = blk * cos + partner * sin
            o_ref[:, col:col + LANES] = blk.astype(BF16)


def _proj_rope(x, g, w, cos, sin, n_rope, name):
    t, d = x.shape
    n = w.shape[1]
    return pl.pallas_call(
        functools.partial(_proj_rope_kernel, n_rope=n_rope, col_chunk=256),
        grid=(t // ROW_TILE,),
        in_specs=[
            pl.BlockSpec((ROW_TILE, d), lambda i: (i, 0)),
            pl.BlockSpec((1, d), lambda i: (0, 0)),
            pl.BlockSpec((d, n), lambda i: (0, 0)),
            pl.BlockSpec((ROW_TILE, LANES), lambda i: (i, 0)),
            pl.BlockSpec((ROW_TILE, LANES), lambda i: (i, 0)),
        ],
        out_specs=pl.BlockSpec((ROW_TILE, n), lambda i: (i, 0)),
        out_shape=jax.ShapeDtypeStruct((t, n), BF16),
        compiler_params=_params("parallel"),
        name=name,
    )(x, g, w, cos, sin)


def _diff_kernel(lam_ref, sg_ref, q_ref, k_ref, v_ref, o_ref, s_sc, p_sc, m_sc, l_sc, *, tq,
                 lam_init):
    lp = lam_ref[...]
    lam = (jnp.exp(jnp.sum(lp[0:1] * lp[1:2], axis=1, keepdims=True))
           - jnp.exp(jnp.sum(lp[2:3] * lp[3:4], axis=1, keepdims=True)) + lam_init)

    def keep_fn(row, col):
        return (col >> CHUNK_SHIFT) <= (row >> CHUNK_SHIFT)

    def emit(qi, out1, out2):
        o = _rms(out1 - lam * out2, sg_ref[...]) * (1.0 - lam_init)
        o_ref[qi * tq:(qi + 1) * tq, :] = o.astype(BF16)

    _attention_sweep(q_ref, k_ref, v_ref, (s_sc, p_sc, m_sc, l_sc), tq, None, keep_fn, emit)


def _diff_attention(lam_rows, subln_g, qb, kv, batch, seq, lam_init):
    t = batch * seq
    return pl.pallas_call(
        functools.partial(_diff_kernel, tq=ATT_TILE, lam_init=lam_init),
        grid=(batch, DIFF_HEADS),
        in_specs=[
            pl.BlockSpec((8, LANES), lambda b, h: (0, 0)),
            pl.BlockSpec((1, LANES), lambda b, h: (0, 0)),
            pl.BlockSpec((seq, LANES), lambda b, h: (b, h)),
            pl.BlockSpec((seq, LANES), lambda b, h: (b, h)),
            pl.BlockSpec((seq, LANES), lambda b, h: (b, DIFF_HEADS + h)),
        ],
        out_specs=pl.BlockSpec((seq, LANES), lambda b, h: (b, h)),
        out_shape=jax.ShapeDtypeStruct((t, DIFF_HEADS * LANES), BF16),
        scratch_shapes=_attention_scratch(ATT_TILE, seq),
        compiler_params=_params("parallel", "parallel"),
        name="diff_attention",
    )(lam_rows, subln_g, qb, kv, kv)


def _moe_pre_kernel(x_ref, g_ref, wr_ref, tri_ref, hp_ref, route_ref, cnt_ref, run_sc):
    @pl.when(pl.program_id(0) == 0)
    def _():
        run_sc[...] = jnp.zeros_like(run_sc)

    hf = _rms(x_ref[...], g_ref[...])
    d = hf.shape[1]
    hb = hf.astype(BF16).astype(F32)
    bits = pltpu.bitcast(hb, jnp.uint32)
    hp_ref[...] = bits[:, :d // 2] | (bits[:, d // 2:] >> 16)

    logits = jnp.dot(hf, wr_ref[...], preferred_element_type=F32,
                     precision=lax.Precision.HIGHEST)
    lane = lax.broadcasted_iota(jnp.int32, logits.shape, 1)
    lanef = lane.astype(F32)
    lg = jnp.where(lane < N_EXPERTS, logits, NEG)
    v1 = jnp.max(lg, axis=1, keepdims=True)
    i1 = jnp.min(jnp.where(lg == v1, lanef, float(LANES)), axis=1, keepdims=True)
    lg2 = jnp.where(lanef == i1, NEG, lg)
    v2 = jnp.max(lg2, axis=1, keepdims=True)
    i2 = jnp.min(jnp.where(lg2 == v2, lanef, float(LANES)), axis=1, keepdims=True)
    e = jnp.exp(v2 - v1)
    g1 = 1.0 / (1.0 + e)
    g2 = e / (1.0 + e)

    oh1 = lanef == i1
    oh2 = lanef == i2
    oh = jnp.where(oh1 | oh2, 1.0, 0.0)
    before = _dot(tri_ref[...], oh.astype(BF16)) + run_sc[...]
    r1 = jnp.sum(jnp.where(oh1, before, 0.0), axis=1, keepdims=True)
    r2 = jnp.sum(jnp.where(oh2, before, 0.0), axis=1, keepdims=True)
    run_sc[...] += jnp.sum(oh, axis=0, keepdims=True)
    cnt_ref[...] = run_sc[...]

    route = jnp.where(lane == 0, i1, 0.0)
    for ln, val in ((1, i2), (2, r1), (3, r2), (4, g1), (5, g2)):
        route = jnp.where(lane == ln, val, route)
    route_ref[...] = route


def _moe_pre(x, g, w_router_pad, tri):
    t, d = x.shape
    return pl.pallas_call(
        _moe_pre_kernel,
        grid=(t // ROW_TILE,),
        in_specs=[
            pl.BlockSpec((ROW_TILE, d), lambda i: (i, 0)),
            pl.BlockSpec((1, d), lambda i: (0, 0)),
            pl.BlockSpec((d, LANES), lambda i: (0, 0)),
            pl.BlockSpec((ROW_TILE, ROW_TILE), lambda i: (0, 0)),
        ],
        out_specs=[
            pl.BlockSpec((ROW_TILE, d // 2), lambda i: (i, 0)),
            pl.BlockSpec((ROW_TILE, LANES), lambda i: (i, 0)),
            pl.BlockSpec((1, LANES), lambda i: (0, 0)),
        ],
        out_shape=[
            jax.ShapeDtypeStruct((t, d // 2), jnp.uint32),
            jax.ShapeDtypeStruct((t, LANES), F32),
            jax.ShapeDtypeStruct((1, LANES), F32),
        ],
        scratch_shapes=[pltpu.VMEM((1, LANES), F32)],
        compiler_params=_params("arbitrary"),
        name="moe_pre",
    )(x, g, w_router_pad, tri)


def _dispatch_kernel(pos_ref, hp_ref, init_ref, out_ref, sem, *, tile):
    del init_ref
    base = pl.program_id(0) * tile * 2

    def row_copy(r, k):
        dst = pos_ref[base + 2 * r + k]
        return pltpu.make_async_copy(hp_ref.at[pl.ds(r, 1)], out_ref.at[pl.ds(dst, 1)], sem)

    def start(r, c):
        row_copy(r, 0).start(priority=0)
        row_copy(r, 1).start(priority=1)
        return c

    def wait(r, c):
        row_copy(r, 0).wait()
        row_copy(r, 1).wait()
        return c

    lax.fori_loop(0, tile, start, 0, unroll=8)
    lax.fori_loop(0, tile, wait, 0, unroll=8)


def _dispatch(pos, hp, n_slots):
    t, w = hp.shape
    init = jnp.zeros((n_slots, w), hp.dtype)
    return pl.pallas_call(
        functools.partial(_dispatch_kernel, tile=GATHER_TILE),
        grid_spec=pltpu.PrefetchScalarGridSpec(
            num_scalar_prefetch=1,
            grid=(t // GATHER_TILE,),
            in_specs=[
                pl.BlockSpec((GATHER_TILE, w), lambda i, pos: (i, 0)),
                pl.BlockSpec(memory_space=pl.ANY),
            ],
            out_specs=pl.BlockSpec(memory_space=pl.ANY),
            scratch_shapes=[pltpu.SemaphoreType.DMA(())],
        ),
        out_shape=jax.ShapeDtypeStruct((n_slots, w), hp.dtype),
        input_output_aliases={2: 0},
        compiler_params=_params("arbitrary"),
        name="moe_dispatch",
    )(pos, hp, init)


def _moe_kernel(te_ref, na_ref, xp_ref, wg_ref, wu_ref, wd_ref, o_ref, xa_sc, xb_sc):
    del te_ref
    i = pl.program_id(0)
    f = pl.program_id(1)

    @pl.when(f == 0)
    def _():
        o_ref[...] = jnp.zeros_like(o_ref)

    @pl.when(i < na_ref[0])
    def _():
        @pl.when(f == 0)
        def _():
            pk = xp_ref[...]
            xa_sc[...] = pltpu.bitcast(pk & jnp.uint32(0xFFFF0000), F32).astype(BF16)
            xb_sc[...] = pltpu.bitcast(pk << 16, F32).astype(BF16)

        xa = xa_sc[...]
        xb = xb_sc[...]
        half = xa.shape[1]
        wg = wg_ref[0].astype(BF16)
        wu = wu_ref[0].astype(BF16)
        g = _dot(xa, wg[:half]) + _dot(xb, wg[half:])
        u = _dot(xa, wu[:half]) + _dot(xb, wu[half:])
        o_ref[...] += _dot(_swiglu_hidden(g, u), wd_ref[0].astype(BF16))


def _moe_experts(tile_expert, n_active, xp, w_gate_up, w_down, n_tiles):
    d = w_down.shape[2]
    ff = w_down.shape[1]
    ct = MOE_COL_TILE
    nf = ff // ct
    tm = MOE_ROW_TILE

    def row(i, f, te, na):
        return (jnp.minimum(i, na[0] - 1), 0)

    def col(i, f, na):
        return jnp.where(i < na[0], f, nf - 1)

    return pl.pallas_call(
        _moe_kernel,
        grid_spec=pltpu.PrefetchScalarGridSpec(
            num_scalar_prefetch=2,
            grid=(n_tiles, nf),
            in_specs=[
                pl.BlockSpec((tm, d // 2), row),
                pl.BlockSpec((1, d, ct), lambda i, f, te, na: (te[i], 0, col(i, f, na))),
                pl.BlockSpec((1, d, ct), lambda i, f, te, na: (te[i], 0, nf + col(i, f, na))),
                pl.BlockSpec((1, ct, d), lambda i, f, te, na: (te[i], col(i, f, na), 0)),
            ],
            out_specs=pl.BlockSpec((tm, d), lambda i, f, te, na: (i, 0)),
            scratch_shapes=[pltpu.VMEM((tm, d // 2), BF16), pltpu.VMEM((tm, d // 2), BF16)],
        ),
        out_shape=jax.ShapeDtypeStruct((n_tiles * tm, d), F32),
        compiler_params=_params("arbitrary", "arbitrary"),
        name="moe_experts",
    )(tile_expert, n_active, xp, w_gate_up, w_gate_up, w_down)


def _combine_kernel(pos_ref, x_ref, route_ref, g_ref, y_ref, o_ref, buf, sem, *, tile):
    base = pl.program_id(0) * tile * 2

    def row_copy(r, k):
        src = pos_ref[base + 2 * r + k]
        return pltpu.make_async_copy(y_ref.at[pl.ds(src, 1)], buf.at[k, pl.ds(r, 1)], sem)

    def start(r, c):
        row_copy(r, 0).start(priority=0)
        row_copy(r, 1).start(priority=1)
        return c

    def wait(r, c):
        row_copy(r, 0).wait()
        row_copy(r, 1).wait()
        return c

    lax.fori_loop(0, tile, start, 0, unroll=8)
    lax.fori_loop(0, tile, wait, 0, unroll=8)
    route = route_ref[...]
    y = x_ref[...] + route[:, 4:5] * buf[0] + route[:, 5:6] * buf[1]
    o_ref[...] = _rms(y, g_ref[...])


def _combine(pos, x, route, g, y_sorted):
    t, d = x.shape
    return pl.pallas_call(
        functools.partial(_combine_kernel, tile=GATHER_TILE),
        grid_spec=pltpu.PrefetchScalarGridSpec(
            num_scalar_prefetch=1,
            grid=(t // GATHER_TILE,),
            in_specs=[
                pl.BlockSpec((GATHER_TILE, d), lambda i, pos: (i, 0)),
                pl.BlockSpec((GATHER_TILE, LANES), lambda i, pos: (i, 0)),
                pl.BlockSpec((1, d), lambda i, pos: (0, 0)),
                pl.BlockSpec(memory_space=pl.ANY),
            ],
            out_specs=pl.BlockSpec((GATHER_TILE, d), lambda i, pos: (i, 0)),
            scratch_shapes=[pltpu.VMEM((2, GATHER_TILE, d), F32), pltpu.SemaphoreType.DMA(())],
        ),
        out_shape=jax.ShapeDtypeStruct((t, d), F32),
        compiler_params=_params("arbitrary"),
        name="moe_combine",
    )(pos, x, route, g, y_sorted)


def _row(v):
    return v.reshape(1, -1).astype(F32)


def _pad_lanes(v):
    return jnp.pad(v.astype(F32), (0, LANES - v.shape[0])).reshape(1, LANES)


def kernel(x, mem, positions, mix_norm_g, ffn_norm_g, mem_norm_g, w_mem_kv, w_out, w_in_a, b_forget, w_q_b, lambda_q1, lambda_k1, lambda_q2, lambda_k2, subln_g, kv_norm_g, w_kv_shared, w_gate_up_dense, w_down_dense, w_router, w_gate_up_moe, w_down_moe, final_norm_g):
    batch, seq, d = x.shape
    mem_tokens = mem.shape[1]
    t = batch * seq
    assert w_in_a.shape[0] == 1 and w_q_b.shape[0] == 1 and w_out.shape[0] == 2
    fox_w = FOX_HEADS * HEAD_DIM
    diff_w = DIFF_HEADS * 2 * HEAD_DIM
    scale = HEAD_DIM ** -0.5

    xf = x.reshape(t, d)
    memf = mem.reshape(batch * mem_tokens, d)

    wa = w_in_a[0]
    w_fl = jnp.pad(wa[:, 3 * fox_w:3 * fox_w + FOX_HEADS], ((0, 0), (0, LANES - FOX_HEADS)))
    w_a = jnp.concatenate(
        [wa[:, :fox_w] * scale, wa[:, fox_w:3 * fox_w], wa[:, 3 * fox_w + FOX_HEADS:] * scale, w_fl],
        axis=1).astype(BF16)
    w_b = (w_q_b[0] * scale).astype(BF16)
    w_kv = w_kv_shared.astype(BF16)
    w_o = w_out.astype(BF16)
    w_mkv = w_mem_kv.astype(BF16)
    w_gu_d = w_gate_up_dense[0].astype(BF16)
    w_dn_d = w_down_dense[0].astype(BF16)
    w_gu_m = w_gate_up_moe[0]
    w_dn_m = w_down_moe[0]
    w_r = jnp.pad(w_router[0].astype(F32), ((0, 0), (0, LANES - N_EXPERTS)))

    qkvm, log_f = _proj_a(xf, _row(mix_norm_g[0]), w_a, _pad_lanes(b_forget[0]))
    cum_t = _cumsum(log_f, batch, seq)
    y_self = _fox_attention(qkvm, cum_t, batch, seq)
    mkv0 = _norm_matmul(memf, _row(mem_norm_g[0]), w_mkv[0], "mem_kv0")
    y_mem = _mem_attention(qkvm, 3 * fox_w // MEM_WIDTH, mkv0, batch, seq, mem_tokens)
    x1 = _out_proj(xf, y_self, y_mem, w_o[0])
    x2 = _dense_ffn(x1, _row(ffn_norm_g[0]), w_gu_d, w_dn_d)

    half = ROT_DIM // 2
    inv_freq = ROPE_THETA ** (-(jnp.arange(half, dtype=F32) * 2.0 / ROT_DIM))
    inv_lane = jnp.tile(inv_freq, LANES // half).reshape(1, LANES)
    cos_t, sin_t = _rope_tables(positions.reshape(t, 1).astype(jnp.int32), inv_lane)
    kv = _proj_rope(x2, _row(kv_norm_g), w_kv, cos_t, sin_t, diff_w, "proj_kv")
    qb = _proj_rope(x2, _row(mix_norm_g[1]), w_b, cos_t, sin_t, diff_w, "proj_b")

    lam_init = 0.8 - 0.6 * math.exp(-0.3 * 1)
    lam_rows = jnp.concatenate(
        [jnp.pad(v[0].astype(F32), (0, LANES - HEAD_DIM)).reshape(1, LANES)
         for v in (lambda_q1, lambda_k1, lambda_q2, lambda_k2)]
        + [jnp.zeros((4, LANES), F32)], axis=0)
    y_self = _diff_attention(lam_rows, _row(subln_g[0]), qb, kv, batch, seq, lam_init)
    mkv1 = _norm_matmul(memf, _row(mem_norm_g[1]), w_mkv[1], "mem_kv1")
    y_mem = _mem_attention(qb, diff_w // MEM_WIDTH, mkv1, batch, seq, mem_tokens)
    x3 = _out_proj(x2, y_self, y_mem, w_o[1])

    tri = jnp.tril(jnp.ones((ROW_TILE, ROW_TILE), BF16), -1)
    hp, route, cnt = _moe_pre(x3, _row(ffn_norm_g[1]), w_r, tri)
    tm = MOE_ROW_TILE
    n_tiles = 2 * t // tm + N_EXPERTS
    counts = cnt[0, :N_EXPERTS].astype(jnp.int32)
    tiles_per = (counts + tm - 1) // tm
    tile_end = jnp.cumsum(tiles_per)
    group_off = (tile_end - tiles_per) * tm
    n_active = tile_end[-1:]
    tile_id = jnp.minimum(jnp.arange(n_tiles, dtype=jnp.int32), n_active - 1)
    tile_expert = jnp.sum(tile_id[:, None] >= tile_end[None, :], axis=1).astype(jnp.int32)
    idx = route[:, 0:2].astype(jnp.int32)
    rank = route[:, 2:4].astype(jnp.int32)
    pos = (group_off[idx] + rank).reshape(2 * t)

    xp = _dispatch(pos, hp, n_tiles * tm)
    y_sorted = _moe_experts(tile_expert, n_active.astype(jnp.int32), xp, w_gu_m, w_dn_m, n_tiles)
    out = _combine(pos, x3, route, _row(final_norm_g), y_sorted)
    return out.reshape(batch, seq, d)
```

```python
import functools
import math

import jax
import jax.numpy as jnp
from jax import lax
from jax.experimental import pallas as pl
from jax.experimental.pallas import tpu as pltpu

F32 = jnp.float32
BF16 = jnp.bfloat16

HEAD_DIM = 64
LANES = 128
SUBLANES = 8
CHUNK_SHIFT = 6
FOX_HEADS = 12
DIFF_HEADS = 6
MEM_HEADS = 4
MEM_WIDTH = MEM_HEADS * HEAD_DIM
ROPE_THETA = 500000.0
ROT_DIM = HEAD_DIM // 4
N_EXPERTS = 8
EPS = 1e-5
NEG = -1e30
VMEM_LIMIT = 48 * 1024 * 1024

ROW_TILE = 512
FFN_ROW_TILE = 1024
FFN_COL_TILE = 256
ATT_TILE = 256
ROW_CHUNK = 16
ATT_SLOTS = 2
MEM_Q_TILE = 512
MOE_PRE_TILE = 1024
MOE_ROW_TILE = 1024
MOE_COL_TILE = 512
GATHER_TILE = 256


def _params(*sem):
    return pltpu.CompilerParams(dimension_semantics=sem, vmem_limit_bytes=VMEM_LIMIT)


def _rms(x, g):
    ms = jnp.mean(x * x, axis=-1, keepdims=True)
    return x * lax.rsqrt(ms + EPS) * g


def _nt_dot(a, b):
    return lax.dot_general(a, b, (((1,), (1,)), ((), ())), preferred_element_type=F32)


def _dot(a, b):
    return jnp.dot(a, b, preferred_element_type=F32)


def _proj_a_kernel(x_ref, g_ref, w_ref, b_ref, o_ref, lf_ref, *, n_main, col_chunk):
    h = _rms(x_ref[...], g_ref[...]).astype(BF16)
    for c in range(n_main // col_chunk):
        cs = slice(c * col_chunk, (c + 1) * col_chunk)
        o_ref[:, cs] = _dot(h, w_ref[:, cs]).astype(BF16)
    z = _dot(h, w_ref[:, n_main:]) + b_ref[...]
    lf_ref[...] = jnp.minimum(z, 0.0) - jnp.log1p(jnp.exp(-jnp.abs(z)))


def _proj_a(x, g, w, b_pad):
    t, d = x.shape
    n_all = w.shape[1]
    n_main = n_all - LANES
    return pl.pallas_call(
        functools.partial(_proj_a_kernel, n_main=n_main, col_chunk=512),
        grid=(t // ROW_TILE,),
        in_specs=[
            pl.BlockSpec((ROW_TILE, d), lambda i: (i, 0)),
            pl.BlockSpec((1, d), lambda i: (0, 0)),
            pl.BlockSpec((d, n_all), lambda i: (0, 0)),
            pl.BlockSpec((1, LANES), lambda i: (0, 0)),
        ],
        out_specs=[
            pl.BlockSpec((ROW_TILE, n_main), lambda i: (i, 0)),
            pl.BlockSpec((ROW_TILE, LANES), lambda i: (i, 0)),
        ],
        out_shape=[
            jax.ShapeDtypeStruct((t, n_main), BF16),
            jax.ShapeDtypeStruct((t, LANES), F32),
        ],
        compiler_params=_params("parallel"),
        name="proj_a",
    )(x, g, w, b_pad)


def _cumsum_kernel(lf_ref, ct_ref):
    x = lf_ref[...]
    s = x.shape[0]
    row = lax.broadcasted_iota(jnp.int32, x.shape, 0)
    sh = 1
    while sh < s:
        x = x + jnp.where(row >= sh, pltpu.roll(x, sh, 0), 0.0)
        sh *= 2
    ct_ref[0] = x.T


def _cumsum(lf, batch, seq):
    return pl.pallas_call(
        _cumsum_kernel,
        grid=(batch,),
        in_specs=[pl.BlockSpec((seq, LANES), lambda b: (b, 0))],
        out_specs=pl.BlockSpec((1, LANES, seq), lambda b: (b, 0, 0)),
        out_shape=jax.ShapeDtypeStruct((batch, LANES, seq), F32),
        compiler_params=_params("parallel"),
        name="cumsum",
    )(lf)


def _softmax_rows(s_ref, p_ref, m_ref, hi, tq, col_bias, keep_fn):
    lo = hi - tq
    shape = (ROW_CHUNK, LANES)

    def visibility(r, c):
        r0, c0 = r * ROW_CHUNK, c * LANES - lo
        if c0 < 0 or keep_fn(r0, c0 + LANES - 1):
            return "all"
        if not keep_fn(r0 + ROW_CHUNK - 1, c0):
            return "none"
        ri = lax.broadcasted_iota(jnp.int32, shape, 0) + r0
        ci = lax.broadcasted_iota(jnp.int32, shape, 1) + c0
        return keep_fn(ri, ci)

    def load(r, c, vis):
        t = s_ref[r * ROW_CHUNK:(r + 1) * ROW_CHUNK, c * LANES:(c + 1) * LANES]
        if col_bias is not None:
            t = t - col_bias[:, c * LANES:(c + 1) * LANES]
        return t if isinstance(vis, str) else jnp.where(vis, t, NEG)

    for r in range(tq // ROW_CHUNK):
        m_acc = None
        for c in range(hi // LANES):
            vis = visibility(r, c)
            if isinstance(vis, str) and vis == "none":
                continue
            t = load(r, c, vis)
            m_acc = t if m_acc is None else jnp.maximum(m_acc, t)
        m_ref[r * ROW_CHUNK:(r + 1) * ROW_CHUNK, :] = jnp.broadcast_to(
            jnp.max(m_acc, axis=1, keepdims=True), shape)
    for r in range(tq // ROW_CHUNK):
        rows = slice(r * ROW_CHUNK, (r + 1) * ROW_CHUNK)
        m = m_ref[rows, :]
        for c in range(hi // LANES):
            cols = slice(c * LANES, (c + 1) * LANES)
            vis = visibility(r, c)
            if isinstance(vis, str) and vis == "none":
                p_ref[rows, cols] = jnp.zeros(shape, BF16)
            else:
                p_ref[rows, cols] = jnp.exp((load(r, c, vis) - m).astype(BF16))


def _attention_sweep(q_ref, k_ref, v_aug, scratch, tq, col_bias_fn, keep_fn, emit):
    seq = q_ref.shape[0]
    n_items = 2 * (seq // tq)
    n_slots = scratch[0].shape[0]
    qh, res = {}, {}

    def bufs(n):
        return [sc.at[n % n_slots] for sc in scratch]

    def score(n):
        qi, hh = divmod(n, 2)
        hi = (qi + 1) * tq
        if hh == 0:
            qh[qi] = _split_heads(q_ref[qi * tq:hi, :])
        bufs(n)[0][:, :hi] = _nt_dot(qh[qi][hh], k_ref[:hi, :])

    def softmax(n):
        qi, hh = divmod(n, 2)
        hi = (qi + 1) * tq
        bias = None if col_bias_fn is None else col_bias_fn(hh, hi)
        _softmax_rows(*bufs(n), hi, tq, bias, keep_fn)

    def values(n):
        qi, hh = divmod(n, 2)
        hi = (qi + 1) * tq
        res[hh] = _dot(bufs(n)[1][:, :hi], v_aug(hh)[:hi, :])
        if hh == 1:
            emit(qi, res[0], res[1])

    score(0)
    for n in range(n_items):
        if n + 1 < n_items:
            score(n + 1)
        softmax(n)
        if n:
            values(n - 1)
    values(n_items - 1)


def _split_heads(q):
    low = lax.broadcasted_iota(jnp.int32, (1, LANES), 1) < HEAD_DIM
    zero = jnp.zeros_like(q)
    return jnp.where(low, q, zero), jnp.where(low, zero, q)


def _fox_kernel(q_ref, k_ref, v_ref, ct_ref, o_ref, s_sc, p_sc, m_sc, va_sc, *, tq):
    sub = (2 * pl.program_id(1)) & 7
    low = lax.broadcasted_iota(jnp.int32, (1, LANES), 1) < HEAD_DIM
    v = v_ref[...]
    one = jnp.ones_like(v)
    va_sc[0] = jnp.where(low, v, one)
    va_sc[1] = jnp.where(low, one, v)

    def keep_fn(row, col):
        return col <= row

    def col_bias(hh, hi):
        return ct_ref[0, pl.ds(sub + hh, 1), :hi]

    def emit(qi, r0, r1):
        num = jnp.where(low, r0, r1)
        den = pltpu.roll(jnp.where(low, r1, r0), HEAD_DIM, 1)
        o_ref[qi * tq:(qi + 1) * tq, :] = (num / den).astype(BF16)

    _attention_sweep(q_ref, k_ref, lambda hh: va_sc.at[hh], (s_sc, p_sc, m_sc), tq, col_bias,
                     keep_fn, emit)


def _attention_scratch(tq, seq):
    n = 2 * ATT_SLOTS
    return [pltpu.VMEM((n, tq, seq), F32), pltpu.VMEM((n, tq, seq), BF16),
            pltpu.VMEM((n, tq, LANES), F32)]


def _fox_attention(qkvm, cum_t, batch, seq):
    t = batch * seq
    npair = FOX_HEADS // 2
    return pl.pallas_call(
        functools.partial(_fox_kernel, tq=ATT_TILE),
        grid=(batch, npair),
        in_specs=[
            pl.BlockSpec((seq, LANES), lambda b, p: (b, p)),
            pl.BlockSpec((seq, LANES), lambda b, p: (b, npair + p)),
            pl.BlockSpec((seq, LANES), lambda b, p: (b, 2 * npair + p)),
            pl.BlockSpec((1, 8, seq), lambda b, p: (b, p // 4, 0)),
        ],
        out_specs=pl.BlockSpec((seq, LANES), lambda b, p: (b, p)),
        out_shape=jax.ShapeDtypeStruct((t, npair * LANES), BF16),
        scratch_shapes=_attention_scratch(ATT_TILE, seq) + [pltpu.VMEM((2, seq, LANES), BF16)],
        compiler_params=_params("parallel", "parallel"),
        name="fox_attention",
    )(qkvm, qkvm, qkvm, cum_t)


def _mem_kernel(q_ref, mk_ref, mv_ref, o_ref):
    q = q_ref[...]
    mk = mk_ref[...]
    mv = mv_ref[...]
    lane = lax.broadcasted_iota(jnp.int32, (1, MEM_WIDTH), 1)
    zero = jnp.zeros_like(q)
    out = jnp.zeros(q.shape, F32)
    for h in range(MEM_HEADS):
        hm = (lane >= h * HEAD_DIM) & (lane < (h + 1) * HEAD_DIM)
        s = _nt_dot(jnp.where(hm, q, zero), mk)
        p = jnp.exp(s - jnp.max(s, axis=1, keepdims=True))
        l = jnp.sum(p, axis=1, keepdims=True)
        out = jnp.where(hm, _dot(p.astype(BF16), mv) / l, out)
    o_ref[...] = out.astype(BF16)


def _mem_attention(q_arr, q_col_block, mkv, batch, seq, mem_tokens):
    nq = seq // MEM_Q_TILE
    return pl.pallas_call(
        _mem_kernel,
        grid=(batch, nq),
        in_specs=[
            pl.BlockSpec((MEM_Q_TILE, MEM_WIDTH), lambda b, i: (b * nq + i, q_col_block)),
            pl.BlockSpec((mem_tokens, MEM_WIDTH), lambda b, i: (b, 0)),
            pl.BlockSpec((mem_tokens, MEM_WIDTH), lambda b, i: (b, 1)),
        ],
        out_specs=pl.BlockSpec((MEM_Q_TILE, MEM_WIDTH), lambda b, i: (b * nq + i, 0)),
        out_shape=jax.ShapeDtypeStruct((batch * seq, MEM_WIDTH), BF16),
        compiler_params=_params("parallel", "parallel"),
        name="mem_attention",
    )(q_arr, mkv, mkv)


def _norm_matmul_kernel(x_ref, g_ref, w_ref, o_ref):
    h = _rms(x_ref[...], g_ref[...]).astype(BF16)
    o_ref[...] = _dot(h, w_ref[...]).astype(o_ref.dtype)


def _norm_matmul(x, g, w, name):
    t, d = x.shape
    n = w.shape[1]
    return pl.pallas_call(
        _norm_matmul_kernel,
        grid=(t // ROW_TILE,),
        in_specs=[
            pl.BlockSpec((ROW_TILE, d), lambda i: (i, 0)),
            pl.BlockSpec((1, d), lambda i: (0, 0)),
            pl.BlockSpec((d, n), lambda i: (0, 0)),
        ],
        out_specs=pl.BlockSpec((ROW_TILE, n), lambda i: (i, 0)),
        out_shape=jax.ShapeDtypeStruct((t, n), BF16),
        compiler_params=_params("parallel"),
        name=name,
    )(x, g, w)


def _mixed_residual(x_ref, ys_ref, ym_ref, wo_ref):
    sw = ys_ref.shape[1]
    return x_ref[...] + _dot(ys_ref[...], wo_ref[:sw, :]) + _dot(ym_ref[...], wo_ref[sw:, :])


def _swiglu_hidden(g, u):
    return (g * jax.nn.sigmoid(g) * u).astype(BF16)


def _ffn_kernel(x_ref, ys_ref, ym_ref, wo_ref, g_ref, wg_ref, wu_ref, wd_ref, o_ref, h_sc):
    @pl.when(pl.program_id(1) == 0)
    def _():
        x = _mixed_residual(x_ref, ys_ref, ym_ref, wo_ref)
        h_sc[...] = _rms(x, g_ref[...]).astype(BF16)
        o_ref[...] = x

    h = h_sc[...]
    hid = _swiglu_hidden(_dot(h, wg_ref[...]), _dot(h, wu_ref[...]))
    o_ref[...] += _dot(hid, wd_ref[...])


def _dense_ffn(x, y_self, y_mem, w_out, g, w_gate_up, w_down):
    t, d = x.shape
    ff = w_down.shape[0]
    nf = ff // FFN_COL_TILE
    tm = FFN_ROW_TILE
    return pl.pallas_call(
        _ffn_kernel,
        grid=(t // tm, nf),
        in_specs=[
            pl.BlockSpec((tm, d), lambda i, f: (i, 0)),
            pl.BlockSpec((tm, y_self.shape[1]), lambda i, f: (i, 0)),
            pl.BlockSpec((tm, y_mem.shape[1]), lambda i, f: (i, 0)),
            pl.BlockSpec(w_out.shape, lambda i, f: (0, 0)),
            pl.BlockSpec((1, d), lambda i, f: (0, 0)),
            pl.BlockSpec((d, FFN_COL_TILE), lambda i, f: (0, f)),
            pl.BlockSpec((d, FFN_COL_TILE), lambda i, f: (0, nf + f)),
            pl.BlockSpec((FFN_COL_TILE, d), lambda i, f: (f, 0)),
        ],
        out_specs=pl.BlockSpec((tm, d), lambda i, f: (i, 0)),
        out_shape=jax.ShapeDtypeStruct((t, d), F32),
        scratch_shapes=[pltpu.VMEM((tm, d), BF16)],
        compiler_params=_params("parallel", "arbitrary"),
        name="dense_ffn",
    )(x, y_self, y_mem, w_out, g, w_gate_up, w_gate_up, w_down)


def _rope_kernel(pos_ref, inv_ref, cos_ref, sin_ref):
    pos = pos_ref[...].astype(F32)
    ang = pos * inv_ref[...]
    lane = lax.broadcasted_iota(jnp.int32, ang.shape, 1)
    rot = (lane & (HEAD_DIM - 1)) < ROT_DIM
    first = (lane & (ROT_DIM - 1)) < (ROT_DIM // 2)
    sn = jnp.sin(ang)
    cos_ref[...] = jnp.where(rot, jnp.cos(ang), 1.0)
    sin_ref[...] = jnp.where(rot, jnp.where(first, -sn, sn), 0.0)


def _rope_tables(pos_col, inv_lane):
    t = pos_col.shape[0]
    return pl.pallas_call(
        _rope_kernel,
        grid=(t // ROW_TILE,),
        in_specs=[
            pl.BlockSpec((ROW_TILE, 1), lambda i: (i, 0)),
            pl.BlockSpec((1, LANES), lambda i: (0, 0)),
        ],
        out_specs=[pl.BlockSpec((ROW_TILE, LANES), lambda i: (i, 0))] * 2,
        out_shape=[jax.ShapeDtypeStruct((t, LANES), F32)] * 2,
        compiler_params=_params("parallel"),
        name="rope_tables",
    )(pos_col, inv_lane)


def _proj_rope_kernel(x_ref, gk_ref, gq_ref, wk_ref, wq_ref, cos_ref, sin_ref, ok_ref, oq_ref, *,
                      n_rope, col_chunk):
    x = x_ref[...]
    xn = x * lax.rsqrt(jnp.mean(x * x, axis=-1, keepdims=True) + EPS)
    cos = cos_ref[...]
    sin = sin_ref[...]
    lane = lax.broadcasted_iota(jnp.int32, (1, LANES), 1)
    first = (lane & (ROT_DIM - 1)) < (ROT_DIM // 2)
    half = ROT_DIM // 2
    for g_ref, w_ref, o_ref in ((gk_ref, wk_ref, ok_ref), (gq_ref, wq_ref, oq_ref)):
        h = (xn * g_ref[...]).astype(BF16)
        for c in range(w_ref.shape[1] // col_chunk):
            a = _dot(h, w_ref[:, c * col_chunk:(c + 1) * col_chunk])
            for s in range(col_chunk // LANES):
                col = c * col_chunk + s * LANES
                blk = a[:, s * LANES:(s + 1) * LANES]
                if col < n_rope:
                    partner = jnp.where(first, pltpu.roll(blk, LANES - half, 1),
                                        pltpu.roll(blk, half, 1))
                    blk = blk * cos + partner * sin
                o_ref[:, col:col + LANES] = blk.astype(BF16)


def _proj_rope(x, g_kv, g_q, w_kv, w_q, cos, sin, n_rope):
    t, d = x.shape
    nk, nq = w_kv.shape[1], w_q.shape[1]
    row = lambda i: (i, 0)
    fixed = lambda i: (0, 0)
    return pl.pallas_call(
        functools.partial(_proj_rope_kernel, n_rope=n_rope, col_chunk=256),
        grid=(t // ROW_TILE,),
        in_specs=[
            pl.BlockSpec((ROW_TILE, d), row),
            pl.BlockSpec((1, d), fixed),
            pl.BlockSpec((1, d), fixed),
            pl.BlockSpec((d, nk), fixed),
            pl.BlockSpec((d, nq), fixed),
            pl.BlockSpec((ROW_TILE, LANES), row),
            pl.BlockSpec((ROW_TILE, LANES), row),
        ],
        out_specs=[pl.BlockSpec((ROW_TILE, nk), row), pl.BlockSpec((ROW_TILE, nq), row)],
        out_shape=[jax.ShapeDtypeStruct((t, nk), BF16), jax.ShapeDtypeStruct((t, nq), BF16)],
        compiler_params=_params("parallel"),
        name="proj_kv_q",
    )(x, g_kv, g_q, w_kv, w_q, cos, sin)


def _diff_kernel(lam_ref, sg_ref, q_ref, k_ref, v_ref, o_ref, s_sc, p_sc, m_sc, va_sc, *, tq,
                 lam_init):
    lp = lam_ref[...]
    lam = (jnp.exp(jnp.sum(lp[0:1] * lp[1:2], axis=1, keepdims=True))
           - jnp.exp(jnp.sum(lp[2:3] * lp[3:4], axis=1, keepdims=True)) + lam_init)
    va_sc[:, :LANES] = v_ref[...]
    va_sc[:, LANES:] = jnp.ones(v_ref.shape, BF16)

    def keep_fn(row, col):
        return (col >> CHUNK_SHIFT) <= (row >> CHUNK_SHIFT)

    def emit(qi, r1, r2):
        o = r1[:, :LANES] / r1[:, LANES:] - lam * (r2[:, :LANES] / r2[:, LANES:])
        o = _rms(o, sg_ref[...]) * (1.0 - lam_init)
        o_ref[qi * tq:(qi + 1) * tq, :] = o.astype(BF16)

    _attention_sweep(q_ref, k_ref, lambda hh: va_sc, (s_sc, p_sc, m_sc), tq, None, keep_fn, emit)


def _diff_attention(lam_rows, subln_g, qb, kv, batch, seq, lam_init):
    t = batch * seq
    return pl.pallas_call(
        functools.partial(_diff_kernel, tq=ATT_TILE, lam_init=lam_init),
        grid=(batch, DIFF_HEADS),
        in_specs=[
            pl.BlockSpec((8, LANES), lambda b, h: (0, 0)),
            pl.BlockSpec((1, LANES), lambda b, h: (0, 0)),
            pl.BlockSpec((seq, LANES), lambda b, h: (b, h)),
            pl.BlockSpec((seq, LANES), lambda b, h: (b, h)),
            pl.BlockSpec((seq, LANES), lambda b, h: (b, DIFF_HEADS + h)),
        ],
        out_specs=pl.BlockSpec((seq, LANES), lambda b, h: (b, h)),
        out_shape=jax.ShapeDtypeStruct((t, DIFF_HEADS * LANES), BF16),
        scratch_shapes=_attention_scratch(ATT_TILE, seq) + [pltpu.VMEM((seq, 2 * LANES), BF16)],
        compiler_params=_params("parallel", "parallel"),
        name="diff_attention",
    )(lam_rows, subln_g, qb, kv, kv)


def _moe_pre_kernel(x_ref, ys_ref, ym_ref, wo_ref, g_ref, wr_ref, tri_ref, x3_ref, hp_ref, route_ref,
                    cnt_ref, run_sc):
    @pl.when(pl.program_id(0) == 0)
    def _():
        run_sc[...] = jnp.zeros_like(run_sc)

    x = _mixed_residual(x_ref, ys_ref, ym_ref, wo_ref)
    x3_ref[...] = x
    hf = _rms(x, g_ref[...])
    hp_ref[...] = hf

    h_hi = hf.astype(BF16)
    h_lo = (hf - h_hi.astype(F32)).astype(BF16)
    logits = _dot(jnp.concatenate([h_hi, h_lo, h_hi], axis=1), wr_ref[...])
    lane = lax.broadcasted_iota(jnp.int32, logits.shape, 1)
    lanef = lane.astype(F32)
    lg = jnp.where(lane < N_EXPERTS, logits, NEG)
    v1 = jnp.max(lg, axis=1, keepdims=True)
    i1 = jnp.min(jnp.where(lg == v1, lanef, float(LANES)), axis=1, keepdims=True)
    lg2 = jnp.where(lanef == i1, NEG, lg)
    v2 = jnp.max(lg2, axis=1, keepdims=True)
    i2 = jnp.min(jnp.where(lg2 == v2, lanef, float(LANES)), axis=1, keepdims=True)
    e = jnp.exp(v2 - v1)
    g1 = 1.0 / (1.0 + e)
    g2 = e / (1.0 + e)

    oh1 = lanef == i1
    oh2 = lanef == i2
    oh = jnp.where(oh1 | oh2, 1.0, 0.0)
    before = _dot(tri_ref[...], oh.astype(BF16)) + run_sc[...]
    r1 = jnp.sum(jnp.where(oh1, before, 0.0), axis=1, keepdims=True)
    r2 = jnp.sum(jnp.where(oh2, before, 0.0), axis=1, keepdims=True)
    run_sc[...] += jnp.sum(oh, axis=0, keepdims=True)
    cnt_ref[...] = run_sc[...]

    route = jnp.where(lane == 0, i1, 0.0)
    for ln, val in ((1, i2), (2, r1), (3, r2), (4, g1), (5, g2)):
        route = jnp.where(lane == ln, val, route)
    route_ref[...] = route


def _moe_pre(x, y_self, y_mem, w_out, g, w_router_pad, tri):
    t, d = x.shape
    tm = tri.shape[0]
    row = lambda i: (i, 0)
    fixed = lambda i: (0, 0)
    return pl.pallas_call(
        _moe_pre_kernel,
        grid=(t // tm,),
        in_specs=[
            pl.BlockSpec((tm, d), row),
            pl.BlockSpec((tm, y_self.shape[1]), row),
            pl.BlockSpec((tm, y_mem.shape[1]), row),
            pl.BlockSpec(w_out.shape, fixed),
            pl.BlockSpec((1, d), fixed),
            pl.BlockSpec((3 * d, LANES), fixed),
            pl.BlockSpec((tm, tm), fixed),
        ],
        out_specs=[
            pl.BlockSpec((tm, d), row),
            pl.BlockSpec((tm, d), row),
            pl.BlockSpec((tm, LANES), row),
            pl.BlockSpec((1, LANES), fixed),
        ],
        out_shape=[
            jax.ShapeDtypeStruct((t, d), F32),
            jax.ShapeDtypeStruct((t, d), F32),
            jax.ShapeDtypeStruct((t, LANES), F32),
            jax.ShapeDtypeStruct((1, LANES), F32),
        ],
        scratch_shapes=[pltpu.VMEM((1, LANES), F32)],
        compiler_params=_params("arbitrary"),
        name="moe_pre",
    )(x, y_self, y_mem, w_out, g, w_router_pad, tri)


def _dispatch_kernel(hi_ref, lo_ref, last_ref, ntile_ref, h_ref, out_ref, zero_sc, sem, zsem, *,
                     tile, groups_per_tile):
    i = pl.program_id(0)
    half = groups_per_tile // 2
    n_tiles = out_ref.shape[0] // groups_per_tile
    n_active = last_ref[N_EXPERTS - 1] + 1

    def zero_tile(tile_idx, wait):
        for part in range(2):
            g0 = tile_idx * groups_per_tile + part * half
            copy = pltpu.make_async_copy(zero_sc, out_ref.at[pl.ds(g0, half)], zsem)
            if wait:
                copy.wait()
            else:
                copy.start()

    @pl.when(i == 0)
    def _():
        zero_sc[...] = jnp.zeros_like(zero_sc)
        for wait in (False, True):
            for e in range(N_EXPERTS):
                @pl.when(ntile_ref[e] > 0)
                def _():
                    zero_tile(last_ref[e], wait)

                @pl.when(n_active + e < n_tiles)
                def _():
                    zero_tile(n_active + e, wait)

    base = i * tile * 2

    def row_copy(j, u, k):
        n = base + 2 * (j * SUBLANES + u) + k
        return pltpu.make_async_copy(h_ref.at[j, pl.ds(u, 1)],
                                     out_ref.at[hi_ref[n], pl.ds(lo_ref[n], 1)], sem)

    def start(j, c):
        for u in range(SUBLANES):
            row_copy(j, u, 0).start(priority=0)
            row_copy(j, u, 1).start(priority=1)
        return c

    def wait(j, c):
        for u in range(SUBLANES):
            row_copy(j, u, 0).wait()
            row_copy(j, u, 1).wait()
        return c

    lax.fori_loop(0, tile // SUBLANES, start, 0)
    lax.fori_loop(0, tile // SUBLANES, wait, 0)


def _dispatch(pos_hi, pos_lo, last_tile, tiles_per, h3, n_slots, moe_tile):
    ng, _, d = h3.shape
    groups_per_tile = moe_tile // SUBLANES
    gt = GATHER_TILE // SUBLANES
    return pl.pallas_call(
        functools.partial(_dispatch_kernel, tile=GATHER_TILE, groups_per_tile=groups_per_tile),
        grid_spec=pltpu.PrefetchScalarGridSpec(
            num_scalar_prefetch=4,
            grid=(ng // gt,),
            in_specs=[pl.BlockSpec((gt, SUBLANES, d), lambda i, *_: (i, 0, 0))],
            out_specs=pl.BlockSpec(memory_space=pl.ANY),
            scratch_shapes=[pltpu.VMEM((groups_per_tile // 2, SUBLANES, d), F32),
                            pltpu.SemaphoreType.DMA(()), pltpu.SemaphoreType.DMA(())],
        ),
        out_shape=jax.ShapeDtypeStruct((n_slots // SUBLANES, SUBLANES, d), F32),
        compiler_params=_params("arbitrary"),
        name="moe_dispatch",
    )(pos_hi, pos_lo, last_tile, tiles_per, h3)


def _moe_kernel(te_ref, na_ref, x_ref, wg_ref, wu_ref, wd_ref, o_ref, xb_sc):
    del te_ref
    i = pl.program_id(0)
    f = pl.program_id(1)

    @pl.when(f == 0)
    def _():
        o_ref[...] = jnp.zeros_like(o_ref)

    @pl.when(i < na_ref[0])
    def _():
        @pl.when(f == 0)
        def _():
            xb_sc[...] = x_ref[...].astype(BF16)

        xb = xb_sc[...]
        g = _dot(xb, wg_ref[0].astype(BF16))
        u = _dot(xb, wu_ref[0].astype(BF16))
        o_ref[...] += _dot(_swiglu_hidden(g, u), wd_ref[0].astype(BF16))


def _moe_experts(tile_expert, n_active, xs, w_gate_up, w_down, n_tiles):
    d = w_down.shape[2]
    ff = w_down.shape[1]
    ct = MOE_COL_TILE
    nf = ff // ct
    tm = MOE_ROW_TILE

    def row(i, f, te, na):
        return (jnp.maximum(jnp.minimum(i, na[0] - 1), 0), 0)

    def col(i, f, na):
        return jnp.where(i < na[0], f, nf - 1)

    return pl.pallas_call(
        _moe_kernel,
        grid_spec=pltpu.PrefetchScalarGridSpec(
            num_scalar_prefetch=2,
            grid=(n_tiles, nf),
            in_specs=[
                pl.BlockSpec((tm, d), row),
                pl.BlockSpec((1, d, ct), lambda i, f, te, na: (te[i], 0, col(i, f, na))),
                pl.BlockSpec((1, d, ct), lambda i, f, te, na: (te[i], 0, nf + col(i, f, na))),
                pl.BlockSpec((1, ct, d), lambda i, f, te, na: (te[i], col(i, f, na), 0)),
            ],
            out_specs=pl.BlockSpec((tm, d), lambda i, f, te, na: (i, 0)),
            scratch_shapes=[pltpu.VMEM((tm, d), BF16)],
        ),
        out_shape=jax.ShapeDtypeStruct((n_tiles * tm, d), F32),
        compiler_params=_params("arbitrary", "arbitrary"),
        name="moe_experts",
    )(tile_expert, n_active, xs, w_gate_up, w_gate_up, w_down)


def _combine_kernel(hi_ref, lo_ref, x_ref, route_ref, g_ref, y_ref, o_ref, buf, sem, *, tile):
    i = pl.program_id(0)
    slot = i & 1
    groups = tile // SUBLANES

    def row_copy(step, sl, j, u, k):
        n = (step * tile + j * SUBLANES + u) * 2 + k
        return pltpu.make_async_copy(y_ref.at[hi_ref[n], pl.ds(lo_ref[n], 1)],
                                     buf.at[sl, k, j, pl.ds(u, 1)], sem.at[sl])

    def gather(step, sl):
        def start(j, c):
            for u in range(SUBLANES):
                row_copy(step, sl, j, u, 0).start(priority=0)
                row_copy(step, sl, j, u, 1).start(priority=1)
            return c
        lax.fori_loop(0, groups, start, 0)

    @pl.when(i == 0)
    def _():
        gather(0, 0)

    @pl.when(i + 1 < pl.num_programs(0))
    def _():
        gather(i + 1, 1 - slot)

    def wait(j, c):
        for u in range(SUBLANES):
            row_copy(i, slot, j, u, 0).wait()
            row_copy(i, slot, j, u, 1).wait()
        return c

    lax.fori_loop(0, groups, wait, 0)
    route = route_ref[...]
    d = x_ref.shape[1]
    y0 = buf[slot, 0].reshape(tile, d)
    y1 = buf[slot, 1].reshape(tile, d)
    y = x_ref[...] + route[:, 4:5] * y0 + route[:, 5:6] * y1
    o_ref[...] = _rms(y, g_ref[...])


def _combine(pos_hi, pos_lo, x, route, g, y3):
    t, d = x.shape
    groups = GATHER_TILE // SUBLANES
    return pl.pallas_call(
        functools.partial(_combine_kernel, tile=GATHER_TILE),
        grid_spec=pltpu.PrefetchScalarGridSpec(
            num_scalar_prefetch=2,
            grid=(t // GATHER_TILE,),
            in_specs=[
                pl.BlockSpec((GATHER_TILE, d), lambda i, *_: (i, 0)),
                pl.BlockSpec((GATHER_TILE, LANES), lambda i, *_: (i, 0)),
                pl.BlockSpec((1, d), lambda i, *_: (0, 0)),
                pl.BlockSpec(memory_space=pl.ANY),
            ],
            out_specs=pl.BlockSpec((GATHER_TILE, d), lambda i, *_: (i, 0)),
            scratch_shapes=[pltpu.VMEM((2, 2, groups, SUBLANES, d), F32),
                            pltpu.SemaphoreType.DMA((2,))],
        ),
        out_shape=jax.ShapeDtypeStruct((t, d), F32),
        compiler_params=_params("arbitrary"),
        name="moe_combine",
    )(pos_hi, pos_lo, x, route, g, y3)


def _row(v):
    return v.reshape(1, -1).astype(F32)


def _pad_lanes(v):
    return jnp.pad(v.astype(F32), (0, LANES - v.shape[0])).reshape(1, LANES)


def kernel(x, mem, positions, mix_norm_g, ffn_norm_g, mem_norm_g, w_mem_kv, w_out, w_in_a, b_forget, w_q_b, lambda_q1, lambda_k1, lambda_q2, lambda_k2, subln_g, kv_norm_g, w_kv_shared, w_gate_up_dense, w_down_dense, w_router, w_gate_up_moe, w_down_moe, final_norm_g):
    batch, seq, d = x.shape
    mem_tokens = mem.shape[1]
    t = batch * seq
    assert w_in_a.shape[0] == 1 and w_q_b.shape[0] == 1 and w_out.shape[0] == 2
    fox_w = FOX_HEADS * HEAD_DIM
    diff_w = DIFF_HEADS * 2 * HEAD_DIM
    scale = HEAD_DIM ** -0.5

    xf = x.reshape(t, d)
    memf = mem.reshape(batch * mem_tokens, d)

    wa = w_in_a[0]
    w_fl = jnp.pad(wa[:, 3 * fox_w:3 * fox_w + FOX_HEADS], ((0, 0), (0, LANES - FOX_HEADS)))
    w_a = jnp.concatenate(
        [wa[:, :fox_w] * scale, wa[:, fox_w:3 * fox_w], wa[:, 3 * fox_w + FOX_HEADS:] * scale, w_fl],
        axis=1).astype(BF16)
    w_b = (w_q_b[0] * scale).astype(BF16)
    w_kv = w_kv_shared.astype(BF16)
    w_o = w_out.astype(BF16)
    w_mkv = w_mem_kv.astype(BF16)
    w_gu_d = w_gate_up_dense[0].astype(BF16)
    w_dn_d = w_down_dense[0].astype(BF16)
    w_gu_m = w_gate_up_moe[0]
    w_dn_m = w_down_moe[0]
    w_r = jnp.pad(w_router[0].astype(F32), ((0, 0), (0, LANES - N_EXPERTS)))
    w_r_hi = w_r.astype(BF16)
    w_r_lo = (w_r - w_r_hi.astype(F32)).astype(BF16)
    w_r = jnp.concatenate([w_r_hi, w_r_hi, w_r_lo], axis=0)

    qkvm, log_f = _proj_a(xf, _row(mix_norm_g[0]), w_a, _pad_lanes(b_forget[0]))
    cum_t = _cumsum(log_f, batch, seq)
    y_self = _fox_attention(qkvm, cum_t, batch, seq)
    mkv0 = _norm_matmul(memf, _row(mem_norm_g[0]), w_mkv[0], "mem_kv0")
    y_mem = _mem_attention(qkvm, 3 * fox_w // MEM_WIDTH, mkv0, batch, seq, mem_tokens)
    x2 = _dense_ffn(xf, y_self, y_mem, w_o[0], _row(ffn_norm_g[0]), w_gu_d, w_dn_d)

    half = ROT_DIM // 2
    inv_freq = ROPE_THETA ** (-(jnp.arange(half, dtype=F32) * 2.0 / ROT_DIM))
    inv_lane = jnp.tile(inv_freq, LANES // half).reshape(1, LANES)
    cos_t, sin_t = _rope_tables(positions.reshape(t, 1).astype(jnp.int32), inv_lane)
    kv, qb = _proj_rope(x2, _row(kv_norm_g), _row(mix_norm_g[1]), w_kv, w_b, cos_t, sin_t, diff_w)

    lam_init = 0.8 - 0.6 * math.exp(-0.3 * 1)
    lam_rows = jnp.concatenate(
        [jnp.pad(v[0].astype(F32), (0, LANES - HEAD_DIM)).reshape(1, LANES)
         for v in (lambda_q1, lambda_k1, lambda_q2, lambda_k2)]
        + [jnp.zeros((4, LANES), F32)], axis=0)
    y_self = _diff_attention(lam_rows, _row(subln_g[0]), qb, kv, batch, seq, lam_init)
    mkv1 = _norm_matmul(memf, _row(mem_norm_g[1]), w_mkv[1], "mem_kv1")
    y_mem = _mem_attention(qb, diff_w // MEM_WIDTH, mkv1, batch, seq, mem_tokens)

    tri = jnp.tril(jnp.ones((MOE_PRE_TILE, MOE_PRE_TILE), BF16), -1)
    x3, h_rows, route, cnt = _moe_pre(x2, y_self, y_mem, w_o[1], _row(ffn_norm_g[1]), w_r, tri)
    tm = MOE_ROW_TILE
    n_tiles = 2 * t // tm + N_EXPERTS
    counts = cnt[0, :N_EXPERTS].astype(jnp.int32)
    tiles_per = (counts + tm - 1) // tm
    tile_end = jnp.cumsum(tiles_per)
    group_off = (tile_end - tiles_per) * tm
    n_active = tile_end[-1:]
    tile_id = jnp.minimum(jnp.arange(n_tiles, dtype=jnp.int32), n_active - 1)
    tile_expert = jnp.sum(tile_id[:, None] >= tile_end[None, :], axis=1).astype(jnp.int32)
    idx = route[:, 0:2].astype(jnp.int32)
    rank = route[:, 2:4].astype(jnp.int32)
    expert_ids = jnp.arange(N_EXPERTS, dtype=jnp.int32)
    off = jnp.sum(jnp.where(idx[..., None] == expert_ids, group_off, 0), axis=-1)
    pos = (off + rank).reshape(2 * t)
    pos_hi, pos_lo = pos // SUBLANES, pos % SUBLANES
    last_tile = jnp.maximum(tile_end - 1, 0).astype(jnp.int32)

    h3 = h_rows.reshape(t // SUBLANES, SUBLANES, d)
    xs3 = _dispatch(pos_hi, pos_lo, last_tile, tiles_per.astype(jnp.int32), h3, n_tiles * tm, tm)
    y_sorted = _moe_experts(tile_expert, n_active.astype(jnp.int32), xs3.reshape(n_tiles * tm, d),
                            w_gu_m, w_dn_m, n_tiles)
    y3 = y_sorted.reshape(n_tiles * tm // SUBLANES, SUBLANES, d)
    out = _combine(pos_hi, pos_lo, x3, route, _row(final_norm_g), y3)
    return out.reshape(batch, seq, d)
```

```python
import functools
import math

import jax
import jax.numpy as jnp
from jax import lax
from jax.experimental import pallas as pl
from jax.experimental.pallas import tpu as pltpu

F32 = jnp.float32
BF16 = jnp.bfloat16

HEAD_DIM = 64
LANES = 128
SUBLANES = 8
CHUNK_SHIFT = 6
FOX_HEADS = 12
DIFF_HEADS = 6
MEM_HEADS = 4
MEM_WIDTH = MEM_HEADS * HEAD_DIM
ROPE_THETA = 500000.0
ROT_DIM = HEAD_DIM // 4
N_EXPERTS = 8
EPS = 1e-5
NEG = -1e30
VMEM_LIMIT = 48 * 1024 * 1024
FFN_VMEM_LIMIT = 56 * 1024 * 1024

ROW_TILE = 512
FFN_ROW_TILE = 1024
FFN_COL_TILE = 256
ATT_TILE = 256
ROW_CHUNK = 16
ATT_SLOTS = 2
MEM_Q_TILE = 512
MOE_PRE_TILE = 1024
MOE_ROW_TILE = 1024
MOE_COL_TILE = 512
GATHER_TILE = 256


def _params(*sem):
    return pltpu.CompilerParams(dimension_semantics=sem, vmem_limit_bytes=VMEM_LIMIT)


def _rms(x, g):
    ms = jnp.mean(x * x, axis=-1, keepdims=True)
    return x * lax.rsqrt(ms + EPS) * g


def _nt_dot(a, b):
    return lax.dot_general(a, b, (((1,), (1,)), ((), ())), preferred_element_type=F32)


def _dot(a, b):
    return jnp.dot(a, b, preferred_element_type=F32)


def _proj_a_kernel(x_ref, g_ref, w_ref, b_ref, o_ref, lf_ref, *, n_main, col_chunk):
    h = _rms(x_ref[...], g_ref[...]).astype(BF16)
    for c in range(n_main // col_chunk):
        cs = slice(c * col_chunk, (c + 1) * col_chunk)
        o_ref[:, cs] = _dot(h, w_ref[:, cs]).astype(BF16)
    z = _dot(h, w_ref[:, n_main:]) + b_ref[...]
    lf_ref[...] = jnp.minimum(z, 0.0) - jnp.log1p(jnp.exp(-jnp.abs(z)))


def _proj_a(x, g, w, b_pad):
    t, d = x.shape
    n_all = w.shape[1]
    n_main = n_all - LANES
    return pl.pallas_call(
        functools.partial(_proj_a_kernel, n_main=n_main, col_chunk=512),
        grid=(t // ROW_TILE,),
        in_specs=[
            pl.BlockSpec((ROW_TILE, d), lambda i: (i, 0)),
            pl.BlockSpec((1, d), lambda i: (0, 0)),
            pl.BlockSpec((d, n_all), lambda i: (0, 0)),
            pl.BlockSpec((1, LANES), lambda i: (0, 0)),
        ],
        out_specs=[
            pl.BlockSpec((ROW_TILE, n_main), lambda i: (i, 0)),
            pl.BlockSpec((ROW_TILE, LANES), lambda i: (i, 0)),
        ],
        out_shape=[
            jax.ShapeDtypeStruct((t, n_main), BF16),
            jax.ShapeDtypeStruct((t, LANES), F32),
        ],
        compiler_params=_params("parallel"),
        name="proj_a",
    )(x, g, w, b_pad)


def _cumsum_kernel(lf_ref, ct_ref):
    x = lf_ref[...]
    s = x.shape[0]
    row = lax.broadcasted_iota(jnp.int32, x.shape, 0)
    sh = 1
    while sh < s:
        x = x + jnp.where(row >= sh, pltpu.roll(x, sh, 0), 0.0)
        sh *= 2
    ct_ref[0] = x.T


def _cumsum(lf, batch, seq):
    return pl.pallas_call(
        _cumsum_kernel,
        grid=(batch,),
        in_specs=[pl.BlockSpec((seq, LANES), lambda b: (b, 0))],
        out_specs=pl.BlockSpec((1, LANES, seq), lambda b: (b, 0, 0)),
        out_shape=jax.ShapeDtypeStruct((batch, LANES, seq), F32),
        compiler_params=_params("parallel"),
        name="cumsum",
    )(lf)


def _softmax_rows(s_ref, p_ref, m_ref, hi, tq, col_bias, keep_fn):
    lo = hi - tq
    shape = (ROW_CHUNK, LANES)

    def visibility(r, c):
        r0, c0 = r * ROW_CHUNK, c * LANES - lo
        if c0 < 0 or keep_fn(r0, c0 + LANES - 1):
            return "all"
        if not keep_fn(r0 + ROW_CHUNK - 1, c0):
            return "none"
        ri = lax.broadcasted_iota(jnp.int32, shape, 0) + r0
        ci = lax.broadcasted_iota(jnp.int32, shape, 1) + c0
        return keep_fn(ri, ci)

    def load(r, c, vis):
        t = s_ref[r * ROW_CHUNK:(r + 1) * ROW_CHUNK, c * LANES:(c + 1) * LANES]
        if col_bias is not None:
            t = t - col_bias[:, c * LANES:(c + 1) * LANES]
        return t if isinstance(vis, str) else jnp.where(vis, t, NEG)

    for r in range(tq // ROW_CHUNK):
        m_acc = None
        for c in range(hi // LANES):
            vis = visibility(r, c)
            if isinstance(vis, str) and vis == "none":
                continue
            t = load(r, c, vis)
            m_acc = t if m_acc is None else jnp.maximum(m_acc, t)
        m_ref[r * ROW_CHUNK:(r + 1) * ROW_CHUNK, :] = jnp.broadcast_to(
            jnp.max(m_acc, axis=1, keepdims=True), shape)
    for r in range(tq // ROW_CHUNK):
        rows = slice(r * ROW_CHUNK, (r + 1) * ROW_CHUNK)
        m = m_ref[rows, :]
        for c in range(hi // LANES):
            cols = slice(c * LANES, (c + 1) * LANES)
            vis = visibility(r, c)
            if isinstance(vis, str) and vis == "none":
                p_ref[rows, cols] = jnp.zeros(shape, BF16)
            else:
                p_ref[rows, cols] = jnp.exp((load(r, c, vis) - m).astype(BF16))


def _attention_sweep(q_ref, k_ref, v_aug, scratch, tq, col_bias_fn, keep_fn, emit):
    seq = q_ref.shape[0]
    n_items = 2 * (seq // tq)
    n_slots = scratch[0].shape[0]
    qh, res = {}, {}

    def bufs(n):
        return [sc.at[n % n_slots] for sc in scratch]

    def score(n):
        qi, hh = divmod(n, 2)
        hi = (qi + 1) * tq
        if hh == 0:
            qh[qi] = _split_heads(q_ref[qi * tq:hi, :])
        bufs(n)[0][:, :hi] = _nt_dot(qh[qi][hh], k_ref[:hi, :])

    def softmax(n):
        qi, hh = divmod(n, 2)
        hi = (qi + 1) * tq
        bias = None if col_bias_fn is None else col_bias_fn(hh, hi)
        _softmax_rows(*bufs(n), hi, tq, bias, keep_fn)

    def values(n):
        qi, hh = divmod(n, 2)
        hi = (qi + 1) * tq
        res[hh] = _dot(bufs(n)[1][:, :hi], v_aug(hh)[:hi, :])
        if hh == 1:
            emit(qi, res[0], res[1])

    score(0)
    for n in range(n_items):
        if n + 1 < n_items:
            score(n + 1)
        softmax(n)
        if n:
            values(n - 1)
    values(n_items - 1)


def _split_heads(q):
    low = lax.broadcasted_iota(jnp.int32, (1, LANES), 1) < HEAD_DIM
    zero = jnp.zeros_like(q)
    return jnp.where(low, q, zero), jnp.where(low, zero, q)


def _fox_kernel(q_ref, k_ref, v_ref, ct_ref, o_ref, s_sc, p_sc, m_sc, va_sc, *, tq):
    sub = (2 * pl.program_id(1)) & 7
    low = lax.broadcasted_iota(jnp.int32, (1, LANES), 1) < HEAD_DIM
    v = v_ref[...]
    one = jnp.ones_like(v)
    va_sc[0] = jnp.where(low, v, one)
    va_sc[1] = jnp.where(low, one, v)

    def keep_fn(row, col):
        return col <= row

    def col_bias(hh, hi):
        return ct_ref[0, pl.ds(sub + hh, 1), :hi]

    def emit(qi, r0, r1):
        num = jnp.where(low, r0, r1)
        den = pltpu.roll(jnp.where(low, r1, r0), HEAD_DIM, 1)
        o_ref[qi * tq:(qi + 1) * tq, :] = (num / den).astype(BF16)

    _attention_sweep(q_ref, k_ref, lambda hh: va_sc.at[hh], (s_sc, p_sc, m_sc), tq, col_bias,
                     keep_fn, emit)


def _attention_scratch(tq, seq):
    n = 2 * ATT_SLOTS
    return [pltpu.VMEM((n, tq, seq), F32), pltpu.VMEM((n, tq, seq), BF16),
            pltpu.VMEM((n, tq, LANES), F32)]


def _fox_attention(qkvm, cum_t, batch, seq):
    t = batch * seq
    npair = FOX_HEADS // 2
    return pl.pallas_call(
        functools.partial(_fox_kernel, tq=ATT_TILE),
        grid=(batch, npair),
        in_specs=[
            pl.BlockSpec((seq, LANES), lambda b, p: (b, p)),
            pl.BlockSpec((seq, LANES), lambda b, p: (b, npair + p)),
            pl.BlockSpec((seq, LANES), lambda b, p: (b, 2 * npair + p)),
            pl.BlockSpec((1, 8, seq), lambda b, p: (b, p // 4, 0)),
        ],
        out_specs=pl.BlockSpec((seq, LANES), lambda b, p: (b, p)),
        out_shape=jax.ShapeDtypeStruct((t, npair * LANES), BF16),
        scratch_shapes=_attention_scratch(ATT_TILE, seq) + [pltpu.VMEM((2, seq, LANES), BF16)],
        compiler_params=_params("parallel", "parallel"),
        name="fox_attention",
    )(qkvm, qkvm, qkvm, cum_t)


def _mem_kernel(q_ref, mk_ref, mv_ref, o_ref):
    q = q_ref[...]
    mk = mk_ref[...]
    mv = mv_ref[...]
    lane = lax.broadcasted_iota(jnp.int32, (1, MEM_WIDTH), 1)
    zero = jnp.zeros_like(q)
    out = jnp.zeros(q.shape, F32)
    for h in range(MEM_HEADS):
        hm = (lane >= h * HEAD_DIM) & (lane < (h + 1) * HEAD_DIM)
        s = _nt_dot(jnp.where(hm, q, zero), mk)
        p = jnp.exp(s - jnp.max(s, axis=1, keepdims=True))
        l = jnp.sum(p, axis=1, keepdims=True)
        out = jnp.where(hm, _dot(p.astype(BF16), mv) / l, out)
    o_ref[...] = out.astype(BF16)


def _mem_attention(q_arr, q_col_block, mkv, batch, seq, mem_tokens):
    nq = seq // MEM_Q_TILE
    return pl.pallas_call(
        _mem_kernel,
        grid=(batch, nq),
        in_specs=[
            pl.BlockSpec((MEM_Q_TILE, MEM_WIDTH), lambda b, i: (b * nq + i, q_col_block)),
            pl.BlockSpec((mem_tokens, MEM_WIDTH), lambda b, i: (b, 0)),
            pl.BlockSpec((mem_tokens, MEM_WIDTH), lambda b, i: (b, 1)),
        ],
        out_specs=pl.BlockSpec((MEM_Q_TILE, MEM_WIDTH), lambda b, i: (b * nq + i, 0)),
        out_shape=jax.ShapeDtypeStruct((batch * seq, MEM_WIDTH), BF16),
        compiler_params=_params("parallel", "parallel"),
        name="mem_attention",
    )(q_arr, mkv, mkv)


def _norm_matmul_kernel(x_ref, g_ref, w_ref, o_ref):
    h = _rms(x_ref[...], g_ref[...]).astype(BF16)
    o_ref[...] = _dot(h, w_ref[...]).astype(o_ref.dtype)


def _norm_matmul(x, g, w, name):
    t, d = x.shape
    n = w.shape[1]
    return pl.pallas_call(
        _norm_matmul_kernel,
        grid=(t // ROW_TILE,),
        in_specs=[
            pl.BlockSpec((ROW_TILE, d), lambda i: (i, 0)),
            pl.BlockSpec((1, d), lambda i: (0, 0)),
            pl.BlockSpec((d, n), lambda i: (0, 0)),
        ],
        out_specs=pl.BlockSpec((ROW_TILE, n), lambda i: (i, 0)),
        out_shape=jax.ShapeDtypeStruct((t, n), BF16),
        compiler_params=_params("parallel"),
        name=name,
    )(x, g, w)


def _mixed_residual(x_ref, ys_ref, ym_ref, wo_ref):
    sw = ys_ref.shape[1]
    return x_ref[...] + _dot(ys_ref[...], wo_ref[:sw, :]) + _dot(ym_ref[...], wo_ref[sw:, :])


def _swiglu_hidden(g, u):
    return (g * jax.nn.sigmoid(g) * u).astype(BF16)


def _ffn_kernel(x_ref, ys_ref, ym_ref, wo_ref, g_ref, wgu_ref, wd_ref, o_ref, h_sc, *, col):
    ff = wd_ref.shape[0]
    x = _mixed_residual(x_ref, ys_ref, ym_ref, wo_ref)
    h_sc[...] = _rms(x, g_ref[...]).astype(BF16)
    o_ref[...] = x
    for c in range(ff // col):
        h = h_sc[...]
        gate = _dot(h, wgu_ref[:, c * col:(c + 1) * col])
        up = _dot(h, wgu_ref[:, ff + c * col:ff + (c + 1) * col])
        o_ref[...] += _dot(_swiglu_hidden(gate, up), wd_ref[c * col:(c + 1) * col, :])


def _dense_ffn(x, y_self, y_mem, w_out, g, w_gate_up, w_down):
    t, d = x.shape
    tm = FFN_ROW_TILE
    row = lambda i: (i, 0)
    resident = dict(index_map=lambda i: (0, 0), pipeline_mode=pl.Buffered(1))
    return pl.pallas_call(
        functools.partial(_ffn_kernel, col=FFN_COL_TILE),
        grid=(t // tm,),
        in_specs=[
            pl.BlockSpec((tm, d), row),
            pl.BlockSpec((tm, y_self.shape[1]), row),
            pl.BlockSpec((tm, y_mem.shape[1]), row),
            pl.BlockSpec(w_out.shape, **resident),
            pl.BlockSpec((1, d), lambda i: (0, 0)),
            pl.BlockSpec(w_gate_up.shape, **resident),
            pl.BlockSpec(w_down.shape, **resident),
        ],
        out_specs=pl.BlockSpec((tm, d), row),
        out_shape=jax.ShapeDtypeStruct((t, d), F32),
        scratch_shapes=[pltpu.VMEM((tm, d), BF16)],
        compiler_params=pltpu.CompilerParams(dimension_semantics=("parallel",),
                                             vmem_limit_bytes=FFN_VMEM_LIMIT),
        name="dense_ffn",
    )(x, y_self, y_mem, w_out, g, w_gate_up, w_down)


def _rope_kernel(pos_ref, inv_ref, cos_ref, sin_ref):
    pos = pos_ref[0].astype(F32)
    ang = (inv_ref[...] * pos).T
    lane = lax.broadcasted_iota(jnp.int32, ang.shape, 1)
    rot = (lane & (HEAD_DIM - 1)) < ROT_DIM
    first = (lane & (ROT_DIM - 1)) < (ROT_DIM // 2)
    sn = jnp.sin(ang)
    cos_ref[...] = jnp.where(rot, jnp.cos(ang), 1.0)
    sin_ref[...] = jnp.where(rot, jnp.where(first, -sn, sn), 0.0)


def _rope_tables(pos_rows, inv_col):
    steps = pos_rows.shape[0]
    t = steps * ROW_TILE
    return pl.pallas_call(
        _rope_kernel,
        grid=(steps,),
        in_specs=[
            pl.BlockSpec((1, 1, ROW_TILE), lambda i: (i, 0, 0)),
            pl.BlockSpec((LANES, 1), lambda i: (0, 0)),
        ],
        out_specs=[pl.BlockSpec((ROW_TILE, LANES), lambda i: (i, 0))] * 2,
        out_shape=[jax.ShapeDtypeStruct((t, LANES), F32)] * 2,
        compiler_params=_params("parallel"),
        name="rope_tables",
    )(pos_rows, inv_col)


def _proj_rope_kernel(x_ref, gk_ref, gq_ref, wk_ref, wq_ref, cos_ref, sin_ref, ok_ref, oq_ref, *,
                      n_rope, col_chunk):
    x = x_ref[...]
    xn = x * lax.rsqrt(jnp.mean(x * x, axis=-1, keepdims=True) + EPS)
    cos = cos_ref[...]
    sin = sin_ref[...]
    lane = lax.broadcasted_iota(jnp.int32, (1, LANES), 1)
    first = (lane & (ROT_DIM - 1)) < (ROT_DIM // 2)
    half = ROT_DIM // 2
    for g_ref, w_ref, o_ref in ((gk_ref, wk_ref, ok_ref), (gq_ref, wq_ref, oq_ref)):
        h = (xn * g_ref[...]).astype(BF16)
        for c in range(w_ref.shape[1] // col_chunk):
            a = _dot(h, w_ref[:, c * col_chunk:(c + 1) * col_chunk])
            for s in range(col_chunk // LANES):
                col = c * col_chunk + s * LANES
                blk = a[:, s * LANES:(s + 1) * LANES]
                if col < n_rope:
                    partner = jnp.where(first, pltpu.roll(blk, LANES - half, 1),
                                        pltpu.roll(blk, half, 1))
                    blk = blk * cos + partner * sin
                o_ref[:, col:col + LANES] = blk.astype(BF16)


def _proj_rope(x, g_kv, g_q, w_kv, w_q, cos, sin, n_rope):
    t, d = x.shape
    nk, nq = w_kv.shape[1], w_q.shape[1]
    row = lambda i: (i, 0)
    fixed = lambda i: (0, 0)
    return pl.pallas_call(
        functools.partial(_proj_rope_kernel, n_rope=n_rope, col_chunk=256),
        grid=(t // ROW_TILE,),
        in_specs=[
            pl.BlockSpec((ROW_TILE, d), row),
            pl.BlockSpec((1, d), fixed),
            pl.BlockSpec((1, d), fixed),
            pl.BlockSpec((d, nk), fixed),
            pl.BlockSpec((d, nq), fixed),
            pl.BlockSpec((ROW_TILE, LANES), row),
            pl.BlockSpec((ROW_TILE, LANES), row),
        ],
        out_specs=[pl.BlockSpec((ROW_TILE, nk), row), pl.BlockSpec((ROW_TILE, nq), row)],
        out_shape=[jax.ShapeDtypeStruct((t, nk), BF16), jax.ShapeDtypeStruct((t, nq), BF16)],
        compiler_params=_params("parallel"),
        name="proj_kv_q",
    )(x, g_kv, g_q, w_kv, w_q, cos, sin)


def _diff_kernel(lam_ref, sg_ref, q_ref, k_ref, v_ref, o_ref, s_sc, p_sc, m_sc, va_sc, *, tq,
                 lam_init):
    lp = lam_ref[...]
    lam = (jnp.exp(jnp.sum(lp[0:1] * lp[1:2], axis=1, keepdims=True))
           - jnp.exp(jnp.sum(lp[2:3] * lp[3:4], axis=1, keepdims=True)) + lam_init)
    va_sc[:, :LANES] = v_ref[...]
    va_sc[:, LANES:] = jnp.ones(v_ref.shape, BF16)

    def keep_fn(row, col):
        return (col >> CHUNK_SHIFT) <= (row >> CHUNK_SHIFT)

    def emit(qi, r1, r2):
        o = r1[:, :LANES] / r1[:, LANES:] - lam * (r2[:, :LANES] / r2[:, LANES:])
        o = _rms(o, sg_ref[...]) * (1.0 - lam_init)
        o_ref[qi * tq:(qi + 1) * tq, :] = o.astype(BF16)

    _attention_sweep(q_ref, k_ref, lambda hh: va_sc, (s_sc, p_sc, m_sc), tq, None, keep_fn, emit)


def _diff_attention(lam_rows, subln_g, qb, kv, batch, seq, lam_init):
    t = batch * seq
    return pl.pallas_call(
        functools.partial(_diff_kernel, tq=ATT_TILE, lam_init=lam_init),
        grid=(batch, DIFF_HEADS),
        in_specs=[
            pl.BlockSpec((8, LANES), lambda b, h: (0, 0)),
            pl.BlockSpec((1, LANES), lambda b, h: (0, 0)),
            pl.BlockSpec((seq, LANES), lambda b, h: (b, h)),
            pl.BlockSpec((seq, LANES), lambda b, h: (b, h)),
            pl.BlockSpec((seq, LANES), lambda b, h: (b, DIFF_HEADS + h)),
        ],
        out_specs=pl.BlockSpec((seq, LANES), lambda b, h: (b, h)),
        out_shape=jax.ShapeDtypeStruct((t, DIFF_HEADS * LANES), BF16),
        scratch_shapes=_attention_scratch(ATT_TILE, seq) + [pltpu.VMEM((seq, 2 * LANES), BF16)],
        compiler_params=_params("parallel", "parallel"),
        name="diff_attention",
    )(lam_rows, subln_g, qb, kv, kv)


def _moe_pre_kernel(x_ref, ys_ref, ym_ref, wo_ref, g_ref, wr_ref, tri_ref, x3_ref, hp_ref, route_ref,
                    cnt_ref, run_sc):
    @pl.when(pl.program_id(0) == 0)
    def _():
        run_sc[...] = jnp.zeros_like(run_sc)

    x = _mixed_residual(x_ref, ys_ref, ym_ref, wo_ref)
    x3_ref[...] = x
    hf = _rms(x, g_ref[...])
    hp_ref[...] = hf

    h_hi = hf.astype(BF16)
    h_lo = (hf - h_hi.astype(F32)).astype(BF16)
    logits = _dot(jnp.concatenate([h_hi, h_lo, h_hi], axis=1), wr_ref[...])
    lane = lax.broadcasted_iota(jnp.int32, logits.shape, 1)
    lanef = lane.astype(F32)
    lg = jnp.where(lane < N_EXPERTS, logits, NEG)
    v1 = jnp.max(lg, axis=1, keepdims=True)
    i1 = jnp.min(jnp.where(lg == v1, lanef, float(LANES)), axis=1, keepdims=True)
    lg2 = jnp.where(lanef == i1, NEG, lg)
    v2 = jnp.max(lg2, axis=1, keepdims=True)
    i2 = jnp.min(jnp.where(lg2 == v2, lanef, float(LANES)), axis=1, keepdims=True)
    e = jnp.exp(v2 - v1)
    g1 = 1.0 / (1.0 + e)
    g2 = e / (1.0 + e)

    oh1 = lanef == i1
    oh2 = lanef == i2
    oh = jnp.where(oh1 | oh2, 1.0, 0.0)
    before = _dot(tri_ref[...], oh.astype(BF16)) + run_sc[...]
    r1 = jnp.sum(jnp.where(oh1, before, 0.0), axis=1, keepdims=True)
    r2 = jnp.sum(jnp.where(oh2, before, 0.0), axis=1, keepdims=True)
    run_sc[...] += jnp.sum(oh, axis=0, keepdims=True)
    cnt_ref[...] = run_sc[...]

    route = jnp.where(lane == 0, i1, 0.0)
    for ln, val in ((1, i2), (2, r1), (3, r2), (4, g1), (5, g2)):
        route = jnp.where(lane == ln, val, route)
    route_ref[...] = route


def _moe_pre(x, y_self, y_mem, w_out, g, w_router_pad, tri):
    t, d = x.shape
    tm = tri.shape[0]
    row = lambda i: (i, 0)
    fixed = lambda i: (0, 0)
    return pl.pallas_call(
        _moe_pre_kernel,
        grid=(t // tm,),
        in_specs=[
            pl.BlockSpec((tm, d), row),
            pl.BlockSpec((tm, y_self.shape[1]), row),
            pl.BlockSpec((tm, y_mem.shape[1]), row),
            pl.BlockSpec(w_out.shape, fixed),
            pl.BlockSpec((1, d), fixed),
            pl.BlockSpec((3 * d, LANES), fixed),
            pl.BlockSpec((tm, tm), fixed),
        ],
        out_specs=[
            pl.BlockSpec((tm, d), row),
            pl.BlockSpec((tm, d), row),
            pl.BlockSpec((tm, LANES), row),
            pl.BlockSpec((1, LANES), fixed),
        ],
        out_shape=[
            jax.ShapeDtypeStruct((t, d), F32),
            jax.ShapeDtypeStruct((t, d), F32),
            jax.ShapeDtypeStruct((t, LANES), F32),
            jax.ShapeDtypeStruct((1, LANES), F32),
        ],
        scratch_shapes=[pltpu.VMEM((1, LANES), F32)],
        compiler_params=_params("arbitrary"),
        name="moe_pre",
    )(x, y_self, y_mem, w_out, g, w_router_pad, tri)


def _dispatch_kernel(hi_ref, lo_ref, last_ref, ntile_ref, h_ref, out_ref, zero_sc, sem, zsem, *,
                     tile, groups_per_tile):
    i = pl.program_id(0)
    half = groups_per_tile // 2
    n_tiles = out_ref.shape[0] // groups_per_tile
    n_active = last_ref[N_EXPERTS - 1] + 1

    def zero_tile(tile_idx, wait):
        for part in range(2):
            g0 = tile_idx * groups_per_tile + part * half
            copy = pltpu.make_async_copy(zero_sc, out_ref.at[pl.ds(g0, half)], zsem)
            if wait:
                copy.wait()
            else:
                copy.start()

    @pl.when(i == 0)
    def _():
        zero_sc[...] = jnp.zeros_like(zero_sc)
        for wait in (False, True):
            for e in range(N_EXPERTS):
                @pl.when(ntile_ref[e] > 0)
                def _():
                    zero_tile(last_ref[e], wait)

                @pl.when(n_active + e < n_tiles)
                def _():
                    zero_tile(n_active + e, wait)

    base = i * tile * 2

    def row_copy(j, u, k):
        n = base + 2 * (j * SUBLANES + u) + k
        return pltpu.make_async_copy(h_ref.at[j, pl.ds(u, 1)],
                                     out_ref.at[hi_ref[n], pl.ds(lo_ref[n], 1)], sem)

    def start(j, c):
        for u in range(SUBLANES):
            row_copy(j, u, 0).start(priority=0)
            row_copy(j, u, 1).start(priority=1)
        return c

    def wait(j, c):
        for u in range(SUBLANES):
            row_copy(j, u, 0).wait()
            row_copy(j, u, 1).wait()
        return c

    lax.fori_loop(0, tile // SUBLANES, start, 0)
    lax.fori_loop(0, tile // SUBLANES, wait, 0)


def _dispatch(pos_hi, pos_lo, last_tile, tiles_per, h3, n_slots, moe_tile):
    ng, _, d = h3.shape
    groups_per_tile = moe_tile // SUBLANES
    gt = GATHER_TILE // SUBLANES
    return pl.pallas_call(
        functools.partial(_dispatch_kernel, tile=GATHER_TILE, groups_per_tile=groups_per_tile),
        grid_spec=pltpu.PrefetchScalarGridSpec(
            num_scalar_prefetch=4,
            grid=(ng // gt,),
            in_specs=[pl.BlockSpec((gt, SUBLANES, d), lambda i, *_: (i, 0, 0))],
            out_specs=pl.BlockSpec(memory_space=pl.ANY),
            scratch_shapes=[pltpu.VMEM((groups_per_tile // 2, SUBLANES, d), F32),
                            pltpu.SemaphoreType.DMA(()), pltpu.SemaphoreType.DMA(())],
        ),
        out_shape=jax.ShapeDtypeStruct((n_slots // SUBLANES, SUBLANES, d), F32),
        compiler_params=_params("arbitrary"),
        name="moe_dispatch",
    )(pos_hi, pos_lo, last_tile, tiles_per, h3)


def _moe_kernel(te_ref, na_ref, nv_ref, x_ref, wg_ref, wu_ref, wd_ref, o_ref, xb_sc):
    del te_ref
    i = pl.program_id(0)
    f = pl.program_id(1)
    half_rows = x_ref.shape[0] // 2

    @pl.when(f == 0)
    def _():
        o_ref[...] = jnp.zeros_like(o_ref)

    @pl.when(i < na_ref[0])
    def _():
        @pl.when(f == 0)
        def _():
            xb_sc[...] = x_ref[...].astype(BF16)

        def expert_rows(n_rows):
            xb = xb_sc[:n_rows, :]
            g = _dot(xb, wg_ref[0].astype(BF16))
            u = _dot(xb, wu_ref[0].astype(BF16))
            o_ref[:n_rows, :] += _dot(_swiglu_hidden(g, u), wd_ref[0].astype(BF16))

        @pl.when(nv_ref[i] > half_rows)
        def _():
            expert_rows(2 * half_rows)

        @pl.when(nv_ref[i] <= half_rows)
        def _():
            expert_rows(half_rows)


def _moe_experts(tile_expert, n_active, n_valid, xs, w_gate_up, w_down, n_tiles):
    d = w_down.shape[2]
    ff = w_down.shape[1]
    ct = MOE_COL_TILE
    nf = ff // ct
    tm = MOE_ROW_TILE

    def row(i, f, te, na, nv):
        return (jnp.maximum(jnp.minimum(i, na[0] - 1), 0), 0)

    def col(i, f, na):
        return jnp.where(i < na[0], f, nf - 1)

    return pl.pallas_call(
        _moe_kernel,
        grid_spec=pltpu.PrefetchScalarGridSpec(
            num_scalar_prefetch=3,
            grid=(n_tiles, nf),
            in_specs=[
                pl.BlockSpec((tm, d), row),
                pl.BlockSpec((1, d, ct), lambda i, f, te, na, nv: (te[i], 0, col(i, f, na))),
                pl.BlockSpec((1, d, ct), lambda i, f, te, na, nv: (te[i], 0, nf + col(i, f, na))),
                pl.BlockSpec((1, ct, d), lambda i, f, te, na, nv: (te[i], col(i, f, na), 0)),
            ],
            out_specs=pl.BlockSpec((tm, d), lambda i, f, te, na, nv: (i, 0)),
            scratch_shapes=[pltpu.VMEM((tm, d), BF16)],
        ),
        out_shape=jax.ShapeDtypeStruct((n_tiles * tm, d), F32),
        compiler_params=_params("arbitrary", "arbitrary"),
        name="moe_experts",
    )(tile_expert, n_active, n_valid, xs, w_gate_up, w_gate_up, w_down)


def _combine_kernel(hi_ref, lo_ref, x_ref, route_ref, g_ref, y_ref, o_ref, buf, sem, *, tile):
    i = pl.program_id(0)
    slot = i & 1
    groups = tile // SUBLANES

    def row_copy(step, sl, j, u, k):
        n = (step * tile + j * SUBLANES + u) * 2 + k
        return pltpu.make_async_copy(y_ref.at[hi_ref[n], pl.ds(lo_ref[n], 1)],
                                     buf.at[sl, k, j, pl.ds(u, 1)], sem.at[sl])

    def gather(step, sl):
        def start(j, c):
            for u in range(SUBLANES):
                row_copy(step, sl, j, u, 0).start(priority=0)
                row_copy(step, sl, j, u, 1).start(priority=1)
            return c
        lax.fori_loop(0, groups, start, 0)

    @pl.when(i == 0)
    def _():
        gather(0, 0)

    @pl.when(i + 1 < pl.num_programs(0))
    def _():
        gather(i + 1, 1 - slot)

    def wait(j, c):
        for u in range(SUBLANES):
            row_copy(i, slot, j, u, 0).wait()
            row_copy(i, slot, j, u, 1).wait()
        return c

    lax.fori_loop(0, groups, wait, 0)
    route = route_ref[...]
    d = x_ref.shape[1]
    y0 = buf[slot, 0].reshape(tile, d)
    y1 = buf[slot, 1].reshape(tile, d)
    y = x_ref[...] + route[:, 4:5] * y0 + route[:, 5:6] * y1
    o_ref[...] = _rms(y, g_ref[...])


def _combine(pos_hi, pos_lo, x, route, g, y3):
    t, d = x.shape
    groups = GATHER_TILE // SUBLANES
    return pl.pallas_call(
        functools.partial(_combine_kernel, tile=GATHER_TILE),
        grid_spec=pltpu.PrefetchScalarGridSpec(
            num_scalar_prefetch=2,
            grid=(t // GATHER_TILE,),
            in_specs=[
                pl.BlockSpec((GATHER_TILE, d), lambda i, *_: (i, 0)),
                pl.BlockSpec((GATHER_TILE, LANES), lambda i, *_: (i, 0)),
                pl.BlockSpec((1, d), lambda i, *_: (0, 0)),
                pl.BlockSpec(memory_space=pl.ANY),
            ],
            out_specs=pl.BlockSpec((GATHER_TILE, d), lambda i, *_: (i, 0)),
            scratch_shapes=[pltpu.VMEM((2, 2, groups, SUBLANES, d), F32),
                            pltpu.SemaphoreType.DMA((2,))],
        ),
        out_shape=jax.ShapeDtypeStruct((t, d), F32),
        compiler_params=_params("arbitrary"),
        name="moe_combine",
    )(pos_hi, pos_lo, x, route, g, y3)


def _row(v):
    return v.reshape(1, -1).astype(F32)


def _pad_lanes(v):
    return jnp.pad(v.astype(F32), (0, LANES - v.shape[0])).reshape(1, LANES)


def kernel(x, mem, positions, mix_norm_g, ffn_norm_g, mem_norm_g, w_mem_kv, w_out, w_in_a, b_forget, w_q_b, lambda_q1, lambda_k1, lambda_q2, lambda_k2, subln_g, kv_norm_g, w_kv_shared, w_gate_up_dense, w_down_dense, w_router, w_gate_up_moe, w_down_moe, final_norm_g):
    batch, seq, d = x.shape
    mem_tokens = mem.shape[1]
    t = batch * seq
    assert w_in_a.shape[0] == 1 and w_q_b.shape[0] == 1 and w_out.shape[0] == 2
    fox_w = FOX_HEADS * HEAD_DIM
    diff_w = DIFF_HEADS * 2 * HEAD_DIM
    scale = HEAD_DIM ** -0.5

    xf = x.reshape(t, d)
    memf = mem.reshape(batch * mem_tokens, d)

    wa = w_in_a[0]
    w_fl = jnp.pad(wa[:, 3 * fox_w:3 * fox_w + FOX_HEADS], ((0, 0), (0, LANES - FOX_HEADS)))
    w_a = jnp.concatenate(
        [wa[:, :fox_w] * scale, wa[:, fox_w:3 * fox_w], wa[:, 3 * fox_w + FOX_HEADS:] * scale, w_fl],
        axis=1).astype(BF16)
    w_b = (w_q_b[0] * scale).astype(BF16)
    w_kv = w_kv_shared.astype(BF16)
    w_o = w_out.astype(BF16)
    w_mkv = w_mem_kv.astype(BF16)
    w_gu_d = w_gate_up_dense[0].astype(BF16)
    w_dn_d = w_down_dense[0].astype(BF16)
    w_gu_m = w_gate_up_moe[0]
    w_dn_m = w_down_moe[0]
    w_r = jnp.pad(w_router[0].astype(F32), ((0, 0), (0, LANES - N_EXPERTS)))
    w_r_hi = w_r.astype(BF16)
    w_r_lo = (w_r - w_r_hi.astype(F32)).astype(BF16)
    w_r = jnp.concatenate([w_r_hi, w_r_hi, w_r_lo], axis=0)

    qkvm, log_f = _proj_a(xf, _row(mix_norm_g[0]), w_a, _pad_lanes(b_forget[0]))
    cum_t = _cumsum(log_f, batch, seq)
    y_self = _fox_attention(qkvm, cum_t, batch, seq)
    mkv0 = _norm_matmul(memf, _row(mem_norm_g[0]), w_mkv[0], "mem_kv0")
    y_mem = _mem_attention(qkvm, 3 * fox_w // MEM_WIDTH, mkv0, batch, seq, mem_tokens)
    x2 = _dense_ffn(xf, y_self, y_mem, w_o[0], _row(ffn_norm_g[0]), w_gu_d, w_dn_d)

    half = ROT_DIM // 2
    inv_freq = ROPE_THETA ** (-(jnp.arange(half, dtype=F32) * 2.0 / ROT_DIM))
    inv_col = jnp.tile(inv_freq, LANES // half).reshape(LANES, 1)
    pos_rows = positions.astype(jnp.int32).reshape(t // ROW_TILE, 1, ROW_TILE)
    cos_t, sin_t = _rope_tables(pos_rows, inv_col)
    kv, qb = _proj_rope(x2, _row(kv_norm_g), _row(mix_norm_g[1]), w_kv, w_b, cos_t, sin_t, diff_w)

    lam_init = 0.8 - 0.6 * math.exp(-0.3 * 1)
    lam_rows = jnp.concatenate(
        [jnp.pad(v[0].astype(F32), (0, LANES - HEAD_DIM)).reshape(1, LANES)
         for v in (lambda_q1, lambda_k1, lambda_q2, lambda_k2)]
        + [jnp.zeros((4, LANES), F32)], axis=0)
    y_self = _diff_attention(lam_rows, _row(subln_g[0]), qb, kv, batch, seq, lam_init)
    mkv1 = _norm_matmul(memf, _row(mem_norm_g[1]), w_mkv[1], "mem_kv1")
    y_mem = _mem_attention(qb, diff_w // MEM_WIDTH, mkv1, batch, seq, mem_tokens)

    tri = jnp.tril(jnp.ones((MOE_PRE_TILE, MOE_PRE_TILE), BF16), -1)
    x3, h_rows, route, cnt = _moe_pre(x2, y_self, y_mem, w_o[1], _row(ffn_norm_g[1]), w_r, tri)
    tm = MOE_ROW_TILE
    n_tiles = 2 * t // tm + N_EXPERTS
    counts = cnt[0, :N_EXPERTS].astype(jnp.int32)
    tiles_per = (counts + tm - 1) // tm
    tile_end = jnp.cumsum(tiles_per)
    group_off = (tile_end - tiles_per) * tm
    n_active = tile_end[-1:]
    tile_id = jnp.minimum(jnp.arange(n_tiles, dtype=jnp.int32), n_active - 1)
    tile_expert = jnp.sum(tile_id[:, None] >= tile_end[None, :], axis=1).astype(jnp.int32)
    idx = route[:, 0:2].astype(jnp.int32)
    rank = route[:, 2:4].astype(jnp.int32)
    expert_ids = jnp.arange(N_EXPERTS, dtype=jnp.int32)
    off = jnp.sum(jnp.where(idx[..., None] == expert_ids, group_off, 0), axis=-1)
    pos = (off + rank).reshape(2 * t)
    pos_hi, pos_lo = pos // SUBLANES, pos % SUBLANES
    last_tile = jnp.maximum(tile_end - 1, 0).astype(jnp.int32)

    h3 = h_rows.reshape(t // SUBLANES, SUBLANES, d)
    xs3 = _dispatch(pos_hi, pos_lo, last_tile, tiles_per.astype(jnp.int32), h3, n_tiles * tm, tm)
    group_end = group_off + counts
    n_valid = jnp.clip(group_end[tile_expert] - tile_id * tm, 0, tm).astype(jnp.int32)
    y_sorted = _moe_experts(tile_expert, n_active.astype(jnp.int32), n_valid,
                            xs3.reshape(n_tiles * tm, d), w_gu_m, w_dn_m, n_tiles)
    y3 = y_sorted.reshape(n_tiles * tm // SUBLANES, SUBLANES, d)
    out = _combine(pos_hi, pos_lo, x3, route, _row(final_norm_g), y3)
    return out.reshape(batch, seq, d)
```

```python
import functools
import math

import jax
import jax.numpy as jnp
from jax import lax
from jax.experimental import pallas as pl
from jax.experimental.pallas import tpu as pltpu

F32 = jnp.float32
BF16 = jnp.bfloat16

HEAD_DIM = 64
LANES = 128
SUBLANES = 8
CHUNK_SHIFT = 6
FOX_HEADS = 12
DIFF_HEADS = 6
MEM_HEADS = 4
MEM_WIDTH = MEM_HEADS * HEAD_DIM
ROPE_THETA = 500000.0
ROT_DIM = HEAD_DIM // 4
N_EXPERTS = 8
EPS = 1e-5
NEG = -1e30
VMEM_LIMIT = 48 * 1024 * 1024
FFN_VMEM_LIMIT = 56 * 1024 * 1024

ROW_TILE = 1024
FFN_ROW_TILE = 1024
FFN_COL_TILE = 256
ATT_TILE = 256
ROW_CHUNK = 16
ATT_SLOTS = 2
MEM_Q_TILE = 512
MOE_PRE_TILE = 1024
MOE_ROW_TILE = 1024
MOE_COL_TILE = 512
GATHER_TILE = 256


def _params(*sem):
    return pltpu.CompilerParams(dimension_semantics=sem, vmem_limit_bytes=VMEM_LIMIT)


def _rms(x, g):
    ms = jnp.mean(x * x, axis=-1, keepdims=True)
    return x * lax.rsqrt(ms + EPS) * g


def _nt_dot(a, b):
    return lax.dot_general(a, b, (((1,), (1,)), ((), ())), preferred_element_type=F32)


def _dot(a, b):
    return jnp.dot(a, b, preferred_element_type=F32)


def _proj_a_kernel(x_ref, g_ref, w_ref, b_ref, o_ref, lf_ref, *, n_main, col_chunk):
    h = _rms(x_ref[...], g_ref[...]).astype(BF16)
    for c in range(n_main // col_chunk):
        cs = slice(c * col_chunk, (c + 1) * col_chunk)
        o_ref[:, cs] = _dot(h, w_ref[:, cs]).astype(BF16)
    z = _dot(h, w_ref[:, n_main:]) + b_ref[...]
    lf_ref[...] = jnp.minimum(z, 0.0) - jnp.log1p(jnp.exp(-jnp.abs(z)))


def _proj_a(x, g, w, b_pad):
    t, d = x.shape
    n_all = w.shape[1]
    n_main = n_all - LANES
    return pl.pallas_call(
        functools.partial(_proj_a_kernel, n_main=n_main, col_chunk=512),
        grid=(t // ROW_TILE,),
        in_specs=[
            pl.BlockSpec((ROW_TILE, d), lambda i: (i, 0)),
            pl.BlockSpec((1, d), lambda i: (0, 0)),
            pl.BlockSpec((d, n_all), lambda i: (0, 0)),
            pl.BlockSpec((1, LANES), lambda i: (0, 0)),
        ],
        out_specs=[
            pl.BlockSpec((ROW_TILE, n_main), lambda i: (i, 0)),
            pl.BlockSpec((ROW_TILE, LANES), lambda i: (i, 0)),
        ],
        out_shape=[
            jax.ShapeDtypeStruct((t, n_main), BF16),
            jax.ShapeDtypeStruct((t, LANES), F32),
        ],
        compiler_params=_params("parallel"),
        name="proj_a",
    )(x, g, w, b_pad)


def _cumsum_kernel(lf_ref, ct_ref):
    x = lf_ref[...]
    s = x.shape[0]
    row = lax.broadcasted_iota(jnp.int32, x.shape, 0)
    sh = 1
    while sh < s:
        x = x + jnp.where(row >= sh, pltpu.roll(x, sh, 0), 0.0)
        sh *= 2
    ct_ref[0] = x.T


def _cumsum(lf, batch, seq):
    return pl.pallas_call(
        _cumsum_kernel,
        grid=(batch,),
        in_specs=[pl.BlockSpec((seq, LANES), lambda b: (b, 0))],
        out_specs=pl.BlockSpec((1, LANES, seq), lambda b: (b, 0, 0)),
        out_shape=jax.ShapeDtypeStruct((batch, LANES, seq), F32),
        compiler_params=_params("parallel"),
        name="cumsum",
    )(lf)


def _softmax_rows(s_ref, p_ref, m_ref, hi, tq, col_bias, keep_fn):
    lo = hi - tq
    shape = (ROW_CHUNK, LANES)

    def visibility(r, c):
        r0, c0 = r * ROW_CHUNK, c * LANES - lo
        if c0 < 0 or keep_fn(r0, c0 + LANES - 1):
            return "all"
        if not keep_fn(r0 + ROW_CHUNK - 1, c0):
            return "none"
        ri = lax.broadcasted_iota(jnp.int32, shape, 0) + r0
        ci = lax.broadcasted_iota(jnp.int32, shape, 1) + c0
        return keep_fn(ri, ci)

    def load(r, c, vis):
        t = s_ref[r * ROW_CHUNK:(r + 1) * ROW_CHUNK, c * LANES:(c + 1) * LANES]
        if col_bias is not None:
            t = t - col_bias[:, c * LANES:(c + 1) * LANES]
        return t if isinstance(vis, str) else jnp.where(vis, t, NEG)

    for r in range(tq // ROW_CHUNK):
        m_acc = None
        for c in range(hi // LANES):
            vis = visibility(r, c)
            if isinstance(vis, str) and vis == "none":
                continue
            t = load(r, c, vis)
            m_acc = t if m_acc is None else jnp.maximum(m_acc, t)
        m_ref[r * ROW_CHUNK:(r + 1) * ROW_CHUNK, :] = jnp.broadcast_to(
            jnp.max(m_acc, axis=1, keepdims=True), shape)
    for r in range(tq // ROW_CHUNK):
        rows = slice(r * ROW_CHUNK, (r + 1) * ROW_CHUNK)
        m = m_ref[rows, :]
        for c in range(hi // LANES):
            cols = slice(c * LANES, (c + 1) * LANES)
            vis = visibility(r, c)
            if isinstance(vis, str) and vis == "none":
                p_ref[rows, cols] = jnp.zeros(shape, BF16)
            else:
                p_ref[rows, cols] = jnp.exp((load(r, c, vis) - m).astype(BF16))


def _attention_sweep(q_ref, k_ref, v_aug, scratch, tq, col_bias_fn, keep_fn, emit):
    seq = q_ref.shape[0]
    n_items = 2 * (seq // tq)
    n_slots = scratch[0].shape[0]
    qh, res = {}, {}

    def bufs(n):
        return [sc.at[n % n_slots] for sc in scratch]

    def score(n):
        qi, hh = divmod(n, 2)
        hi = (qi + 1) * tq
        if hh == 0:
            qh[qi] = _split_heads(q_ref[qi * tq:hi, :])
        bufs(n)[0][:, :hi] = _nt_dot(qh[qi][hh], k_ref[:hi, :])

    def softmax(n):
        qi, hh = divmod(n, 2)
        hi = (qi + 1) * tq
        bias = None if col_bias_fn is None else col_bias_fn(hh, hi)
        _softmax_rows(*bufs(n), hi, tq, bias, keep_fn)

    def values(n):
        qi, hh = divmod(n, 2)
        hi = (qi + 1) * tq
        res[hh] = _dot(bufs(n)[1][:, :hi], v_aug(hh)[:hi, :])
        if hh == 1:
            emit(qi, res[0], res[1])

    score(0)
    for n in range(n_items):
        if n + 1 < n_items:
            score(n + 1)
        softmax(n)
        if n:
            values(n - 1)
    values(n_items - 1)


def _split_heads(q):
    low = lax.broadcasted_iota(jnp.int32, (1, LANES), 1) < HEAD_DIM
    zero = jnp.zeros_like(q)
    return jnp.where(low, q, zero), jnp.where(low, zero, q)


def _fox_kernel(q_ref, k_ref, v_ref, ct_ref, o_ref, s_sc, p_sc, m_sc, va_sc, *, tq):
    sub = (2 * pl.program_id(1)) & 7
    low = lax.broadcasted_iota(jnp.int32, (1, LANES), 1) < HEAD_DIM
    v = v_ref[...]
    one = jnp.ones_like(v)
    va_sc[0] = jnp.where(low, v, one)
    va_sc[1] = jnp.where(low, one, v)

    def keep_fn(row, col):
        return col <= row

    def col_bias(hh, hi):
        return ct_ref[0, pl.ds(sub + hh, 1), :hi]

    def emit(qi, r0, r1):
        num = jnp.where(low, r0, r1)
        den = pltpu.roll(jnp.where(low, r1, r0), HEAD_DIM, 1)
        o_ref[qi * tq:(qi + 1) * tq, :] = (num / den).astype(BF16)

    _attention_sweep(q_ref, k_ref, lambda hh: va_sc.at[hh], (s_sc, p_sc, m_sc), tq, col_bias,
                     keep_fn, emit)


def _attention_scratch(tq, seq):
    n = 2 * ATT_SLOTS
    return [pltpu.VMEM((n, tq, seq), F32), pltpu.VMEM((n, tq, seq), BF16),
            pltpu.VMEM((n, tq, LANES), F32)]


def _fox_attention(qkvm, cum_t, batch, seq):
    t = batch * seq
    npair = FOX_HEADS // 2
    return pl.pallas_call(
        functools.partial(_fox_kernel, tq=ATT_TILE),
        grid=(batch, npair),
        in_specs=[
            pl.BlockSpec((seq, LANES), lambda b, p: (b, p)),
            pl.BlockSpec((seq, LANES), lambda b, p: (b, npair + p)),
            pl.BlockSpec((seq, LANES), lambda b, p: (b, 2 * npair + p)),
            pl.BlockSpec((1, 8, seq), lambda b, p: (b, p // 4, 0)),
        ],
        out_specs=pl.BlockSpec((seq, LANES), lambda b, p: (b, p)),
        out_shape=jax.ShapeDtypeStruct((t, npair * LANES), BF16),
        scratch_shapes=_attention_scratch(ATT_TILE, seq) + [pltpu.VMEM((2, seq, LANES), BF16)],
        compiler_params=_params("parallel", "parallel"),
        name="fox_attention",
    )(qkvm, qkvm, qkvm, cum_t)


def _mem_kernel(q_ref, mk_ref, mv_ref, o_ref):
    q = q_ref[...]
    mk = mk_ref[...]
    mv = mv_ref[...]
    lane = lax.broadcasted_iota(jnp.int32, (1, MEM_WIDTH), 1)
    zero = jnp.zeros_like(q)
    out = jnp.zeros(q.shape, F32)
    for h in range(MEM_HEADS):
        hm = (lane >= h * HEAD_DIM) & (lane < (h + 1) * HEAD_DIM)
        s = _nt_dot(jnp.where(hm, q, zero), mk)
        p = jnp.exp(s - jnp.max(s, axis=1, keepdims=True))
        l = jnp.sum(p, axis=1, keepdims=True)
        out = jnp.where(hm, _dot(p.astype(BF16), mv) / l, out)
    o_ref[...] = out.astype(BF16)


def _mem_attention(q_arr, q_col_block, mkv, batch, seq, mem_tokens):
    nq = seq // MEM_Q_TILE
    return pl.pallas_call(
        _mem_kernel,
        grid=(batch, nq),
        in_specs=[
            pl.BlockSpec((MEM_Q_TILE, MEM_WIDTH), lambda b, i: (b * nq + i, q_col_block)),
            pl.BlockSpec((mem_tokens, MEM_WIDTH), lambda b, i: (b, 0)),
            pl.BlockSpec((mem_tokens, MEM_WIDTH), lambda b, i: (b, 1)),
        ],
        out_specs=pl.BlockSpec((MEM_Q_TILE, MEM_WIDTH), lambda b, i: (b * nq + i, 0)),
        out_shape=jax.ShapeDtypeStruct((batch * seq, MEM_WIDTH), BF16),
        compiler_params=_params("parallel", "parallel"),
        name="mem_attention",
    )(q_arr, mkv, mkv)


def _norm_matmul_kernel(x_ref, g_ref, w_ref, o_ref):
    h = _rms(x_ref[...], g_ref[...]).astype(BF16)
    o_ref[...] = _dot(h, w_ref[...]).astype(o_ref.dtype)


def _norm_matmul(x, g, w, name):
    t, d = x.shape
    n = w.shape[1]
    return pl.pallas_call(
        _norm_matmul_kernel,
        grid=(t // ROW_TILE,),
        in_specs=[
            pl.BlockSpec((ROW_TILE, d), lambda i: (i, 0)),
            pl.BlockSpec((1, d), lambda i: (0, 0)),
            pl.BlockSpec((d, n), lambda i: (0, 0)),
        ],
        out_specs=pl.BlockSpec((ROW_TILE, n), lambda i: (i, 0)),
        out_shape=jax.ShapeDtypeStruct((t, n), BF16),
        compiler_params=_params("parallel"),
        name=name,
    )(x, g, w)


def _mixed_residual(x_ref, ys_ref, ym_ref, wo_ref):
    sw = ys_ref.shape[1]
    return x_ref[...] + _dot(ys_ref[...], wo_ref[:sw, :]) + _dot(ym_ref[...], wo_ref[sw:, :])


def _swiglu_hidden(g, u):
    return (g * jax.nn.sigmoid(g) * u).astype(BF16)


def _ffn_kernel(x_ref, ys_ref, ym_ref, wo_ref, g_ref, wgu_ref, wd_ref, o_ref, h_sc, *, col):
    ff = wd_ref.shape[0]
    x = _mixed_residual(x_ref, ys_ref, ym_ref, wo_ref)
    h_sc[...] = _rms(x, g_ref[...]).astype(BF16)
    o_ref[...] = x
    for c in range(ff // col):
        h = h_sc[...]
        gate = _dot(h, wgu_ref[:, c * col:(c + 1) * col])
        up = _dot(h, wgu_ref[:, ff + c * col:ff + (c + 1) * col])
        o_ref[...] += _dot(_swiglu_hidden(gate, up), wd_ref[c * col:(c + 1) * col, :])


def _dense_ffn(x, y_self, y_mem, w_out, g, w_gate_up, w_down):
    t, d = x.shape
    tm = FFN_ROW_TILE
    row = lambda i: (i, 0)
    resident = dict(index_map=lambda i: (0, 0), pipeline_mode=pl.Buffered(1))
    return pl.pallas_call(
        functools.partial(_ffn_kernel, col=FFN_COL_TILE),
        grid=(t // tm,),
        in_specs=[
            pl.BlockSpec((tm, d), row),
            pl.BlockSpec((tm, y_self.shape[1]), row),
            pl.BlockSpec((tm, y_mem.shape[1]), row),
            pl.BlockSpec(w_out.shape, **resident),
            pl.BlockSpec((1, d), lambda i: (0, 0)),
            pl.BlockSpec(w_gate_up.shape, **resident),
            pl.BlockSpec(w_down.shape, **resident),
        ],
        out_specs=pl.BlockSpec((tm, d), row),
        out_shape=jax.ShapeDtypeStruct((t, d), F32),
        scratch_shapes=[pltpu.VMEM((tm, d), BF16)],
        compiler_params=pltpu.CompilerParams(dimension_semantics=("parallel",),
                                             vmem_limit_bytes=FFN_VMEM_LIMIT),
        name="dense_ffn",
    )(x, y_self, y_mem, w_out, g, w_gate_up, w_down)


def _rope_kernel(pos_ref, inv_ref, cos_ref, sin_ref):
    pos = pos_ref[0].astype(F32)
    ang = (inv_ref[...] * pos).T
    lane = lax.broadcasted_iota(jnp.int32, ang.shape, 1)
    rot = (lane & (HEAD_DIM - 1)) < ROT_DIM
    first = (lane & (ROT_DIM - 1)) < (ROT_DIM // 2)
    sn = jnp.sin(ang)
    cos_ref[...] = jnp.where(rot, jnp.cos(ang), 1.0)
    sin_ref[...] = jnp.where(rot, jnp.where(first, -sn, sn), 0.0)


def _rope_tables(pos_rows, inv_col):
    steps = pos_rows.shape[0]
    t = steps * ROW_TILE
    return pl.pallas_call(
        _rope_kernel,
        grid=(steps,),
        in_specs=[
            pl.BlockSpec((1, 1, ROW_TILE), lambda i: (i, 0, 0)),
            pl.BlockSpec((LANES, 1), lambda i: (0, 0)),
        ],
        out_specs=[pl.BlockSpec((ROW_TILE, LANES), lambda i: (i, 0))] * 2,
        out_shape=[jax.ShapeDtypeStruct((t, LANES), F32)] * 2,
        compiler_params=_params("parallel"),
        name="rope_tables",
    )(pos_rows, inv_col)


def _proj_rope_kernel(x_ref, gk_ref, gq_ref, wk_ref, wq_ref, cos_ref, sin_ref, ok_ref, oq_ref, *,
                      n_rope, col_chunk):
    x = x_ref[...]
    xn = x * lax.rsqrt(jnp.mean(x * x, axis=-1, keepdims=True) + EPS)
    cos = cos_ref[...]
    sin = sin_ref[...]
    lane = lax.broadcasted_iota(jnp.int32, (1, LANES), 1)
    first = (lane & (ROT_DIM - 1)) < (ROT_DIM // 2)
    half = ROT_DIM // 2
    for g_ref, w_ref, o_ref in ((gk_ref, wk_ref, ok_ref), (gq_ref, wq_ref, oq_ref)):
        h = (xn * g_ref[...]).astype(BF16)
        for c in range(w_ref.shape[1] // col_chunk):
            a = _dot(h, w_ref[:, c * col_chunk:(c + 1) * col_chunk])
            for s in range(col_chunk // LANES):
                col = c * col_chunk + s * LANES
                blk = a[:, s * LANES:(s + 1) * LANES]
                if col < n_rope:
                    partner = jnp.where(first, pltpu.roll(blk, LANES - half, 1),
                                        pltpu.roll(blk, half, 1))
                    blk = blk * cos + partner * sin
                o_ref[:, col:col + LANES] = blk.astype(BF16)


def _proj_rope(x, g_kv, g_q, w_kv, w_q, cos, sin, n_rope):
    t, d = x.shape
    nk, nq = w_kv.shape[1], w_q.shape[1]
    row = lambda i: (i, 0)
    fixed = lambda i: (0, 0)
    return pl.pallas_call(
        functools.partial(_proj_rope_kernel, n_rope=n_rope, col_chunk=256),
        grid=(t // ROW_TILE,),
        in_specs=[
            pl.BlockSpec((ROW_TILE, d), row),
            pl.BlockSpec((1, d), fixed),
            pl.BlockSpec((1, d), fixed),
            pl.BlockSpec((d, nk), fixed),
            pl.BlockSpec((d, nq), fixed),
            pl.BlockSpec((ROW_TILE, LANES), row),
            pl.BlockSpec((ROW_TILE, LANES), row),
        ],
        out_specs=[pl.BlockSpec((ROW_TILE, nk), row), pl.BlockSpec((ROW_TILE, nq), row)],
        out_shape=[jax.ShapeDtypeStruct((t, nk), BF16), jax.ShapeDtypeStruct((t, nq), BF16)],
        compiler_params=_params("parallel"),
        name="proj_kv_q",
    )(x, g_kv, g_q, w_kv, w_q, cos, sin)


def _diff_kernel(lam_ref, sg_ref, q_ref, k_ref, v_ref, o_ref, s_sc, p_sc, m_sc, va_sc, *, tq,
                 lam_init):
    lp = lam_ref[...]
    lam = (jnp.exp(jnp.sum(lp[0:1] * lp[1:2], axis=1, keepdims=True))
           - jnp.exp(jnp.sum(lp[2:3] * lp[3:4], axis=1, keepdims=True)) + lam_init)
    va_sc[:, :LANES] = v_ref[...]
    va_sc[:, LANES:] = jnp.ones(v_ref.shape, BF16)

    def keep_fn(row, col):
        return (col >> CHUNK_SHIFT) <= (row >> CHUNK_SHIFT)

    def emit(qi, r1, r2):
        o = r1[:, :LANES] / r1[:, LANES:] - lam * (r2[:, :LANES] / r2[:, LANES:])
        o = _rms(o, sg_ref[...]) * (1.0 - lam_init)
        o_ref[qi * tq:(qi + 1) * tq, :] = o.astype(BF16)

    _attention_sweep(q_ref, k_ref, lambda hh: va_sc, (s_sc, p_sc, m_sc), tq, None, keep_fn, emit)


def _diff_attention(lam_rows, subln_g, qb, kv, batch, seq, lam_init):
    t = batch * seq
    return pl.pallas_call(
        functools.partial(_diff_kernel, tq=ATT_TILE, lam_init=lam_init),
        grid=(batch, DIFF_HEADS),
        in_specs=[
            pl.BlockSpec((8, LANES), lambda b, h: (0, 0)),
            pl.BlockSpec((1, LANES), lambda b, h: (0, 0)),
            pl.BlockSpec((seq, LANES), lambda b, h: (b, h)),
            pl.BlockSpec((seq, LANES), lambda b, h: (b, h)),
            pl.BlockSpec((seq, LANES), lambda b, h: (b, DIFF_HEADS + h)),
        ],
        out_specs=pl.BlockSpec((seq, LANES), lambda b, h: (b, h)),
        out_shape=jax.ShapeDtypeStruct((t, DIFF_HEADS * LANES), BF16),
        scratch_shapes=_attention_scratch(ATT_TILE, seq) + [pltpu.VMEM((seq, 2 * LANES), BF16)],
        compiler_params=_params("parallel", "parallel"),
        name="diff_attention",
    )(lam_rows, subln_g, qb, kv, kv)


def _moe_pre_kernel(x_ref, ys_ref, ym_ref, wo_ref, g_ref, wr_ref, tri_ref, x3_ref, hp_ref, route_ref,
                    route_t_ref, cnt_ref, run_sc):
    @pl.when(pl.program_id(0) == 0)
    def _():
        run_sc[...] = jnp.zeros_like(run_sc)

    x = _mixed_residual(x_ref, ys_ref, ym_ref, wo_ref)
    x3_ref[...] = x
    hf = _rms(x, g_ref[...])
    hp_ref[...] = hf

    h_hi = hf.astype(BF16)
    h_lo = (hf - h_hi.astype(F32)).astype(BF16)
    logits = _dot(jnp.concatenate([h_hi, h_lo, h_hi], axis=1), wr_ref[...])
    lane = lax.broadcasted_iota(jnp.int32, logits.shape, 1)
    lanef = lane.astype(F32)
    lg = jnp.where(lane < N_EXPERTS, logits, NEG)
    v1 = jnp.max(lg, axis=1, keepdims=True)
    i1 = jnp.min(jnp.where(lg == v1, lanef, float(LANES)), axis=1, keepdims=True)
    lg2 = jnp.where(lanef == i1, NEG, lg)
    v2 = jnp.max(lg2, axis=1, keepdims=True)
    i2 = jnp.min(jnp.where(lg2 == v2, lanef, float(LANES)), axis=1, keepdims=True)
    e = jnp.exp(v2 - v1)
    g1 = 1.0 / (1.0 + e)
    g2 = e / (1.0 + e)

    oh1 = lanef == i1
    oh2 = lanef == i2
    oh = jnp.where(oh1 | oh2, 1.0, 0.0)
    before = _dot(tri_ref[...], oh.astype(BF16)) + run_sc[...]
    r1 = jnp.sum(jnp.where(oh1, before, 0.0), axis=1, keepdims=True)
    r2 = jnp.sum(jnp.where(oh2, before, 0.0), axis=1, keepdims=True)
    run_sc[...] += jnp.sum(oh, axis=0, keepdims=True)
    cnt_ref[...] = run_sc[...]

    route = jnp.where(lane == 0, i1, 0.0)
    for ln, val in ((1, i2), (2, r1), (3, r2), (4, g1), (5, g2)):
        route = jnp.where(lane == ln, val, route)
    route_ref[...] = route
    route_t_ref[...] = route.T[:SUBLANES, :]


def _moe_pre(x, y_self, y_mem, w_out, g, w_router_pad, tri):
    t, d = x.shape
    tm = tri.shape[0]
    row = lambda i: (i, 0)
    fixed = lambda i: (0, 0)
    return pl.pallas_call(
        _moe_pre_kernel,
        grid=(t // tm,),
        in_specs=[
            pl.BlockSpec((tm, d), row),
            pl.BlockSpec((tm, y_self.shape[1]), row),
            pl.BlockSpec((tm, y_mem.shape[1]), row),
            pl.BlockSpec(w_out.shape, fixed),
            pl.BlockSpec((1, d), fixed),
            pl.BlockSpec((3 * d, LANES), fixed),
            pl.BlockSpec((tm, tm), fixed),
        ],
        out_specs=[
            pl.BlockSpec((tm, d), row),
            pl.BlockSpec((tm, d), row),
            pl.BlockSpec((tm, LANES), row),
            pl.BlockSpec((SUBLANES, tm), lambda i: (0, i)),
            pl.BlockSpec((1, LANES), fixed),
        ],
        out_shape=[
            jax.ShapeDtypeStruct((t, d), F32),
            jax.ShapeDtypeStruct((t, d), F32),
            jax.ShapeDtypeStruct((t, LANES), F32),
            jax.ShapeDtypeStruct((SUBLANES, t), F32),
            jax.ShapeDtypeStruct((1, LANES), F32),
        ],
        scratch_shapes=[pltpu.VMEM((1, LANES), F32)],
        compiler_params=_params("arbitrary"),
        name="moe_pre",
    )(x, y_self, y_mem, w_out, g, w_router_pad, tri)


def _dispatch_kernel(hi_ref, lo_ref, last_ref, ntile_ref, h_ref, out_ref, zero_sc, sem, zsem, *,
                     tile, groups_per_tile):
    i = pl.program_id(0)
    half = groups_per_tile // 2
    n_tiles = out_ref.shape[0] // groups_per_tile
    n_active = last_ref[N_EXPERTS - 1] + 1

    def zero_tile(tile_idx, wait):
        for part in range(2):
            g0 = tile_idx * groups_per_tile + part * half
            copy = pltpu.make_async_copy(zero_sc, out_ref.at[pl.ds(g0, half)], zsem)
            if wait:
                copy.wait()
            else:
                copy.start()

    @pl.when(i == 0)
    def _():
        zero_sc[...] = jnp.zeros_like(zero_sc)
        for wait in (False, True):
            for e in range(N_EXPERTS):
                @pl.when(ntile_ref[e] > 0)
                def _():
                    zero_tile(last_ref[e], wait)

                @pl.when(n_active + e < n_tiles)
                def _():
                    zero_tile(n_active + e, wait)

    n_tokens = hi_ref.shape[0] // 2

    def row_copy(j, u, k):
        n = k * n_tokens + i * tile + j * SUBLANES + u
        return pltpu.make_async_copy(h_ref.at[j, pl.ds(u, 1)],
                                     out_ref.at[hi_ref[n], pl.ds(lo_ref[n], 1)], sem)

    def start(j, c):
        for u in range(SUBLANES):
            row_copy(j, u, 0).start(priority=0)
            row_copy(j, u, 1).start(priority=1)
        return c

    def wait(j, c):
        for u in range(SUBLANES):
            row_copy(j, u, 0).wait()
            row_copy(j, u, 1).wait()
        return c

    lax.fori_loop(0, tile // SUBLANES, start, 0)
    lax.fori_loop(0, tile // SUBLANES, wait, 0)


def _dispatch(pos_hi, pos_lo, last_tile, tiles_per, h3, n_slots, moe_tile):
    ng, _, d = h3.shape
    groups_per_tile = moe_tile // SUBLANES
    gt = GATHER_TILE // SUBLANES
    return pl.pallas_call(
        functools.partial(_dispatch_kernel, tile=GATHER_TILE, groups_per_tile=groups_per_tile),
        grid_spec=pltpu.PrefetchScalarGridSpec(
            num_scalar_prefetch=4,
            grid=(ng // gt,),
            in_specs=[pl.BlockSpec((gt, SUBLANES, d), lambda i, *_: (i, 0, 0))],
            out_specs=pl.BlockSpec(memory_space=pl.ANY),
            scratch_shapes=[pltpu.VMEM((groups_per_tile // 2, SUBLANES, d), F32),
                            pltpu.SemaphoreType.DMA(()), pltpu.SemaphoreType.DMA(())],
        ),
        out_shape=jax.ShapeDtypeStruct((n_slots // SUBLANES, SUBLANES, d), F32),
        compiler_params=_params("arbitrary"),
        name="moe_dispatch",
    )(pos_hi, pos_lo, last_tile, tiles_per, h3)


def _moe_kernel(te_ref, na_ref, nv_ref, x_ref, wg_ref, wu_ref, wd_ref, o_ref, xb_sc):
    del te_ref
    i = pl.program_id(0)
    f = pl.program_id(1)
    half_rows = x_ref.shape[0] // 2

    @pl.when(f == 0)
    def _():
        o_ref[...] = jnp.zeros_like(o_ref)

    @pl.when(i < na_ref[0])
    def _():
        @pl.when(f == 0)
        def _():
            xb_sc[...] = x_ref[...].astype(BF16)

        def expert_rows(n_rows):
            xb = xb_sc[:n_rows, :]
            g = _dot(xb, wg_ref[0].astype(BF16))
            u = _dot(xb, wu_ref[0].astype(BF16))
            o_ref[:n_rows, :] += _dot(_swiglu_hidden(g, u), wd_ref[0].astype(BF16))

        @pl.when(nv_ref[i] > half_rows)
        def _():
            expert_rows(2 * half_rows)

        @pl.when(nv_ref[i] <= half_rows)
        def _():
            expert_rows(half_rows)


def _moe_experts(tile_expert, n_active, n_valid, xs, w_gate_up, w_down, n_tiles):
    d = w_down.shape[2]
    ff = w_down.shape[1]
    ct = MOE_COL_TILE
    nf = ff // ct
    tm = MOE_ROW_TILE

    def row(i, f, te, na, nv):
        return (jnp.maximum(jnp.minimum(i, na[0] - 1), 0), 0)

    def col(i, f, na):
        return jnp.where(i < na[0], f, nf - 1)

    return pl.pallas_call(
        _moe_kernel,
        grid_spec=pltpu.PrefetchScalarGridSpec(
            num_scalar_prefetch=3,
            grid=(n_tiles, nf),
            in_specs=[
                pl.BlockSpec((tm, d), row),
                pl.BlockSpec((1, d, ct), lambda i, f, te, na, nv: (te[i], 0, col(i, f, na))),
                pl.BlockSpec((1, d, ct), lambda i, f, te, na, nv: (te[i], 0, nf + col(i, f, na))),
                pl.BlockSpec((1, ct, d), lambda i, f, te, na, nv: (te[i], col(i, f, na), 0)),
            ],
            out_specs=pl.BlockSpec((tm, d), lambda i, f, te, na, nv: (i, 0)),
            scratch_shapes=[pltpu.VMEM((tm, d), BF16)],
        ),
        out_shape=jax.ShapeDtypeStruct((n_tiles * tm, d), F32),
        compiler_params=_params("arbitrary", "arbitrary"),
        name="moe_experts",
    )(tile_expert, n_active, n_valid, xs, w_gate_up, w_gate_up, w_down)


def _combine_kernel(hi_ref, lo_ref, x_ref, route_ref, g_ref, y_ref, o_ref, buf, sem, *, tile):
    i = pl.program_id(0)
    slot = i & 1
    groups = tile // SUBLANES

    n_tokens = hi_ref.shape[0] // 2

    def row_copy(step, sl, j, u, k):
        n = k * n_tokens + step * tile + j * SUBLANES + u
        return pltpu.make_async_copy(y_ref.at[hi_ref[n], pl.ds(lo_ref[n], 1)],
                                     buf.at[sl, k, j, pl.ds(u, 1)], sem.at[sl])

    def gather(step, sl):
        def start(j, c):
            for u in range(SUBLANES):
                row_copy(step, sl, j, u, 0).start(priority=0)
                row_copy(step, sl, j, u, 1).start(priority=1)
            return c
        lax.fori_loop(0, groups, start, 0)

    @pl.when(i == 0)
    def _():
        gather(0, 0)

    @pl.when(i + 1 < pl.num_programs(0))
    def _():
        gather(i + 1, 1 - slot)

    def wait(j, c):
        for u in range(SUBLANES):
            row_copy(i, slot, j, u, 0).wait()
            row_copy(i, slot, j, u, 1).wait()
        return c

    lax.fori_loop(0, groups, wait, 0)
    route = route_ref[...]
    d = x_ref.shape[1]
    y0 = buf[slot, 0].reshape(tile, d)
    y1 = buf[slot, 1].reshape(tile, d)
    y = x_ref[...] + route[:, 4:5] * y0 + route[:, 5:6] * y1
    o_ref[...] = _rms(y, g_ref[...])


def _combine(pos_hi, pos_lo, x, route, g, y3):
    t, d = x.shape
    groups = GATHER_TILE // SUBLANES
    return pl.pallas_call(
        functools.partial(_combine_kernel, tile=GATHER_TILE),
        grid_spec=pltpu.PrefetchScalarGridSpec(
            num_scalar_prefetch=2,
            grid=(t // GATHER_TILE,),
            in_specs=[
                pl.BlockSpec((GATHER_TILE, d), lambda i, *_: (i, 0)),
                pl.BlockSpec((GATHER_TILE, LANES), lambda i, *_: (i, 0)),
                pl.BlockSpec((1, d), lambda i, *_: (0, 0)),
                pl.BlockSpec(memory_space=pl.ANY),
            ],
            out_specs=pl.BlockSpec((GATHER_TILE, d), lambda i, *_: (i, 0)),
            scratch_shapes=[pltpu.VMEM((2, 2, groups, SUBLANES, d), F32),
                            pltpu.SemaphoreType.DMA((2,))],
        ),
        out_shape=jax.ShapeDtypeStruct((t, d), F32),
        compiler_params=_params("arbitrary"),
        name="moe_combine",
    )(pos_hi, pos_lo, x, route, g, y3)


def _row(v):
    return v.reshape(1, -1).astype(F32)


def _pad_lanes(v):
    return jnp.pad(v.astype(F32), (0, LANES - v.shape[0])).reshape(1, LANES)


def kernel(x, mem, positions, mix_norm_g, ffn_norm_g, mem_norm_g, w_mem_kv, w_out, w_in_a, b_forget, w_q_b, lambda_q1, lambda_k1, lambda_q2, lambda_k2, subln_g, kv_norm_g, w_kv_shared, w_gate_up_dense, w_down_dense, w_router, w_gate_up_moe, w_down_moe, final_norm_g):
    batch, seq, d = x.shape
    mem_tokens = mem.shape[1]
    t = batch * seq
    assert w_in_a.shape[0] == 1 and w_q_b.shape[0] == 1 and w_out.shape[0] == 2
    fox_w = FOX_HEADS * HEAD_DIM
    diff_w = DIFF_HEADS * 2 * HEAD_DIM
    scale = HEAD_DIM ** -0.5

    xf = x.reshape(t, d)
    memf = mem.reshape(batch * mem_tokens, d)

    wa = w_in_a[0]
    w_fl = jnp.pad(wa[:, 3 * fox_w:3 * fox_w + FOX_HEADS], ((0, 0), (0, LANES - FOX_HEADS)))
    w_a = jnp.concatenate(
        [wa[:, :fox_w] * scale, wa[:, fox_w:3 * fox_w], wa[:, 3 * fox_w + FOX_HEADS:] * scale, w_fl],
        axis=1).astype(BF16)
    w_b = (w_q_b[0] * scale).astype(BF16)
    w_kv = w_kv_shared.astype(BF16)
    w_o = w_out.astype(BF16)
    w_mkv = w_mem_kv.astype(BF16)
    w_gu_d = w_gate_up_dense[0].astype(BF16)
    w_dn_d = w_down_dense[0].astype(BF16)
    w_gu_m = w_gate_up_moe[0]
    w_dn_m = w_down_moe[0]
    w_r = jnp.pad(w_router[0].astype(F32), ((0, 0), (0, LANES - N_EXPERTS)))
    w_r_hi = w_r.astype(BF16)
    w_r_lo = (w_r - w_r_hi.astype(F32)).astype(BF16)
    w_r = jnp.concatenate([w_r_hi, w_r_hi, w_r_lo], axis=0)

    qkvm, log_f = _proj_a(xf, _row(mix_norm_g[0]), w_a, _pad_lanes(b_forget[0]))
    cum_t = _cumsum(log_f, batch, seq)
    y_self = _fox_attention(qkvm, cum_t, batch, seq)
    mkv0 = _norm_matmul(memf, _row(mem_norm_g[0]), w_mkv[0], "mem_kv0")
    y_mem = _mem_attention(qkvm, 3 * fox_w // MEM_WIDTH, mkv0, batch, seq, mem_tokens)
    x2 = _dense_ffn(xf, y_self, y_mem, w_o[0], _row(ffn_norm_g[0]), w_gu_d, w_dn_d)

    half = ROT_DIM // 2
    inv_freq = ROPE_THETA ** (-(jnp.arange(half, dtype=F32) * 2.0 / ROT_DIM))
    inv_col = jnp.tile(inv_freq, LANES // half).reshape(LANES, 1)
    pos_rows = positions.astype(jnp.int32).reshape(t // ROW_TILE, 1, ROW_TILE)
    cos_t, sin_t = _rope_tables(pos_rows, inv_col)
    kv, qb = _proj_rope(x2, _row(kv_norm_g), _row(mix_norm_g[1]), w_kv, w_b, cos_t, sin_t, diff_w)

    lam_init = 0.8 - 0.6 * math.exp(-0.3 * 1)
    lam_rows = jnp.concatenate(
        [jnp.pad(v[0].astype(F32), (0, LANES - HEAD_DIM)).reshape(1, LANES)
         for v in (lambda_q1, lambda_k1, lambda_q2, lambda_k2)]
        + [jnp.zeros((4, LANES), F32)], axis=0)
    y_self = _diff_attention(lam_rows, _row(subln_g[0]), qb, kv, batch, seq, lam_init)
    mkv1 = _norm_matmul(memf, _row(mem_norm_g[1]), w_mkv[1], "mem_kv1")
    y_mem = _mem_attention(qb, diff_w // MEM_WIDTH, mkv1, batch, seq, mem_tokens)

    tri = jnp.tril(jnp.ones((MOE_PRE_TILE, MOE_PRE_TILE), BF16), -1)
    x3, h_rows, route, route_t, cnt = _moe_pre(x2, y_self, y_mem, w_o[1], _row(ffn_norm_g[1]), w_r, tri)
    tm = MOE_ROW_TILE
    n_tiles = 2 * t // tm + N_EXPERTS
    counts = cnt[0, :N_EXPERTS].astype(jnp.int32)
    tiles_per = (counts + tm - 1) // tm
    tile_end = jnp.cumsum(tiles_per)
    group_off = (tile_end - tiles_per) * tm
    n_active = tile_end[-1:]
    tile_id = jnp.minimum(jnp.arange(n_tiles, dtype=jnp.int32), n_active - 1)
    tile_expert = jnp.sum(tile_id[:, None] >= tile_end[None, :], axis=1).astype(jnp.int32)
    idx = route_t[0:2].astype(jnp.int32)
    rank = route_t[2:4].astype(jnp.int32)
    expert_ids = jnp.arange(N_EXPERTS, dtype=jnp.int32).reshape(N_EXPERTS, 1, 1)
    off = jnp.sum(jnp.where(idx[None] == expert_ids, group_off.reshape(N_EXPERTS, 1, 1), 0), axis=0)
    pos = (off + rank).reshape(2 * t)
    pos_hi, pos_lo = pos // SUBLANES, pos % SUBLANES
    last_tile = jnp.maximum(tile_end - 1, 0).astype(jnp.int32)

    h3 = h_rows.reshape(t // SUBLANES, SUBLANES, d)
    xs3 = _dispatch(pos_hi, pos_lo, last_tile, tiles_per.astype(jnp.int32), h3, n_tiles * tm, tm)
    group_end = group_off + counts
    n_valid = jnp.clip(group_end[tile_expert] - tile_id * tm, 0, tm).astype(jnp.int32)
    y_sorted = _moe_experts(tile_expert, n_active.astype(jnp.int32), n_valid,
                            xs3.reshape(n_tiles * tm, d), w_gu_m, w_dn_m, n_tiles)
    y3 = y_sorted.reshape(n_tiles * tm // SUBLANES, SUBLANES, d)
    out = _combine(pos_hi, pos_lo, x3, route, _row(final_norm_g), y3)
    return out.reshape(batch, seq, d)
```

```python
import functools
import math

import jax
import jax.numpy as jnp
from jax import lax
from jax.experimental import pallas as pl
from jax.experimental.pallas import tpu as pltpu

F32 = jnp.float32
BF16 = jnp.bfloat16

HEAD_DIM = 64
LANES = 128
SUBLANES = 8
CHUNK_SHIFT = 6
FOX_HEADS = 12
DIFF_HEADS = 6
MEM_HEADS = 4
MEM_WIDTH = MEM_HEADS * HEAD_DIM
ROPE_THETA = 500000.0
ROT_DIM = HEAD_DIM // 4
N_EXPERTS = 8
EPS = 1e-5
NEG = -1e30
VMEM_LIMIT = 48 * 1024 * 1024
FFN_VMEM_LIMIT = 56 * 1024 * 1024

ROW_TILE = 1024
FFN_ROW_TILE = 1024
FFN_COL_TILE = 256
ATT_TILE = 256
ROW_CHUNK = 16
ATT_SLOTS = 2
MEM_Q_TILE = 512
MOE_PRE_TILE = 1024
MOE_ROW_TILE = 1024
MOE_COL_TILE = 512
MOE_SRC_STRIDE = 1032
GATHER_TILE = 256


def _params(*sem):
    return pltpu.CompilerParams(dimension_semantics=sem, vmem_limit_bytes=VMEM_LIMIT)


def _rms(x, g):
    ms = jnp.mean(x * x, axis=-1, keepdims=True)
    return x * lax.rsqrt(ms + EPS) * g


def _nt_dot(a, b):
    return lax.dot_general(a, b, (((1,), (1,)), ((), ())), preferred_element_type=F32)


def _dot(a, b):
    return jnp.dot(a, b, preferred_element_type=F32)


def _proj_a_kernel(x_ref, g_ref, w_ref, b_ref, o_ref, lf_ref, *, n_main, col_chunk):
    h = _rms(x_ref[...], g_ref[...]).astype(BF16)
    for c in range(n_main // col_chunk):
        cs = slice(c * col_chunk, (c + 1) * col_chunk)
        o_ref[:, cs] = _dot(h, w_ref[:, cs]).astype(BF16)
    z = _dot(h, w_ref[:, n_main:]) + b_ref[...]
    lf_ref[...] = jnp.minimum(z, 0.0) - jnp.log1p(jnp.exp(-jnp.abs(z)))


def _proj_a(x, g, w, b_pad):
    t, d = x.shape
    n_all = w.shape[1]
    n_main = n_all - LANES
    return pl.pallas_call(
        functools.partial(_proj_a_kernel, n_main=n_main, col_chunk=512),
        grid=(t // ROW_TILE,),
        in_specs=[
            pl.BlockSpec((ROW_TILE, d), lambda i: (i, 0)),
            pl.BlockSpec((1, d), lambda i: (0, 0)),
            pl.BlockSpec((d, n_all), lambda i: (0, 0)),
            pl.BlockSpec((1, LANES), lambda i: (0, 0)),
        ],
        out_specs=[
            pl.BlockSpec((ROW_TILE, n_main), lambda i: (i, 0)),
            pl.BlockSpec((ROW_TILE, LANES), lambda i: (i, 0)),
        ],
        out_shape=[
            jax.ShapeDtypeStruct((t, n_main), BF16),
            jax.ShapeDtypeStruct((t, LANES), F32),
        ],
        compiler_params=_params("parallel"),
        name="proj_a",
    )(x, g, w, b_pad)


def _cumsum_kernel(lf_ref, ct_ref):
    x = lf_ref[...]
    s = x.shape[0]
    row = lax.broadcasted_iota(jnp.int32, x.shape, 0)
    sh = 1
    while sh < s:
        x = x + jnp.where(row >= sh, pltpu.roll(x, sh, 0), 0.0)
        sh *= 2
    ct_ref[0] = x.T


def _cumsum(lf, batch, seq):
    return pl.pallas_call(
        _cumsum_kernel,
        grid=(batch,),
        in_specs=[pl.BlockSpec((seq, LANES), lambda b: (b, 0))],
        out_specs=pl.BlockSpec((1, LANES, seq), lambda b: (b, 0, 0)),
        out_shape=jax.ShapeDtypeStruct((batch, LANES, seq), F32),
        compiler_params=_params("parallel"),
        name="cumsum",
    )(lf)


def _softmax_rows(s_ref, p_ref, m_ref, hi, tq, col_bias, keep_fn):
    lo = hi - tq
    shape = (ROW_CHUNK, LANES)

    def visibility(r, c):
        r0, c0 = r * ROW_CHUNK, c * LANES - lo
        if c0 < 0 or keep_fn(r0, c0 + LANES - 1):
            return "all"
        if not keep_fn(r0 + ROW_CHUNK - 1, c0):
            return "none"
        ri = lax.broadcasted_iota(jnp.int32, shape, 0) + r0
        ci = lax.broadcasted_iota(jnp.int32, shape, 1) + c0
        return keep_fn(ri, ci)

    def load(r, c, vis):
        t = s_ref[r * ROW_CHUNK:(r + 1) * ROW_CHUNK, c * LANES:(c + 1) * LANES]
        if col_bias is not None:
            t = t - col_bias[:, c * LANES:(c + 1) * LANES]
        return t if isinstance(vis, str) else jnp.where(vis, t, NEG)

    for r in range(tq // ROW_CHUNK):
        m_acc = None
        for c in range(hi // LANES):
            vis = visibility(r, c)
            if isinstance(vis, str) and vis == "none":
                continue
            t = load(r, c, vis)
            m_acc = t if m_acc is None else jnp.maximum(m_acc, t)
        m_ref[r * ROW_CHUNK:(r + 1) * ROW_CHUNK, :] = jnp.broadcast_to(
            jnp.max(m_acc, axis=1, keepdims=True), shape)
    for r in range(tq // ROW_CHUNK):
        rows = slice(r * ROW_CHUNK, (r + 1) * ROW_CHUNK)
        m = m_ref[rows, :]
        for c in range(hi // LANES):
            cols = slice(c * LANES, (c + 1) * LANES)
            vis = visibility(r, c)
            if isinstance(vis, str) and vis == "none":
                p_ref[rows, cols] = jnp.zeros(shape, BF16)
            else:
                p_ref[rows, cols] = jnp.exp((load(r, c, vis) - m).astype(BF16))


def _attention_sweep(q_ref, k_ref, v_aug, scratch, tq, col_bias_fn, keep_fn, emit):
    seq = q_ref.shape[0]
    n_items = 2 * (seq // tq)
    n_slots = scratch[0].shape[0]
    qh, res = {}, {}

    def bufs(n):
        return [sc.at[n % n_slots] for sc in scratch]

    def score(n):
        qi, hh = divmod(n, 2)
        hi = (qi + 1) * tq
        if hh == 0:
            qh[qi] = _split_heads(q_ref[qi * tq:hi, :])
        bufs(n)[0][:, :hi] = _nt_dot(qh[qi][hh], k_ref[:hi, :])

    def softmax(n):
        qi, hh = divmod(n, 2)
        hi = (qi + 1) * tq
        bias = None if col_bias_fn is None else col_bias_fn(hh, hi)
        _softmax_rows(*bufs(n), hi, tq, bias, keep_fn)

    def values(n):
        qi, hh = divmod(n, 2)
        hi = (qi + 1) * tq
        res[hh] = _dot(bufs(n)[1][:, :hi], v_aug(hh)[:hi, :])
        if hh == 1:
            emit(qi, res[0], res[1])

    score(0)
    for n in range(n_items):
        if n + 1 < n_items:
            score(n + 1)
        softmax(n)
        if n:
            values(n - 1)
    values(n_items - 1)


def _split_heads(q):
    low = lax.broadcasted_iota(jnp.int32, (1, LANES), 1) < HEAD_DIM
    zero = jnp.zeros_like(q)
    return jnp.where(low, q, zero), jnp.where(low, zero, q)


def _fox_kernel(q_ref, k_ref, v_ref, ct_ref, o_ref, s_sc, p_sc, m_sc, va_sc, *, tq):
    sub = (2 * pl.program_id(1)) & 7
    low = lax.broadcasted_iota(jnp.int32, (1, LANES), 1) < HEAD_DIM
    v = v_ref[...]
    one = jnp.ones_like(v)
    va_sc[0] = jnp.where(low, v, one)
    va_sc[1] = jnp.where(low, one, v)

    def keep_fn(row, col):
        return col <= row

    def col_bias(hh, hi):
        return ct_ref[0, pl.ds(sub + hh, 1), :hi]

    def emit(qi, r0, r1):
        num = jnp.where(low, r0, r1)
        den = pltpu.roll(jnp.where(low, r1, r0), HEAD_DIM, 1)
        o_ref[qi * tq:(qi + 1) * tq, :] = (num / den).astype(BF16)

    _attention_sweep(q_ref, k_ref, lambda hh: va_sc.at[hh], (s_sc, p_sc, m_sc), tq, col_bias,
                     keep_fn, emit)


def _attention_scratch(tq, seq):
    n = 2 * ATT_SLOTS
    return [pltpu.VMEM((n, tq, seq), F32), pltpu.VMEM((n, tq, seq), BF16),
            pltpu.VMEM((n, tq, LANES), F32)]


def _fox_attention(qkvm, cum_t, batch, seq):
    t = batch * seq
    npair = FOX_HEADS // 2
    return pl.pallas_call(
        functools.partial(_fox_kernel, tq=ATT_TILE),
        grid=(batch, npair),
        in_specs=[
            pl.BlockSpec((seq, LANES), lambda b, p: (b, p)),
            pl.BlockSpec((seq, LANES), lambda b, p: (b, npair + p)),
            pl.BlockSpec((seq, LANES), lambda b, p: (b, 2 * npair + p)),
            pl.BlockSpec((1, 8, seq), lambda b, p: (b, p // 4, 0)),
        ],
        out_specs=pl.BlockSpec((seq, LANES), lambda b, p: (b, p)),
        out_shape=jax.ShapeDtypeStruct((t, npair * LANES), BF16),
        scratch_shapes=_attention_scratch(ATT_TILE, seq) + [pltpu.VMEM((2, seq, LANES), BF16)],
        compiler_params=_params("parallel", "parallel"),
        name="fox_attention",
    )(qkvm, qkvm, qkvm, cum_t)


def _mem_kernel(q_ref, mk_ref, mv_ref, o_ref):
    q = q_ref[...]
    mk = mk_ref[...]
    mv = mv_ref[...]
    lane = lax.broadcasted_iota(jnp.int32, (1, MEM_WIDTH), 1)
    zero = jnp.zeros_like(q)
    out = jnp.zeros(q.shape, F32)
    for h in range(MEM_HEADS):
        hm = (lane >= h * HEAD_DIM) & (lane < (h + 1) * HEAD_DIM)
        s = _nt_dot(jnp.where(hm, q, zero), mk)
        p = jnp.exp(s - jnp.max(s, axis=1, keepdims=True))
        l = jnp.sum(p, axis=1, keepdims=True)
        out = jnp.where(hm, _dot(p.astype(BF16), mv) / l, out)
    o_ref[...] = out.astype(BF16)


def _mem_attention(q_arr, q_col_block, mkv, batch, seq, mem_tokens):
    nq = seq // MEM_Q_TILE
    return pl.pallas_call(
        _mem_kernel,
        grid=(batch, nq),
        in_specs=[
            pl.BlockSpec((MEM_Q_TILE, MEM_WIDTH), lambda b, i: (b * nq + i, q_col_block)),
            pl.BlockSpec((mem_tokens, MEM_WIDTH), lambda b, i: (b, 0)),
            pl.BlockSpec((mem_tokens, MEM_WIDTH), lambda b, i: (b, 1)),
        ],
        out_specs=pl.BlockSpec((MEM_Q_TILE, MEM_WIDTH), lambda b, i: (b * nq + i, 0)),
        out_shape=jax.ShapeDtypeStruct((batch * seq, MEM_WIDTH), BF16),
        compiler_params=_params("parallel", "parallel"),
        name="mem_attention",
    )(q_arr, mkv, mkv)


def _norm_matmul_kernel(x_ref, g_ref, w_ref, o_ref):
    h = _rms(x_ref[...], g_ref[...]).astype(BF16)
    o_ref[...] = _dot(h, w_ref[...]).astype(o_ref.dtype)


def _norm_matmul(x, g, w, name):
    t, d = x.shape
    n = w.shape[1]
    return pl.pallas_call(
        _norm_matmul_kernel,
        grid=(t // ROW_TILE,),
        in_specs=[
            pl.BlockSpec((ROW_TILE, d), lambda i: (i, 0)),
            pl.BlockSpec((1, d), lambda i: (0, 0)),
            pl.BlockSpec((d, n), lambda i: (0, 0)),
        ],
        out_specs=pl.BlockSpec((ROW_TILE, n), lambda i: (i, 0)),
        out_shape=jax.ShapeDtypeStruct((t, n), BF16),
        compiler_params=_params("parallel"),
        name=name,
    )(x, g, w)


def _mixed_residual(x_ref, ys_ref, ym_ref, wo_ref):
    sw = ys_ref.shape[1]
    return x_ref[...] + _dot(ys_ref[...], wo_ref[:sw, :]) + _dot(ym_ref[...], wo_ref[sw:, :])


def _swiglu_hidden(g, u):
    return (g * jax.nn.sigmoid(g) * u).astype(BF16)


def _ffn_kernel(x_ref, ys_ref, ym_ref, wo_ref, g_ref, wgu_ref, wd_ref, o_ref, h_sc, *, col):
    ff = wd_ref.shape[0]
    x = _mixed_residual(x_ref, ys_ref, ym_ref, wo_ref)
    h_sc[...] = _rms(x, g_ref[...]).astype(BF16)
    o_ref[...] = x
    for c in range(ff // col):
        h = h_sc[...]
        gate = _dot(h, wgu_ref[:, c * col:(c + 1) * col])
        up = _dot(h, wgu_ref[:, ff + c * col:ff + (c + 1) * col])
        o_ref[...] += _dot(_swiglu_hidden(gate, up), wd_ref[c * col:(c + 1) * col, :])


def _dense_ffn(x, y_self, y_mem, w_out, g, w_gate_up, w_down):
    t, d = x.shape
    tm = FFN_ROW_TILE
    row = lambda i: (i, 0)
    resident = dict(index_map=lambda i: (0, 0), pipeline_mode=pl.Buffered(1))
    return pl.pallas_call(
        functools.partial(_ffn_kernel, col=FFN_COL_TILE),
        grid=(t // tm,),
        in_specs=[
            pl.BlockSpec((tm, d), row),
            pl.BlockSpec((tm, y_self.shape[1]), row),
            pl.BlockSpec((tm, y_mem.shape[1]), row),
            pl.BlockSpec(w_out.shape, **resident),
            pl.BlockSpec((1, d), lambda i: (0, 0)),
            pl.BlockSpec(w_gate_up.shape, **resident),
            pl.BlockSpec(w_down.shape, **resident),
        ],
        out_specs=pl.BlockSpec((tm, d), row),
        out_shape=jax.ShapeDtypeStruct((t, d), F32),
        scratch_shapes=[pltpu.VMEM((tm, d), BF16)],
        compiler_params=pltpu.CompilerParams(dimension_semantics=("parallel",),
                                             vmem_limit_bytes=FFN_VMEM_LIMIT),
        name="dense_ffn",
    )(x, y_self, y_mem, w_out, g, w_gate_up, w_down)


def _rope_kernel(pos_ref, inv_ref, cos_ref, sin_ref):
    pos = pos_ref[0].astype(F32)
    ang = (inv_ref[...] * pos).T
    lane = lax.broadcasted_iota(jnp.int32, ang.shape, 1)
    rot = (lane & (HEAD_DIM - 1)) < ROT_DIM
    first = (lane & (ROT_DIM - 1)) < (ROT_DIM // 2)
    sn = jnp.sin(ang)
    cos_ref[...] = jnp.where(rot, jnp.cos(ang), 1.0)
    sin_ref[...] = jnp.where(rot, jnp.where(first, -sn, sn), 0.0)


def _rope_tables(pos_rows, inv_col):
    steps = pos_rows.shape[0]
    t = steps * ROW_TILE
    return pl.pallas_call(
        _rope_kernel,
        grid=(steps,),
        in_specs=[
            pl.BlockSpec((1, 1, ROW_TILE), lambda i: (i, 0, 0)),
            pl.BlockSpec((LANES, 1), lambda i: (0, 0)),
        ],
        out_specs=[pl.BlockSpec((ROW_TILE, LANES), lambda i: (i, 0))] * 2,
        out_shape=[jax.ShapeDtypeStruct((t, LANES), F32)] * 2,
        compiler_params=_params("parallel"),
        name="rope_tables",
    )(pos_rows, inv_col)


def _proj_rope_kernel(x_ref, gk_ref, gq_ref, wk_ref, wq_ref, cos_ref, sin_ref, ok_ref, oq_ref, *,
                      n_rope, col_chunk):
    x = x_ref[...]
    xn = x * lax.rsqrt(jnp.mean(x * x, axis=-1, keepdims=True) + EPS)
    cos = cos_ref[...]
    sin = sin_ref[...]
    lane = lax.broadcasted_iota(jnp.int32, (1, LANES), 1)
    first = (lane & (ROT_DIM - 1)) < (ROT_DIM // 2)
    half = ROT_DIM // 2
    for g_ref, w_ref, o_ref in ((gk_ref, wk_ref, ok_ref), (gq_ref, wq_ref, oq_ref)):
        h = (xn * g_ref[...]).astype(BF16)
        for c in range(w_ref.shape[1] // col_chunk):
            a = _dot(h, w_ref[:, c * col_chunk:(c + 1) * col_chunk])
            for s in range(col_chunk // LANES):
                col = c * col_chunk + s * LANES
                blk = a[:, s * LANES:(s + 1) * LANES]
                if col < n_rope:
                    partner = jnp.where(first, pltpu.roll(blk, LANES - half, 1),
                                        pltpu.roll(blk, half, 1))
                    blk = blk * cos + partner * sin
                o_ref[:, col:col + LANES] = blk.astype(BF16)


def _proj_rope(x, g_kv, g_q, w_kv, w_q, cos, sin, n_rope):
    t, d = x.shape
    nk, nq = w_kv.shape[1], w_q.shape[1]
    row = lambda i: (i, 0)
    fixed = lambda i: (0, 0)
    return pl.pallas_call(
        functools.partial(_proj_rope_kernel, n_rope=n_rope, col_chunk=256),
        grid=(t // ROW_TILE,),
        in_specs=[
            pl.BlockSpec((ROW_TILE, d), row),
            pl.BlockSpec((1, d), fixed),
            pl.BlockSpec((1, d), fixed),
            pl.BlockSpec((d, nk), fixed),
            pl.BlockSpec((d, nq), fixed),
            pl.BlockSpec((ROW_TILE, LANES), row),
            pl.BlockSpec((ROW_TILE, LANES), row),
        ],
        out_specs=[pl.BlockSpec((ROW_TILE, nk), row), pl.BlockSpec((ROW_TILE, nq), row)],
        out_shape=[jax.ShapeDtypeStruct((t, nk), BF16), jax.ShapeDtypeStruct((t, nq), BF16)],
        compiler_params=_params("parallel"),
        name="proj_kv_q",
    )(x, g_kv, g_q, w_kv, w_q, cos, sin)


def _diff_kernel(lam_ref, sg_ref, q_ref, k_ref, v_ref, o_ref, s_sc, p_sc, m_sc, va_sc, *, tq,
                 lam_init):
    lp = lam_ref[...]
    lam = (jnp.exp(jnp.sum(lp[0:1] * lp[1:2], axis=1, keepdims=True))
           - jnp.exp(jnp.sum(lp[2:3] * lp[3:4], axis=1, keepdims=True)) + lam_init)
    va_sc[:, :LANES] = v_ref[...]
    va_sc[:, LANES:] = jnp.ones(v_ref.shape, BF16)

    def keep_fn(row, col):
        return (col >> CHUNK_SHIFT) <= (row >> CHUNK_SHIFT)

    def emit(qi, r1, r2):
        o = r1[:, :LANES] / r1[:, LANES:] - lam * (r2[:, :LANES] / r2[:, LANES:])
        o = _rms(o, sg_ref[...]) * (1.0 - lam_init)
        o_ref[qi * tq:(qi + 1) * tq, :] = o.astype(BF16)

    _attention_sweep(q_ref, k_ref, lambda hh: va_sc, (s_sc, p_sc, m_sc), tq, None, keep_fn, emit)


def _diff_attention(lam_rows, subln_g, qb, kv, batch, seq, lam_init):
    t = batch * seq
    return pl.pallas_call(
        functools.partial(_diff_kernel, tq=ATT_TILE, lam_init=lam_init),
        grid=(batch, DIFF_HEADS),
        in_specs=[
            pl.BlockSpec((8, LANES), lambda b, h: (0, 0)),
            pl.BlockSpec((1, LANES), lambda b, h: (0, 0)),
            pl.BlockSpec((seq, LANES), lambda b, h: (b, h)),
            pl.BlockSpec((seq, LANES), lambda b, h: (b, h)),
            pl.BlockSpec((seq, LANES), lambda b, h: (b, DIFF_HEADS + h)),
        ],
        out_specs=pl.BlockSpec((seq, LANES), lambda b, h: (b, h)),
        out_shape=jax.ShapeDtypeStruct((t, DIFF_HEADS * LANES), BF16),
        scratch_shapes=_attention_scratch(ATT_TILE, seq) + [pltpu.VMEM((seq, 2 * LANES), BF16)],
        compiler_params=_params("parallel", "parallel"),
        name="diff_attention",
    )(lam_rows, subln_g, qb, kv, kv)


def _moe_pre_kernel(x_ref, ys_ref, ym_ref, wo_ref, g_ref, wr_ref, tri_ref, x3_ref, hp_ref, route_ref,
                    route_t_ref, cnt_ref, run_sc):
    @pl.when(pl.program_id(0) == 0)
    def _():
        run_sc[...] = jnp.zeros_like(run_sc)

    x = _mixed_residual(x_ref, ys_ref, ym_ref, wo_ref)
    x3_ref[...] = x
    hf = _rms(x, g_ref[...])
    hp_ref[...] = hf

    h_hi = hf.astype(BF16)
    h_lo = (hf - h_hi.astype(F32)).astype(BF16)
    logits = _dot(jnp.concatenate([h_hi, h_lo, h_hi], axis=1), wr_ref[...])
    lane = lax.broadcasted_iota(jnp.int32, logits.shape, 1)
    lanef = lane.astype(F32)
    lg = jnp.where(lane < N_EXPERTS, logits, NEG)
    v1 = jnp.max(lg, axis=1, keepdims=True)
    i1 = jnp.min(jnp.where(lg == v1, lanef, float(LANES)), axis=1, keepdims=True)
    lg2 = jnp.where(lanef == i1, NEG, lg)
    v2 = jnp.max(lg2, axis=1, keepdims=True)
    i2 = jnp.min(jnp.where(lg2 == v2, lanef, float(LANES)), axis=1, keepdims=True)
    e = jnp.exp(v2 - v1)
    g1 = 1.0 / (1.0 + e)
    g2 = e / (1.0 + e)

    oh1 = lanef == i1
    oh2 = lanef == i2
    oh = jnp.where(oh1 | oh2, 1.0, 0.0)
    before = _dot(tri_ref[...], oh.astype(BF16)) + run_sc[...]
    r1 = jnp.sum(jnp.where(oh1, before, 0.0), axis=1, keepdims=True)
    r2 = jnp.sum(jnp.where(oh2, before, 0.0), axis=1, keepdims=True)
    run_sc[...] += jnp.sum(oh, axis=0, keepdims=True)
    cnt_ref[...] = run_sc[...]

    route = jnp.where(lane == 0, i1, 0.0)
    for ln, val in ((1, i2), (2, r1), (3, r2), (4, g1), (5, g2)):
        route = jnp.where(lane == ln, val, route)
    route_ref[...] = route
    route_t_ref[...] = route.T[:SUBLANES, :]


def _moe_pre(x, y_self, y_mem, w_out, g, w_router_pad, tri):
    t, d = x.shape
    tm = tri.shape[0]
    row = lambda i: (i, 0)
    fixed = lambda i: (0, 0)
    return pl.pallas_call(
        _moe_pre_kernel,
        grid=(t // tm,),
        in_specs=[
            pl.BlockSpec((tm, d), row),
            pl.BlockSpec((tm, y_self.shape[1]), row),
            pl.BlockSpec((tm, y_mem.shape[1]), row),
            pl.BlockSpec(w_out.shape, fixed),
            pl.BlockSpec((1, d), fixed),
            pl.BlockSpec((3 * d, LANES), fixed),
            pl.BlockSpec((tm, tm), fixed),
        ],
        out_specs=[
            pl.BlockSpec((tm, d), row),
            pl.BlockSpec((tm, d), row),
            pl.BlockSpec((tm, LANES), row),
            pl.BlockSpec((SUBLANES, tm), lambda i: (0, i)),
            pl.BlockSpec((1, LANES), fixed),
        ],
        out_shape=[
            jax.ShapeDtypeStruct((t, d), F32),
            jax.ShapeDtypeStruct((t, d), F32),
            jax.ShapeDtypeStruct((t, LANES), F32),
            jax.ShapeDtypeStruct((SUBLANES, t), F32),
            jax.ShapeDtypeStruct((1, LANES), F32),
        ],
        scratch_shapes=[pltpu.VMEM((1, LANES), F32)],
        compiler_params=_params("arbitrary"),
        name="moe_pre",
    )(x, y_self, y_mem, w_out, g, w_router_pad, tri)


def _moe_kernel(te_ref, na_ref, nv_ref, src_ref, h_ref, wg_ref, wu_ref, wd_ref, o_ref, xg_sc, xb_sc,
                gsem, *, chunk, stride):
    del te_ref
    i = pl.program_id(0)
    f = pl.program_id(1)
    tm = o_ref.shape[0]
    n_rows_copied = pl.num_programs(1) * chunk
    slot = i & 1

    def row_copy(tile, sl, r):
        src = src_ref[tile * stride + r]
        return pltpu.make_async_copy(h_ref.at[pl.ds(src, 1)], xg_sc.at[sl, pl.ds(r, 1)],
                                     gsem.at[sl])

    def wait_tile(tile, sl):
        def body(r, c):
            row_copy(tile, sl, r).wait()
            return c
        lax.fori_loop(0, n_rows_copied, body, 0, unroll=8)

    @pl.when(f == 0)
    def _():
        o_ref[...] = jnp.zeros_like(o_ref)

    @pl.when((i == 0) & (f == 0))
    def _():
        def body(r, c):
            row_copy(0, 0, r).start()
            return c
        lax.fori_loop(0, n_rows_copied, body, 0, unroll=8)

    @pl.when((i == na_ref[0]) & (f == 0))
    def _():
        wait_tile(i, slot)

    @pl.when(i < na_ref[0])
    def _():
        @pl.when(f == 0)
        def _():
            wait_tile(i, slot)
            xb_sc[...] = xg_sc[slot, :tm, :].astype(BF16)

        def prefetch(lo, hi):
            for r in range(lo, hi):
                row_copy(i + 1, 1 - slot, f * chunk + r).start()

        def expert_rows(n_rows):
            xb = xb_sc[:n_rows, :]
            g = _dot(xb, wg_ref[0].astype(BF16))
            prefetch(0, chunk // 2)
            u = _dot(xb, wu_ref[0].astype(BF16))
            prefetch(chunk // 2, chunk)
            o_ref[:n_rows, :] += _dot(_swiglu_hidden(g, u), wd_ref[0].astype(BF16))

        @pl.when(nv_ref[i] > tm // 2)
        def _():
            expert_rows(tm)

        @pl.when(nv_ref[i] <= tm // 2)
        def _():
            expert_rows(tm // 2)


def _moe_experts(tile_expert, n_active, n_valid, src, h_rows, w_gate_up, w_down, n_tiles):
    d = w_down.shape[2]
    ff = w_down.shape[1]
    ct = MOE_COL_TILE
    nf = ff // ct
    tm = MOE_ROW_TILE
    chunk = -(-tm // nf)
    assert nf * chunk <= MOE_SRC_STRIDE

    def col(i, f, na):
        return jnp.where(i < na[0], f, nf - 1)

    return pl.pallas_call(
        functools.partial(_moe_kernel, chunk=chunk, stride=MOE_SRC_STRIDE),
        grid_spec=pltpu.PrefetchScalarGridSpec(
            num_scalar_prefetch=4,
            grid=(n_tiles, nf),
            in_specs=[
                pl.BlockSpec(memory_space=pl.ANY),
                pl.BlockSpec((1, d, ct), lambda i, f, te, na, *_: (te[i], 0, col(i, f, na))),
                pl.BlockSpec((1, d, ct), lambda i, f, te, na, *_: (te[i], 0, nf + col(i, f, na))),
                pl.BlockSpec((1, ct, d), lambda i, f, te, na, *_: (te[i], col(i, f, na), 0)),
            ],
            out_specs=pl.BlockSpec((tm, d), lambda i, f, *_: (i, 0)),
            scratch_shapes=[pltpu.VMEM((2, MOE_SRC_STRIDE, d), F32), pltpu.VMEM((tm, d), BF16),
                            pltpu.SemaphoreType.DMA((2,))],
        ),
        out_shape=jax.ShapeDtypeStruct((n_tiles * tm, d), F32),
        compiler_params=_params("arbitrary", "arbitrary"),
        name="moe_experts",
    )(tile_expert, n_active, n_valid, src, h_rows, w_gate_up, w_gate_up, w_down)


def _combine_kernel(hi_ref, lo_ref, x_ref, route_ref, g_ref, y_ref, o_ref, buf, sem, *, tile):
    i = pl.program_id(0)
    slot = i & 1
    groups = tile // SUBLANES

    n_tokens = hi_ref.shape[0] // 2

    def row_copy(step, sl, j, u, k):
        n = k * n_tokens + step * tile + j * SUBLANES + u
        return pltpu.make_async_copy(y_ref.at[hi_ref[n], pl.ds(lo_ref[n], 1)],
                                     buf.at[sl, k, j, pl.ds(u, 1)], sem.at[sl])

    def gather(step, sl):
        def start(j, c):
            for u in range(SUBLANES):
                row_copy(step, sl, j, u, 0).start(priority=0)
                row_copy(step, sl, j, u, 1).start(priority=1)
            return c
        lax.fori_loop(0, groups, start, 0)

    @pl.when(i == 0)
    def _():
        gather(0, 0)

    @pl.when(i + 1 < pl.num_programs(0))
    def _():
        gather(i + 1, 1 - slot)

    def wait(j, c):
        for u in range(SUBLANES):
            row_copy(i, slot, j, u, 0).wait()
            row_copy(i, slot, j, u, 1).wait()
        return c

    lax.fori_loop(0, groups, wait, 0)
    route = route_ref[...]
    d = x_ref.shape[1]
    y0 = buf[slot, 0].reshape(tile, d)
    y1 = buf[slot, 1].reshape(tile, d)
    y = x_ref[...] + route[:, 4:5] * y0 + route[:, 5:6] * y1
    o_ref[...] = _rms(y, g_ref[...])


def _combine(pos_hi, pos_lo, x, route, g, y3):
    t, d = x.shape
    groups = GATHER_TILE // SUBLANES
    return pl.pallas_call(
        functools.partial(_combine_kernel, tile=GATHER_TILE),
        grid_spec=pltpu.PrefetchScalarGridSpec(
            num_scalar_prefetch=2,
            grid=(t // GATHER_TILE,),
            in_specs=[
                pl.BlockSpec((GATHER_TILE, d), lambda i, *_: (i, 0)),
                pl.BlockSpec((GATHER_TILE, LANES), lambda i, *_: (i, 0)),
                pl.BlockSpec((1, d), lambda i, *_: (0, 0)),
                pl.BlockSpec(memory_space=pl.ANY),
            ],
            out_specs=pl.BlockSpec((GATHER_TILE, d), lambda i, *_: (i, 0)),
            scratch_shapes=[pltpu.VMEM((2, 2, groups, SUBLANES, d), F32),
                            pltpu.SemaphoreType.DMA((2,))],
        ),
        out_shape=jax.ShapeDtypeStruct((t, d), F32),
        compiler_params=_params("arbitrary"),
        name="moe_combine",
    )(pos_hi, pos_lo, x, route, g, y3)


def _row(v):
    return v.reshape(1, -1).astype(F32)


def _pad_lanes(v):
    return jnp.pad(v.astype(F32), (0, LANES - v.shape[0])).reshape(1, LANES)


def kernel(x, mem, positions, mix_norm_g, ffn_norm_g, mem_norm_g, w_mem_kv, w_out, w_in_a, b_forget, w_q_b, lambda_q1, lambda_k1, lambda_q2, lambda_k2, subln_g, kv_norm_g, w_kv_shared, w_gate_up_dense, w_down_dense, w_router, w_gate_up_moe, w_down_moe, final_norm_g):
    batch, seq, d = x.shape
    mem_tokens = mem.shape[1]
    t = batch * seq
    assert w_in_a.shape[0] == 1 and w_q_b.shape[0] == 1 and w_out.shape[0] == 2
    fox_w = FOX_HEADS * HEAD_DIM
    diff_w = DIFF_HEADS * 2 * HEAD_DIM
    scale = HEAD_DIM ** -0.5

    xf = x.reshape(t, d)
    memf = mem.reshape(batch * mem_tokens, d)

    wa = w_in_a[0]
    w_fl = jnp.pad(wa[:, 3 * fox_w:3 * fox_w + FOX_HEADS], ((0, 0), (0, LANES - FOX_HEADS)))
    w_a = jnp.concatenate(
        [wa[:, :fox_w] * scale, wa[:, fox_w:3 * fox_w], wa[:, 3 * fox_w + FOX_HEADS:] * scale, w_fl],
        axis=1).astype(BF16)
    w_b = (w_q_b[0] * scale).astype(BF16)
    w_kv = w_kv_shared.astype(BF16)
    w_o = w_out.astype(BF16)
    w_mkv = w_mem_kv.astype(BF16)
    w_gu_d = w_gate_up_dense[0].astype(BF16)
    w_dn_d = w_down_dense[0].astype(BF16)
    w_gu_m = w_gate_up_moe[0]
    w_dn_m = w_down_moe[0]
    w_r = jnp.pad(w_router[0].astype(F32), ((0, 0), (0, LANES - N_EXPERTS)))
    w_r_hi = w_r.astype(BF16)
    w_r_lo = (w_r - w_r_hi.astype(F32)).astype(BF16)
    w_r = jnp.concatenate([w_r_hi, w_r_hi, w_r_lo], axis=0)

    qkvm, log_f = _proj_a(xf, _row(mix_norm_g[0]), w_a, _pad_lanes(b_forget[0]))
    cum_t = _cumsum(log_f, batch, seq)
    y_self = _fox_attention(qkvm, cum_t, batch, seq)
    mkv0 = _norm_matmul(memf, _row(mem_norm_g[0]), w_mkv[0], "mem_kv0")
    y_mem = _mem_attention(qkvm, 3 * fox_w // MEM_WIDTH, mkv0, batch, seq, mem_tokens)
    x2 = _dense_ffn(xf, y_self, y_mem, w_o[0], _row(ffn_norm_g[0]), w_gu_d, w_dn_d)

    half = ROT_DIM // 2
    inv_freq = ROPE_THETA ** (-(jnp.arange(half, dtype=F32) * 2.0 / ROT_DIM))
    inv_col = jnp.tile(inv_freq, LANES // half).reshape(LANES, 1)
    pos_rows = positions.astype(jnp.int32).reshape(t // ROW_TILE, 1, ROW_TILE)
    cos_t, sin_t = _rope_tables(pos_rows, inv_col)
    kv, qb = _proj_rope(x2, _row(kv_norm_g), _row(mix_norm_g[1]), w_kv, w_b, cos_t, sin_t, diff_w)

    lam_init = 0.8 - 0.6 * math.exp(-0.3 * 1)
    lam_rows = jnp.concatenate(
        [jnp.pad(v[0].astype(F32), (0, LANES - HEAD_DIM)).reshape(1, LANES)
         for v in (lambda_q1, lambda_k1, lambda_q2, lambda_k2)]
        + [jnp.zeros((4, LANES), F32)], axis=0)
    y_self = _diff_attention(lam_rows, _row(subln_g[0]), qb, kv, batch, seq, lam_init)
    mkv1 = _norm_matmul(memf, _row(mem_norm_g[1]), w_mkv[1], "mem_kv1")
    y_mem = _mem_attention(qb, diff_w // MEM_WIDTH, mkv1, batch, seq, mem_tokens)

    tri = jnp.tril(jnp.ones((MOE_PRE_TILE, MOE_PRE_TILE), BF16), -1)
    x3, h_rows, route, route_t, cnt = _moe_pre(x2, y_self, y_mem, w_o[1], _row(ffn_norm_g[1]), w_r, tri)
    tm = MOE_ROW_TILE
    n_tiles = 2 * t // tm + N_EXPERTS
    counts = cnt[0, :N_EXPERTS].astype(jnp.int32)
    tiles_per = (counts + tm - 1) // tm
    tile_end = jnp.cumsum(tiles_per)
    group_off = (tile_end - tiles_per) * tm
    n_active = tile_end[-1:]
    tile_id = jnp.minimum(jnp.arange(n_tiles, dtype=jnp.int32), n_active - 1)
    tile_expert = jnp.sum(tile_id[:, None] >= tile_end[None, :], axis=1).astype(jnp.int32)
    idx = route_t[0:2].astype(jnp.int32)
    rank = route_t[2:4].astype(jnp.int32)
    expert_ids = jnp.arange(N_EXPERTS, dtype=jnp.int32).reshape(N_EXPERTS, 1, 1)
    off = jnp.sum(jnp.where(idx[None] == expert_ids, group_off.reshape(N_EXPERTS, 1, 1), 0), axis=0)
    pos = (off + rank).reshape(2 * t)
    pos_hi, pos_lo = pos // SUBLANES, pos % SUBLANES

    slot_ids = jnp.arange(n_tiles * tm, dtype=jnp.int32)
    inv = jnp.full((n_tiles * tm,), -1, jnp.int32).at[pos].set(
        jnp.arange(2 * t, dtype=jnp.int32) % t, unique_indices=True)
    src = jnp.where(inv >= 0, inv, slot_ids % t).reshape(n_tiles, tm)
    src = jnp.pad(src, ((0, 0), (0, MOE_SRC_STRIDE - tm))).reshape(n_tiles * MOE_SRC_STRIDE)
    group_end = group_off + counts
    n_valid = jnp.clip(group_end[tile_expert] - tile_id * tm, 0, tm).astype(jnp.int32)
    y_sorted = _moe_experts(tile_expert, n_active.astype(jnp.int32), n_valid, src, h_rows,
                            w_gu_m, w_dn_m, n_tiles)
    y3 = y_sorted.reshape(n_tiles * tm // SUBLANES, SUBLANES, d)
    out = _combine(pos_hi, pos_lo, x3, route, _row(final_norm_g), y3)
    return out.reshape(batch, seq, d)
```

```python
import functools
import math

import jax
import jax.numpy as jnp
from jax import lax
from jax.experimental import pallas as pl
from jax.experimental.pallas import tpu as pltpu

F32 = jnp.float32
BF16 = jnp.bfloat16

HEAD_DIM = 64
LANES = 128
SUBLANES = 8
CHUNK_SHIFT = 6
FOX_HEADS = 12
DIFF_HEADS = 6
MEM_HEADS = 4
MEM_WIDTH = MEM_HEADS * HEAD_DIM
ROPE_THETA = 500000.0
ROT_DIM = HEAD_DIM // 4
N_EXPERTS = 8
EPS = 1e-5
NEG = -1e30
VMEM_LIMIT = 48 * 1024 * 1024
FFN_VMEM_LIMIT = 56 * 1024 * 1024

ROW_TILE = 1024
FFN_ROW_TILE = 1024
FFN_COL_TILE = 256
ATT_TILE = 256
ROW_CHUNK = 16
ATT_SLOTS = 2
MEM_Q_TILE = 512
MOE_PRE_TILE = 1024
MOE_ROW_TILE = 1024
MOE_COL_TILE = 512
MOE_SRC_STRIDE = 1032
GATHER_TILE = 256


def _params(*sem):
    return pltpu.CompilerParams(dimension_semantics=sem, vmem_limit_bytes=VMEM_LIMIT)


def _rms(x, g):
    ms = jnp.mean(x * x, axis=-1, keepdims=True)
    return x * lax.rsqrt(ms + EPS) * g


def _nt_dot(a, b):
    return lax.dot_general(a, b, (((1,), (1,)), ((), ())), preferred_element_type=F32)


def _dot(a, b):
    return jnp.dot(a, b, preferred_element_type=F32)


def _proj_a_kernel(x_ref, g_ref, w_ref, b_ref, o_ref, lf_ref, *, n_qkv, n_gate, col_chunk):
    h = _rms(x_ref[...], g_ref[...]).astype(BF16)
    for c in range(n_qkv // col_chunk):
        cs = slice(c * col_chunk, (c + 1) * col_chunk)
        o_ref[:, cs] = _dot(h, w_ref[:, cs]).astype(BF16)
    tail = _dot(h, w_ref[:, n_qkv:])
    blocks = [tail[:, c * LANES:(c + 1) * LANES] for c in range(tail.shape[1] // LANES)]
    rolled = [pltpu.roll(b, LANES - n_gate, 1) for b in blocks]
    keep = lax.broadcasted_iota(jnp.int32, (1, LANES), 1) < LANES - n_gate
    for c in range(len(blocks) - 1):
        o_ref[:, n_qkv + c * LANES:n_qkv + (c + 1) * LANES] = jnp.where(
            keep, rolled[c], rolled[c + 1]).astype(BF16)
    z = blocks[0] + b_ref[...]
    lf_ref[...] = jnp.minimum(z, 0.0) - jnp.log1p(jnp.exp(-jnp.abs(z)))


def _proj_a(x, g, w, b_pad, n_qkv, n_gate):
    t, d = x.shape
    n_all = w.shape[1]
    n_main = n_all - LANES
    return pl.pallas_call(
        functools.partial(_proj_a_kernel, n_qkv=n_qkv, n_gate=n_gate, col_chunk=768),
        grid=(t // ROW_TILE,),
        in_specs=[
            pl.BlockSpec((ROW_TILE, d), lambda i: (i, 0)),
            pl.BlockSpec((1, d), lambda i: (0, 0)),
            pl.BlockSpec((d, n_all), lambda i: (0, 0)),
            pl.BlockSpec((1, LANES), lambda i: (0, 0)),
        ],
        out_specs=[
            pl.BlockSpec((ROW_TILE, n_main), lambda i: (i, 0)),
            pl.BlockSpec((ROW_TILE, LANES), lambda i: (i, 0)),
        ],
        out_shape=[
            jax.ShapeDtypeStruct((t, n_main), BF16),
            jax.ShapeDtypeStruct((t, LANES), F32),
        ],
        compiler_params=_params("parallel"),
        name="proj_a",
    )(x, g, w, b_pad)


def _cumsum_kernel(lf_ref, ct_ref):
    x = lf_ref[...]
    s = x.shape[0]
    row = lax.broadcasted_iota(jnp.int32, x.shape, 0)
    sh = 1
    while sh < s:
        x = x + jnp.where(row >= sh, pltpu.roll(x, sh, 0), 0.0)
        sh *= 2
    ct_ref[0] = x.T


def _cumsum(lf, batch, seq):
    return pl.pallas_call(
        _cumsum_kernel,
        grid=(batch,),
        in_specs=[pl.BlockSpec((seq, LANES), lambda b: (b, 0))],
        out_specs=pl.BlockSpec((1, LANES, seq), lambda b: (b, 0, 0)),
        out_shape=jax.ShapeDtypeStruct((batch, LANES, seq), F32),
        compiler_params=_params("parallel"),
        name="cumsum",
    )(lf)


def _softmax_rows(s_ref, p_ref, m_ref, hi, tq, col_bias, keep_fn):
    lo = hi - tq
    shape = (ROW_CHUNK, LANES)

    def visibility(r, c):
        r0, c0 = r * ROW_CHUNK, c * LANES - lo
        if c0 < 0 or keep_fn(r0, c0 + LANES - 1):
            return "all"
        if not keep_fn(r0 + ROW_CHUNK - 1, c0):
            return "none"
        ri = lax.broadcasted_iota(jnp.int32, shape, 0) + r0
        ci = lax.broadcasted_iota(jnp.int32, shape, 1) + c0
        return keep_fn(ri, ci)

    def load(r, c, vis):
        t = s_ref[r * ROW_CHUNK:(r + 1) * ROW_CHUNK, c * LANES:(c + 1) * LANES]
        if col_bias is not None:
            t = t - col_bias[:, c * LANES:(c + 1) * LANES]
        return t if isinstance(vis, str) else jnp.where(vis, t, NEG)

    for r in range(tq // ROW_CHUNK):
        m_acc = None
        for c in range(hi // LANES):
            vis = visibility(r, c)
            if isinstance(vis, str) and vis == "none":
                continue
            t = load(r, c, vis)
            m_acc = t if m_acc is None else jnp.maximum(m_acc, t)
        m_ref[r * ROW_CHUNK:(r + 1) * ROW_CHUNK, :] = jnp.broadcast_to(
            jnp.max(m_acc, axis=1, keepdims=True), shape)
    for r in range(tq // ROW_CHUNK):
        rows = slice(r * ROW_CHUNK, (r + 1) * ROW_CHUNK)
        m = m_ref[rows, :]
        for c in range(hi // LANES):
            cols = slice(c * LANES, (c + 1) * LANES)
            vis = visibility(r, c)
            if isinstance(vis, str) and vis == "none":
                p_ref[rows, cols] = jnp.zeros(shape, BF16)
            else:
                p_ref[rows, cols] = jnp.exp((load(r, c, vis) - m).astype(BF16))


def _attention_sweep(q_ref, k_ref, v_aug, scratch, tq, col_bias_fn, keep_fn, emit):
    seq = q_ref.shape[0]
    n_items = 2 * (seq // tq)
    n_slots = scratch[0].shape[0]
    qh, res = {}, {}

    def bufs(n):
        return [sc.at[n % n_slots] for sc in scratch]

    def score(n):
        qi, hh = divmod(n, 2)
        hi = (qi + 1) * tq
        if hh == 0:
            qh[qi] = _split_heads(q_ref[qi * tq:hi, :])
        bufs(n)[0][:, :hi] = _nt_dot(qh[qi][hh], k_ref[:hi, :])

    def softmax(n):
        qi, hh = divmod(n, 2)
        hi = (qi + 1) * tq
        bias = None if col_bias_fn is None else col_bias_fn(hh, hi)
        _softmax_rows(*bufs(n), hi, tq, bias, keep_fn)

    def values(n):
        qi, hh = divmod(n, 2)
        hi = (qi + 1) * tq
        res[hh] = _dot(bufs(n)[1][:, :hi], v_aug(hh)[:hi, :])
        if hh == 1:
            emit(qi, res[0], res[1])

    score(0)
    for n in range(n_items):
        if n + 1 < n_items:
            score(n + 1)
        softmax(n)
        if n:
            values(n - 1)
    values(n_items - 1)


def _split_heads(q):
    low = lax.broadcasted_iota(jnp.int32, (1, LANES), 1) < HEAD_DIM
    zero = jnp.zeros_like(q)
    return jnp.where(low, q, zero), jnp.where(low, zero, q)


def _fox_kernel(q_ref, k_ref, v_ref, ct_ref, o_ref, s_sc, p_sc, m_sc, va_sc, *, tq):
    sub = (2 * pl.program_id(1)) & 7
    low = lax.broadcasted_iota(jnp.int32, (1, LANES), 1) < HEAD_DIM
    v = v_ref[...]
    one = jnp.ones_like(v)
    va_sc[0] = jnp.where(low, v, one)
    va_sc[1] = jnp.where(low, one, v)

    def keep_fn(row, col):
        return col <= row

    def col_bias(hh, hi):
        return ct_ref[0, pl.ds(sub + hh, 1), :hi]

    def emit(qi, r0, r1):
        num = jnp.where(low, r0, r1)
        den = pltpu.roll(jnp.where(low, r1, r0), HEAD_DIM, 1)
        o_ref[qi * tq:(qi + 1) * tq, :] = (num / den).astype(BF16)

    _attention_sweep(q_ref, k_ref, lambda hh: va_sc.at[hh], (s_sc, p_sc, m_sc), tq, col_bias,
                     keep_fn, emit)


def _attention_scratch(tq, seq):
    n = 2 * ATT_SLOTS
    return [pltpu.VMEM((n, tq, seq), F32), pltpu.VMEM((n, tq, seq), BF16),
            pltpu.VMEM((n, tq, LANES), F32)]


def _fox_attention(qkvm, cum_t, batch, seq):
    t = batch * seq
    npair = FOX_HEADS // 2
    return pl.pallas_call(
        functools.partial(_fox_kernel, tq=ATT_TILE),
        grid=(batch, npair),
        in_specs=[
            pl.BlockSpec((seq, LANES), lambda b, p: (b, p)),
            pl.BlockSpec((seq, LANES), lambda b, p: (b, npair + p)),
            pl.BlockSpec((seq, LANES), lambda b, p: (b, 2 * npair + p)),
            pl.BlockSpec((1, 8, seq), lambda b, p: (b, p // 4, 0)),
        ],
        out_specs=pl.BlockSpec((seq, LANES), lambda b, p: (b, p)),
        out_shape=jax.ShapeDtypeStruct((t, npair * LANES), BF16),
        scratch_shapes=_attention_scratch(ATT_TILE, seq) + [pltpu.VMEM((2, seq, LANES), BF16)],
        compiler_params=_params("parallel", "parallel"),
        name="fox_attention",
    )(qkvm, qkvm, qkvm, cum_t)


def _mem_kernel(q_ref, mk_ref, mv_ref, o_ref):
    q = q_ref[...]
    mk = mk_ref[...]
    mv = mv_ref[...]
    lane = lax.broadcasted_iota(jnp.int32, (1, MEM_WIDTH), 1)
    zero = jnp.zeros_like(q)
    out = jnp.zeros(q.shape, F32)
    for h in range(MEM_HEADS):
        hm = (lane >= h * HEAD_DIM) & (lane < (h + 1) * HEAD_DIM)
        s = _nt_dot(jnp.where(hm, q, zero), mk)
        p = jnp.exp(s - jnp.max(s, axis=1, keepdims=True))
        l = jnp.sum(p, axis=1, keepdims=True)
        out = jnp.where(hm, _dot(p.astype(BF16), mv) / l, out)
    o_ref[...] = out.astype(BF16)


def _mem_attention(q_arr, q_col_block, mkv, batch, seq, mem_tokens):
    nq = seq // MEM_Q_TILE
    return pl.pallas_call(
        _mem_kernel,
        grid=(batch, nq),
        in_specs=[
            pl.BlockSpec((MEM_Q_TILE, MEM_WIDTH), lambda b, i: (b * nq + i, q_col_block)),
            pl.BlockSpec((mem_tokens, MEM_WIDTH), lambda b, i: (b, 0)),
            pl.BlockSpec((mem_tokens, MEM_WIDTH), lambda b, i: (b, 1)),
        ],
        out_specs=pl.BlockSpec((MEM_Q_TILE, MEM_WIDTH), lambda b, i: (b * nq + i, 0)),
        out_shape=jax.ShapeDtypeStruct((batch * seq, MEM_WIDTH), BF16),
        compiler_params=_params("parallel", "parallel"),
        name="mem_attention",
    )(q_arr, mkv, mkv)


def _norm_matmul_kernel(x_ref, g_ref, w_ref, o_ref):
    h = _rms(x_ref[...], g_ref[...]).astype(BF16)
    o_ref[...] = _dot(h, w_ref[...]).astype(o_ref.dtype)


def _norm_matmul(x, g, w, name):
    t, d = x.shape
    n = w.shape[1]
    return pl.pallas_call(
        _norm_matmul_kernel,
        grid=(t // ROW_TILE,),
        in_specs=[
            pl.BlockSpec((ROW_TILE, d), lambda i: (i, 0)),
            pl.BlockSpec((1, d), lambda i: (0, 0)),
            pl.BlockSpec((d, n), lambda i: (0, 0)),
        ],
        out_specs=pl.BlockSpec((ROW_TILE, n), lambda i: (i, 0)),
        out_shape=jax.ShapeDtypeStruct((t, n), BF16),
        compiler_params=_params("parallel"),
        name=name,
    )(x, g, w)


def _mixed_residual(x_ref, ys_ref, ym_ref, wo_ref):
    sw = ys_ref.shape[1]
    return x_ref[...] + _dot(ys_ref[...], wo_ref[:sw, :]) + _dot(ym_ref[...], wo_ref[sw:, :])


def _swiglu_hidden(g, u):
    return (g * jax.nn.sigmoid(g) * u).astype(BF16)


def _ffn_kernel(x_ref, ys_ref, ym_ref, wo_ref, g_ref, wgu_ref, wd_ref, o_ref, h_sc, *, col):
    ff = wd_ref.shape[0]
    x = _mixed_residual(x_ref, ys_ref, ym_ref, wo_ref)
    h_sc[...] = _rms(x, g_ref[...]).astype(BF16)
    o_ref[...] = x
    for c in range(ff // col):
        h = h_sc[...]
        gate = _dot(h, wgu_ref[:, c * col:(c + 1) * col])
        up = _dot(h, wgu_ref[:, ff + c * col:ff + (c + 1) * col])
        o_ref[...] += _dot(_swiglu_hidden(gate, up), wd_ref[c * col:(c + 1) * col, :])


def _dense_ffn(x, y_self, y_mem, w_out, g, w_gate_up, w_down):
    t, d = x.shape
    tm = FFN_ROW_TILE
    row = lambda i: (i, 0)
    resident = dict(index_map=lambda i: (0, 0), pipeline_mode=pl.Buffered(1))
    return pl.pallas_call(
        functools.partial(_ffn_kernel, col=FFN_COL_TILE),
        grid=(t // tm,),
        in_specs=[
            pl.BlockSpec((tm, d), row),
            pl.BlockSpec((tm, y_self.shape[1]), row),
            pl.BlockSpec((tm, y_mem.shape[1]), row),
            pl.BlockSpec(w_out.shape, **resident),
            pl.BlockSpec((1, d), lambda i: (0, 0)),
            pl.BlockSpec(w_gate_up.shape, **resident),
            pl.BlockSpec(w_down.shape, **resident),
        ],
        out_specs=pl.BlockSpec((tm, d), row),
        out_shape=jax.ShapeDtypeStruct((t, d), F32),
        scratch_shapes=[pltpu.VMEM((tm, d), BF16)],
        compiler_params=pltpu.CompilerParams(dimension_semantics=("parallel",),
                                             vmem_limit_bytes=FFN_VMEM_LIMIT),
        name="dense_ffn",
    )(x, y_self, y_mem, w_out, g, w_gate_up, w_down)


def _rope_kernel(pos_ref, inv_ref, cos_ref, sin_ref):
    pos = pos_ref[0].astype(F32)
    ang = (inv_ref[...] * pos).T
    lane = lax.broadcasted_iota(jnp.int32, ang.shape, 1)
    rot = (lane & (HEAD_DIM - 1)) < ROT_DIM
    first = (lane & (ROT_DIM - 1)) < (ROT_DIM // 2)
    sn = jnp.sin(ang)
    cos_ref[...] = jnp.where(rot, jnp.cos(ang), 1.0)
    sin_ref[...] = jnp.where(rot, jnp.where(first, -sn, sn), 0.0)


def _rope_tables(pos_rows, inv_col):
    steps = pos_rows.shape[0]
    t = steps * ROW_TILE
    return pl.pallas_call(
        _rope_kernel,
        grid=(steps,),
        in_specs=[
            pl.BlockSpec((1, 1, ROW_TILE), lambda i: (i, 0, 0)),
            pl.BlockSpec((LANES, 1), lambda i: (0, 0)),
        ],
        out_specs=[pl.BlockSpec((ROW_TILE, LANES), lambda i: (i, 0))] * 2,
        out_shape=[jax.ShapeDtypeStruct((t, LANES), F32)] * 2,
        compiler_params=_params("parallel"),
        name="rope_tables",
    )(pos_rows, inv_col)


def _proj_rope_kernel(x_ref, gk_ref, gq_ref, wk_ref, wq_ref, cos_ref, sin_ref, ok_ref, oq_ref, *,
                      n_rope, col_chunk):
    x = x_ref[...]
    xn = x * lax.rsqrt(jnp.mean(x * x, axis=-1, keepdims=True) + EPS)
    cos = cos_ref[...]
    sin = sin_ref[...]
    lane = lax.broadcasted_iota(jnp.int32, (1, LANES), 1)
    first = (lane & (ROT_DIM - 1)) < (ROT_DIM // 2)
    half = ROT_DIM // 2
    for g_ref, w_ref, o_ref in ((gk_ref, wk_ref, ok_ref), (gq_ref, wq_ref, oq_ref)):
        h = (xn * g_ref[...]).astype(BF16)
        for c in range(w_ref.shape[1] // col_chunk):
            a = _dot(h, w_ref[:, c * col_chunk:(c + 1) * col_chunk])
            for s in range(col_chunk // LANES):
                col = c * col_chunk + s * LANES
                blk = a[:, s * LANES:(s + 1) * LANES]
                if col < n_rope:
                    partner = jnp.where(first, pltpu.roll(blk, LANES - half, 1),
                                        pltpu.roll(blk, half, 1))
                    blk = blk * cos + partner * sin
                o_ref[:, col:col + LANES] = blk.astype(BF16)


def _proj_rope(x, g_kv, g_q, w_kv, w_q, cos, sin, n_rope):
    t, d = x.shape
    nk, nq = w_kv.shape[1], w_q.shape[1]
    row = lambda i: (i, 0)
    fixed = lambda i: (0, 0)
    return pl.pallas_call(
        functools.partial(_proj_rope_kernel, n_rope=n_rope, col_chunk=256),
        grid=(t // ROW_TILE,),
        in_specs=[
            pl.BlockSpec((ROW_TILE, d), row),
            pl.BlockSpec((1, d), fixed),
            pl.BlockSpec((1, d), fixed),
            pl.BlockSpec((d, nk), fixed),
            pl.BlockSpec((d, nq), fixed),
            pl.BlockSpec((ROW_TILE, LANES), row),
            pl.BlockSpec((ROW_TILE, LANES), row),
        ],
        out_specs=[pl.BlockSpec((ROW_TILE, nk), row), pl.BlockSpec((ROW_TILE, nq), row)],
        out_shape=[jax.ShapeDtypeStruct((t, nk), BF16), jax.ShapeDtypeStruct((t, nq), BF16)],
        compiler_params=_params("parallel"),
        name="proj_kv_q",
    )(x, g_kv, g_q, w_kv, w_q, cos, sin)


def _diff_kernel(lam_ref, sg_ref, q_ref, k_ref, v_ref, o_ref, s_sc, p_sc, m_sc, va_sc, *, tq,
                 lam_init):
    lp = lam_ref[...]
    lam = (jnp.exp(jnp.sum(lp[0:1] * lp[1:2], axis=1, keepdims=True))
           - jnp.exp(jnp.sum(lp[2:3] * lp[3:4], axis=1, keepdims=True)) + lam_init)
    va_sc[:, :LANES] = v_ref[...]
    va_sc[:, LANES:] = jnp.ones(v_ref.shape, BF16)

    def keep_fn(row, col):
        return (col >> CHUNK_SHIFT) <= (row >> CHUNK_SHIFT)

    def emit(qi, r1, r2):
        o = r1[:, :LANES] / r1[:, LANES:] - lam * (r2[:, :LANES] / r2[:, LANES:])
        o = _rms(o, sg_ref[...]) * (1.0 - lam_init)
        o_ref[qi * tq:(qi + 1) * tq, :] = o.astype(BF16)

    _attention_sweep(q_ref, k_ref, lambda hh: va_sc, (s_sc, p_sc, m_sc), tq, None, keep_fn, emit)


def _diff_attention(lam_rows, subln_g, qb, kv, batch, seq, lam_init):
    t = batch * seq
    return pl.pallas_call(
        functools.partial(_diff_kernel, tq=ATT_TILE, lam_init=lam_init),
        grid=(batch, DIFF_HEADS),
        in_specs=[
            pl.BlockSpec((8, LANES), lambda b, h: (0, 0)),
            pl.BlockSpec((1, LANES), lambda b, h: (0, 0)),
            pl.BlockSpec((seq, LANES), lambda b, h: (b, h)),
            pl.BlockSpec((seq, LANES), lambda b, h: (b, h)),
            pl.BlockSpec((seq, LANES), lambda b, h: (b, DIFF_HEADS + h)),
        ],
        out_specs=pl.BlockSpec((seq, LANES), lambda b, h: (b, h)),
        out_shape=jax.ShapeDtypeStruct((t, DIFF_HEADS * LANES), BF16),
        scratch_shapes=_attention_scratch(ATT_TILE, seq) + [pltpu.VMEM((seq, 2 * LANES), BF16)],
        compiler_params=_params("parallel", "parallel"),
        name="diff_attention",
    )(lam_rows, subln_g, qb, kv, kv)


def _moe_pre_kernel(x_ref, ys_ref, ym_ref, wo_ref, g_ref, wr_ref, tri_ref, x3_ref, hp_ref, route_ref,
                    route_t_ref, cnt_ref, run_sc):
    @pl.when(pl.program_id(0) == 0)
    def _():
        run_sc[...] = jnp.zeros_like(run_sc)

    x = _mixed_residual(x_ref, ys_ref, ym_ref, wo_ref)
    x3_ref[...] = x
    hf = _rms(x, g_ref[...])
    hp_ref[...] = hf

    h_hi = hf.astype(BF16)
    h_lo = (hf - h_hi.astype(F32)).astype(BF16)
    logits = _dot(jnp.concatenate([h_hi, h_lo, h_hi], axis=1), wr_ref[...])
    lane = lax.broadcasted_iota(jnp.int32, logits.shape, 1)
    lanef = lane.astype(F32)
    lg = jnp.where(lane < N_EXPERTS, logits, NEG)
    v1 = jnp.max(lg, axis=1, keepdims=True)
    i1 = jnp.min(jnp.where(lg == v1, lanef, float(LANES)), axis=1, keepdims=True)
    lg2 = jnp.where(lanef == i1, NEG, lg)
    v2 = jnp.max(lg2, axis=1, keepdims=True)
    i2 = jnp.min(jnp.where(lg2 == v2, lanef, float(LANES)), axis=1, keepdims=True)
    e = jnp.exp(v2 - v1)
    g1 = 1.0 / (1.0 + e)
    g2 = e / (1.0 + e)

    oh1 = lanef == i1
    oh2 = lanef == i2
    oh = jnp.where(oh1 | oh2, 1.0, 0.0)
    before = _dot(tri_ref[...], oh.astype(BF16)) + run_sc[...]
    r1 = jnp.sum(jnp.where(oh1, before, 0.0), axis=1, keepdims=True)
    r2 = jnp.sum(jnp.where(oh2, before, 0.0), axis=1, keepdims=True)
    run_sc[...] += jnp.sum(oh, axis=0, keepdims=True)
    cnt_ref[...] = run_sc[...]

    route = jnp.where(lane == 0, i1, 0.0)
    for ln, val in ((1, i2), (2, r1), (3, r2), (4, g1), (5, g2)):
        route = jnp.where(lane == ln, val, route)
    route_ref[...] = route
    route_t_ref[...] = route.T[:SUBLANES, :]


def _moe_pre(x, y_self, y_mem, w_out, g, w_router_pad, tri):
    t, d = x.shape
    tm = tri.shape[0]
    row = lambda i: (i, 0)
    fixed = lambda i: (0, 0)
    return pl.pallas_call(
        _moe_pre_kernel,
        grid=(t // tm,),
        in_specs=[
            pl.BlockSpec((tm, d), row),
            pl.BlockSpec((tm, y_self.shape[1]), row),
            pl.BlockSpec((tm, y_mem.shape[1]), row),
            pl.BlockSpec(w_out.shape, fixed),
            pl.BlockSpec((1, d), fixed),
            pl.BlockSpec((3 * d, LANES), fixed),
            pl.BlockSpec((tm, tm), fixed),
        ],
        out_specs=[
            pl.BlockSpec((tm, d), row),
            pl.BlockSpec((tm, d), row),
            pl.BlockSpec((tm, LANES), row),
            pl.BlockSpec((SUBLANES, tm), lambda i: (0, i)),
            pl.BlockSpec((1, LANES), fixed),
        ],
        out_shape=[
            jax.ShapeDtypeStruct((t, d), F32),
            jax.ShapeDtypeStruct((t, d), F32),
            jax.ShapeDtypeStruct((t, LANES), F32),
            jax.ShapeDtypeStruct((SUBLANES, t), F32),
            jax.ShapeDtypeStruct((1, LANES), F32),
        ],
        scratch_shapes=[pltpu.VMEM((1, LANES), F32)],
        compiler_params=_params("arbitrary"),
        name="moe_pre",
    )(x, y_self, y_mem, w_out, g, w_router_pad, tri)


def _slot_source_kernel(idx_ref, src_ref, *, n_tokens, n_tiles, stride):
    def fill(tile, c):
        for r in range(stride):
            src_ref[tile * stride + r] = r
        return c

    lax.fori_loop(0, n_tiles, fill, 0)

    def body(tok, c):
        for k in range(2):
            src_ref[idx_ref[k * n_tokens + tok]] = tok
        return c

    lax.fori_loop(0, n_tokens, body, 0, unroll=8)


def _slot_sources(pos, n_tiles, n_tokens):
    tm = MOE_ROW_TILE
    assert MOE_SRC_STRIDE <= n_tokens
    table_idx = (pos // tm) * MOE_SRC_STRIDE + pos % tm
    smem = pl.BlockSpec(memory_space=pltpu.SMEM)
    return pl.pallas_call(
        functools.partial(_slot_source_kernel, n_tokens=n_tokens, n_tiles=n_tiles,
                          stride=MOE_SRC_STRIDE),
        in_specs=[smem],
        out_specs=smem,
        out_shape=jax.ShapeDtypeStruct((n_tiles * MOE_SRC_STRIDE,), jnp.int32),
        name="moe_slot_sources",
    )(table_idx)


def _moe_kernel(te_ref, na_ref, nv_ref, src_ref, h_ref, wg_ref, wu_ref, wd_ref, o_ref, xg_sc, xb_sc,
                gsem, *, chunk, stride):
    del te_ref
    i = pl.program_id(0)
    f = pl.program_id(1)
    tm = o_ref.shape[0]
    n_rows_copied = pl.num_programs(1) * chunk
    slot = i & 1

    def row_copy(tile, sl, r):
        src = src_ref[tile * stride + r]
        return pltpu.make_async_copy(h_ref.at[pl.ds(src, 1)], xg_sc.at[sl, pl.ds(r, 1)],
                                     gsem.at[sl])

    def wait_tile(tile, sl):
        def body(r, c):
            row_copy(tile, sl, r).wait()
            return c
        lax.fori_loop(0, n_rows_copied, body, 0, unroll=8)

    @pl.when(f == 0)
    def _():
        o_ref[...] = jnp.zeros_like(o_ref)

    @pl.when((i == 0) & (f == 0))
    def _():
        def body(r, c):
            row_copy(0, 0, r).start()
            return c
        lax.fori_loop(0, n_rows_copied, body, 0, unroll=8)

    @pl.when((i == na_ref[0]) & (f == 0))
    def _():
        wait_tile(i, slot)

    @pl.when(i < na_ref[0])
    def _():
        @pl.when(f == 0)
        def _():
            wait_tile(i, slot)
            xb_sc[...] = xg_sc[slot, :tm, :].astype(BF16)

        def prefetch(lo, hi):
            for r in range(lo, hi):
                row_copy(i + 1, 1 - slot, f * chunk + r).start()

        def expert_rows(n_rows):
            xb = xb_sc[:n_rows, :]
            g = _dot(xb, wg_ref[0].astype(BF16))
            prefetch(0, chunk // 2)
            u = _dot(xb, wu_ref[0].astype(BF16))
            prefetch(chunk // 2, chunk)
            o_ref[:n_rows, :] += _dot(_swiglu_hidden(g, u), wd_ref[0].astype(BF16))

        @pl.when(nv_ref[i] > tm // 2)
        def _():
            expert_rows(tm)

        @pl.when(nv_ref[i] <= tm // 2)
        def _():
            expert_rows(tm // 2)


def _moe_experts(tile_expert, n_active, n_valid, src, h_rows, w_gate_up, w_down, n_tiles):
    d = w_down.shape[2]
    ff = w_down.shape[1]
    ct = MOE_COL_TILE
    nf = ff // ct
    tm = MOE_ROW_TILE
    chunk = -(-tm // nf)
    assert nf * chunk <= MOE_SRC_STRIDE

    def col(i, f, na):
        return jnp.where(i < na[0], f, nf - 1)

    return pl.pallas_call(
        functools.partial(_moe_kernel, chunk=chunk, stride=MOE_SRC_STRIDE),
        grid_spec=pltpu.PrefetchScalarGridSpec(
            num_scalar_prefetch=4,
            grid=(n_tiles, nf),
            in_specs=[
                pl.BlockSpec(memory_space=pl.ANY),
                pl.BlockSpec((1, d, ct), lambda i, f, te, na, *_: (te[i], 0, col(i, f, na))),
                pl.BlockSpec((1, d, ct), lambda i, f, te, na, *_: (te[i], 0, nf + col(i, f, na))),
                pl.BlockSpec((1, ct, d), lambda i, f, te, na, *_: (te[i], col(i, f, na), 0)),
            ],
            out_specs=pl.BlockSpec((tm, d), lambda i, f, *_: (i, 0)),
            scratch_shapes=[pltpu.VMEM((2, MOE_SRC_STRIDE, d), F32), pltpu.VMEM((tm, d), BF16),
                            pltpu.SemaphoreType.DMA((2,))],
        ),
        out_shape=jax.ShapeDtypeStruct((n_tiles * tm, d), F32),
        compiler_params=_params("arbitrary", "arbitrary"),
        name="moe_experts",
    )(tile_expert, n_active, n_valid, src, h_rows, w_gate_up, w_gate_up, w_down)


def _combine_kernel(hi_ref, lo_ref, x_ref, route_ref, g_ref, y_ref, o_ref, buf, sem, *, tile):
    i = pl.program_id(0)
    slot = i & 1
    groups = tile // SUBLANES

    n_tokens = hi_ref.shape[0] // 2

    def row_copy(step, sl, j, u, k):
        n = k * n_tokens + step * tile + j * SUBLANES + u
        return pltpu.make_async_copy(y_ref.at[hi_ref[n], pl.ds(lo_ref[n], 1)],
                                     buf.at[sl, k, j, pl.ds(u, 1)], sem.at[sl])

    def gather(step, sl):
        def start(j, c):
            for u in range(SUBLANES):
                row_copy(step, sl, j, u, 0).start(priority=0)
                row_copy(step, sl, j, u, 1).start(priority=1)
            return c
        lax.fori_loop(0, groups, start, 0)

    @pl.when(i == 0)
    def _():
        gather(0, 0)

    @pl.when(i + 1 < pl.num_programs(0))
    def _():
        gather(i + 1, 1 - slot)

    def wait(j, c):
        for u in range(SUBLANES):
            row_copy(i, slot, j, u, 0).wait()
            row_copy(i, slot, j, u, 1).wait()
        return c

    lax.fori_loop(0, groups, wait, 0)
    route = route_ref[...]
    d = x_ref.shape[1]
    y0 = buf[slot, 0].reshape(tile, d)
    y1 = buf[slot, 1].reshape(tile, d)
    y = x_ref[...] + route[:, 4:5] * y0 + route[:, 5:6] * y1
    o_ref[...] = _rms(y, g_ref[...])


def _combine(pos_hi, pos_lo, x, route, g, y3):
    t, d = x.shape
    groups = GATHER_TILE // SUBLANES
    return pl.pallas_call(
        functools.partial(_combine_kernel, tile=GATHER_TILE),
        grid_spec=pltpu.PrefetchScalarGridSpec(
            num_scalar_prefetch=2,
            grid=(t // GATHER_TILE,),
            in_specs=[
                pl.BlockSpec((GATHER_TILE, d), lambda i, *_: (i, 0)),
                pl.BlockSpec((GATHER_TILE, LANES), lambda i, *_: (i, 0)),
                pl.BlockSpec((1, d), lambda i, *_: (0, 0)),
                pl.BlockSpec(memory_space=pl.ANY),
            ],
            out_specs=pl.BlockSpec((GATHER_TILE, d), lambda i, *_: (i, 0)),
            scratch_shapes=[pltpu.VMEM((2, 2, groups, SUBLANES, d), F32),
                            pltpu.SemaphoreType.DMA((2,))],
        ),
        out_shape=jax.ShapeDtypeStruct((t, d), F32),
        compiler_params=_params("arbitrary"),
        name="moe_combine",
    )(pos_hi, pos_lo, x, route, g, y3)


def _row(v):
    return v.reshape(1, -1).astype(F32)


def _pad_lanes(v):
    return jnp.pad(v.astype(F32), (0, LANES - v.shape[0])).reshape(1, LANES)


def kernel(x, mem, positions, mix_norm_g, ffn_norm_g, mem_norm_g, w_mem_kv, w_out, w_in_a, b_forget, w_q_b, lambda_q1, lambda_k1, lambda_q2, lambda_k2, subln_g, kv_norm_g, w_kv_shared, w_gate_up_dense, w_down_dense, w_router, w_gate_up_moe, w_down_moe, final_norm_g):
    batch, seq, d = x.shape
    mem_tokens = mem.shape[1]
    t = batch * seq
    assert w_in_a.shape[0] == 1 and w_q_b.shape[0] == 1 and w_out.shape[0] == 2
    fox_w = FOX_HEADS * HEAD_DIM
    diff_w = DIFF_HEADS * 2 * HEAD_DIM
    scale = HEAD_DIM ** -0.5

    xf = x.reshape(t, d)
    memf = mem.reshape(batch * mem_tokens, d)

    wa = w_in_a[0]
    n_a = wa.shape[1]
    n_a_pad = -(-n_a // LANES) * LANES
    col = jnp.arange(n_a_pad)
    col_scale = jnp.where((col < fox_w) | (col >= 3 * fox_w + FOX_HEADS), scale, 1.0).astype(F32)
    w_a = (jnp.pad(wa, ((0, 0), (0, n_a_pad - n_a))) * col_scale).astype(BF16)
    w_b = (w_q_b[0] * scale).astype(BF16)
    w_kv = w_kv_shared.astype(BF16)
    w_o = w_out.astype(BF16)
    w_mkv = w_mem_kv.astype(BF16)
    w_gu_d = w_gate_up_dense[0].astype(BF16)
    w_dn_d = w_down_dense[0].astype(BF16)
    w_gu_m = w_gate_up_moe[0]
    w_dn_m = w_down_moe[0]
    w_r = jnp.pad(w_router[0].astype(F32), ((0, 0), (0, LANES - N_EXPERTS)))
    w_r_hi = w_r.astype(BF16)
    w_r_lo = (w_r - w_r_hi.astype(F32)).astype(BF16)
    w_r = jnp.concatenate([w_r_hi, w_r_hi, w_r_lo], axis=0)

    qkvm, log_f = _proj_a(xf, _row(mix_norm_g[0]), w_a, _pad_lanes(b_forget[0]), 3 * fox_w,
                          FOX_HEADS)
    cum_t = _cumsum(log_f, batch, seq)
    y_self = _fox_attention(qkvm, cum_t, batch, seq)
    mkv0 = _norm_matmul(memf, _row(mem_norm_g[0]), w_mkv[0], "mem_kv0")
    y_mem = _mem_attention(qkvm, 3 * fox_w // MEM_WIDTH, mkv0, batch, seq, mem_tokens)
    x2 = _dense_ffn(xf, y_self, y_mem, w_o[0], _row(ffn_norm_g[0]), w_gu_d, w_dn_d)

    half = ROT_DIM // 2
    inv_freq = ROPE_THETA ** (-(jnp.arange(half, dtype=F32) * 2.0 / ROT_DIM))
    inv_col = jnp.tile(inv_freq, LANES // half).reshape(LANES, 1)
    pos_rows = positions.astype(jnp.int32).reshape(t // ROW_TILE, 1, ROW_TILE)
    cos_t, sin_t = _rope_tables(pos_rows, inv_col)
    kv, qb = _proj_rope(x2, _row(kv_norm_g), _row(mix_norm_g[1]), w_kv, w_b, cos_t, sin_t, diff_w)

    lam_init = 0.8 - 0.6 * math.exp(-0.3 * 1)
    lam_rows = jnp.concatenate(
        [jnp.pad(v[0].astype(F32), (0, LANES - HEAD_DIM)).reshape(1, LANES)
         for v in (lambda_q1, lambda_k1, lambda_q2, lambda_k2)]
        + [jnp.zeros((4, LANES), F32)], axis=0)
    y_self = _diff_attention(lam_rows, _row(subln_g[0]), qb, kv, batch, seq, lam_init)
    mkv1 = _norm_matmul(memf, _row(mem_norm_g[1]), w_mkv[1], "mem_kv1")
    y_mem = _mem_attention(qb, diff_w // MEM_WIDTH, mkv1, batch, seq, mem_tokens)

    tri = jnp.tril(jnp.ones((MOE_PRE_TILE, MOE_PRE_TILE), BF16), -1)
    x3, h_rows, route, route_t, cnt = _moe_pre(x2, y_self, y_mem, w_o[1], _row(ffn_norm_g[1]), w_r, tri)
    tm = MOE_ROW_TILE
    n_tiles = 2 * t // tm + N_EXPERTS
    counts = cnt[0, :N_EXPERTS].astype(jnp.int32)
    tiles_per = (counts + tm - 1) // tm
    tile_end = jnp.cumsum(tiles_per)
    group_off = (tile_end - tiles_per) * tm
    n_active = tile_end[-1:]
    tile_id = jnp.minimum(jnp.arange(n_tiles, dtype=jnp.int32), n_active - 1)
    tile_expert = jnp.sum(tile_id[:, None] >= tile_end[None, :], axis=1).astype(jnp.int32)
    idx = route_t[0:2].astype(jnp.int32)
    rank = route_t[2:4].astype(jnp.int32)
    expert_ids = jnp.arange(N_EXPERTS, dtype=jnp.int32).reshape(N_EXPERTS, 1, 1)
    off = jnp.sum(jnp.where(idx[None] == expert_ids, group_off.reshape(N_EXPERTS, 1, 1), 0), axis=0)
    pos = (off + rank).reshape(2 * t)
    pos_hi, pos_lo = pos // SUBLANES, pos % SUBLANES

    src = _slot_sources(pos, n_tiles, t)
    group_end = group_off + counts
    n_valid = jnp.clip(group_end[tile_expert] - tile_id * tm, 0, tm).astype(jnp.int32)
    y_sorted = _moe_experts(tile_expert, n_active.astype(jnp.int32), n_valid, src, h_rows,
                            w_gu_m, w_dn_m, n_tiles)
    y3 = y_sorted.reshape(n_tiles * tm // SUBLANES, SUBLANES, d)
    out = _combine(pos_hi, pos_lo, x3, route, _row(final_norm_g), y3)
    return out.reshape(batch, seq, d)
```

```python
import functools
import math

import jax
import jax.numpy as jnp
from jax import lax
from jax.experimental import pallas as pl
from jax.experimental.pallas import tpu as pltpu

F32 = jnp.float32
BF16 = jnp.bfloat16

HEAD_DIM = 64
LANES = 128
SUBLANES = 8
CHUNK_SHIFT = 6
FOX_HEADS = 12
DIFF_HEADS = 6
MEM_HEADS = 4
MEM_WIDTH = MEM_HEADS * HEAD_DIM
ROPE_THETA = 500000.0
ROT_DIM = HEAD_DIM // 4
N_EXPERTS = 8
EPS = 1e-5
NEG = -1e30
VMEM_LIMIT = 48 * 1024 * 1024
FFN_VMEM_LIMIT = 56 * 1024 * 1024

ROW_TILE = 1024
FFN_ROW_TILE = 1024
FFN_COL_TILE = 256
ATT_TILE = 256
ROW_CHUNK = 16
ATT_SLOTS = 2
MEM_Q_TILE = 512
MOE_PRE_TILE = 1024
MOE_ROW_TILE = 1024
MOE_COL_TILE = 512
MOE_SRC_STRIDE = 1032
COMBINE_TILE = 512


def _params(*sem):
    return pltpu.CompilerParams(dimension_semantics=sem, vmem_limit_bytes=VMEM_LIMIT)


def _rms(x, g):
    ms = jnp.mean(x * x, axis=-1, keepdims=True)
    return x * lax.rsqrt(ms + EPS) * g


def _nt_dot(a, b):
    return lax.dot_general(a, b, (((1,), (1,)), ((), ())), preferred_element_type=F32)


def _dot(a, b):
    return jnp.dot(a, b, preferred_element_type=F32)


def _proj_a_kernel(x_ref, g_ref, w_ref, b_ref, o_ref, lf_ref, *, n_qkv, n_gate, col_chunk):
    h = _rms(x_ref[...], g_ref[...]).astype(BF16)
    for c in range(n_qkv // col_chunk):
        cs = slice(c * col_chunk, (c + 1) * col_chunk)
        o_ref[:, cs] = _dot(h, w_ref[:, cs]).astype(BF16)
    tail = _dot(h, w_ref[:, n_qkv:])
    blocks = [tail[:, c * LANES:(c + 1) * LANES] for c in range(tail.shape[1] // LANES)]
    rolled = [pltpu.roll(b, LANES - n_gate, 1) for b in blocks]
    keep = lax.broadcasted_iota(jnp.int32, (1, LANES), 1) < LANES - n_gate
    for c in range(len(blocks) - 1):
        o_ref[:, n_qkv + c * LANES:n_qkv + (c + 1) * LANES] = jnp.where(
            keep, rolled[c], rolled[c + 1]).astype(BF16)
    z = blocks[0] + b_ref[...]
    lf_ref[...] = jnp.minimum(z, 0.0) - jnp.log1p(jnp.exp(-jnp.abs(z)))


def _proj_a(x, g, w, b_pad, n_qkv, n_gate):
    t, d = x.shape
    n_all = w.shape[1]
    n_main = n_all - LANES
    return pl.pallas_call(
        functools.partial(_proj_a_kernel, n_qkv=n_qkv, n_gate=n_gate, col_chunk=768),
        grid=(t // ROW_TILE,),
        in_specs=[
            pl.BlockSpec((ROW_TILE, d), lambda i: (i, 0)),
            pl.BlockSpec((1, d), lambda i: (0, 0)),
            pl.BlockSpec((d, n_all), lambda i: (0, 0)),
            pl.BlockSpec((1, LANES), lambda i: (0, 0)),
        ],
        out_specs=[
            pl.BlockSpec((ROW_TILE, n_main), lambda i: (i, 0)),
            pl.BlockSpec((ROW_TILE, LANES), lambda i: (i, 0)),
        ],
        out_shape=[
            jax.ShapeDtypeStruct((t, n_main), BF16),
            jax.ShapeDtypeStruct((t, LANES), F32),
        ],
        compiler_params=_params("parallel"),
        name="proj_a",
    )(x, g, w, b_pad)


def _cumsum_kernel(lf_ref, ct_ref):
    x = lf_ref[...]
    s = x.shape[0]
    row = lax.broadcasted_iota(jnp.int32, x.shape, 0)
    sh = 1
    while sh < s:
        x = x + jnp.where(row >= sh, pltpu.roll(x, sh, 0), 0.0)
        sh *= 2
    ct_ref[0] = x.T


def _cumsum(lf, batch, seq):
    return pl.pallas_call(
        _cumsum_kernel,
        grid=(batch,),
        in_specs=[pl.BlockSpec((seq, LANES), lambda b: (b, 0))],
        out_specs=pl.BlockSpec((1, LANES, seq), lambda b: (b, 0, 0)),
        out_shape=jax.ShapeDtypeStruct((batch, LANES, seq), F32),
        compiler_params=_params("parallel"),
        name="cumsum",
    )(lf)


def _softmax_rows(s_ref, p_ref, m_ref, hi, tq, col_bias, keep_fn):
    lo = hi - tq
    shape = (ROW_CHUNK, LANES)

    def visibility(r, c):
        r0, c0 = r * ROW_CHUNK, c * LANES - lo
        if c0 < 0 or keep_fn(r0, c0 + LANES - 1):
            return "all"
        if not keep_fn(r0 + ROW_CHUNK - 1, c0):
            return "none"
        ri = lax.broadcasted_iota(jnp.int32, shape, 0) + r0
        ci = lax.broadcasted_iota(jnp.int32, shape, 1) + c0
        return keep_fn(ri, ci)

    def load(r, c, vis):
        t = s_ref[r * ROW_CHUNK:(r + 1) * ROW_CHUNK, c * LANES:(c + 1) * LANES]
        if col_bias is not None:
            t = t - col_bias[:, c * LANES:(c + 1) * LANES]
        return t if isinstance(vis, str) else jnp.where(vis, t, NEG)

    for r in range(tq // ROW_CHUNK):
        m_acc = None
        for c in range(hi // LANES):
            vis = visibility(r, c)
            if isinstance(vis, str) and vis == "none":
                continue
            t = load(r, c, vis)
            m_acc = t if m_acc is None else jnp.maximum(m_acc, t)
        m_ref[r * ROW_CHUNK:(r + 1) * ROW_CHUNK, :] = jnp.broadcast_to(
            jnp.max(m_acc, axis=1, keepdims=True), shape)
    for r in range(tq // ROW_CHUNK):
        rows = slice(r * ROW_CHUNK, (r + 1) * ROW_CHUNK)
        m = m_ref[rows, :]
        for c in range(hi // LANES):
            cols = slice(c * LANES, (c + 1) * LANES)
            vis = visibility(r, c)
            if isinstance(vis, str) and vis == "none":
                p_ref[rows, cols] = jnp.zeros(shape, BF16)
            else:
                p_ref[rows, cols] = jnp.exp((load(r, c, vis) - m).astype(BF16))


def _attention_sweep(q_ref, k_ref, v_aug, scratch, tq, col_bias_fn, keep_fn, emit):
    seq = q_ref.shape[0]
    n_items = 2 * (seq // tq)
    n_slots = scratch[0].shape[0]
    qh, res = {}, {}

    def bufs(n):
        return [sc.at[n % n_slots] for sc in scratch]

    def score(n):
        qi, hh = divmod(n, 2)
        hi = (qi + 1) * tq
        if hh == 0:
            qh[qi] = _split_heads(q_ref[qi * tq:hi, :])
        bufs(n)[0][:, :hi] = _nt_dot(qh[qi][hh], k_ref[:hi, :])

    def softmax(n):
        qi, hh = divmod(n, 2)
        hi = (qi + 1) * tq
        bias = None if col_bias_fn is None else col_bias_fn(hh, hi)
        _softmax_rows(*bufs(n), hi, tq, bias, keep_fn)

    def values(n):
        qi, hh = divmod(n, 2)
        hi = (qi + 1) * tq
        res[hh] = _dot(bufs(n)[1][:, :hi], v_aug(hh)[:hi, :])
        if hh == 1:
            emit(qi, res[0], res[1])

    score(0)
    for n in range(n_items):
        if n + 1 < n_items:
            score(n + 1)
        softmax(n)
        if n:
            values(n - 1)
    values(n_items - 1)


def _split_heads(q):
    low = lax.broadcasted_iota(jnp.int32, (1, LANES), 1) < HEAD_DIM
    zero = jnp.zeros_like(q)
    return jnp.where(low, q, zero), jnp.where(low, zero, q)


def _fox_kernel(q_ref, k_ref, v_ref, ct_ref, o_ref, s_sc, p_sc, m_sc, va_sc, *, tq):
    sub = (2 * pl.program_id(1)) & 7
    low = lax.broadcasted_iota(jnp.int32, (1, LANES), 1) < HEAD_DIM
    v = v_ref[...]
    one = jnp.ones_like(v)
    va_sc[0] = jnp.where(low, v, one)
    va_sc[1] = jnp.where(low, one, v)

    def keep_fn(row, col):
        return col <= row

    def col_bias(hh, hi):
        return ct_ref[0, pl.ds(sub + hh, 1), :hi]

    def emit(qi, r0, r1):
        num = jnp.where(low, r0, r1)
        den = pltpu.roll(jnp.where(low, r1, r0), HEAD_DIM, 1)
        o_ref[qi * tq:(qi + 1) * tq, :] = (num / den).astype(BF16)

    _attention_sweep(q_ref, k_ref, lambda hh: va_sc.at[hh], (s_sc, p_sc, m_sc), tq, col_bias,
                     keep_fn, emit)


def _attention_scratch(tq, seq):
    n = 2 * ATT_SLOTS
    return [pltpu.VMEM((n, tq, seq), F32), pltpu.VMEM((n, tq, seq), BF16),
            pltpu.VMEM((n, tq, LANES), F32)]


def _fox_attention(qkvm, cum_t, batch, seq):
    t = batch * seq
    npair = FOX_HEADS // 2
    return pl.pallas_call(
        functools.partial(_fox_kernel, tq=ATT_TILE),
        grid=(batch, npair),
        in_specs=[
            pl.BlockSpec((seq, LANES), lambda b, p: (b, p)),
            pl.BlockSpec((seq, LANES), lambda b, p: (b, npair + p)),
            pl.BlockSpec((seq, LANES), lambda b, p: (b, 2 * npair + p)),
            pl.BlockSpec((1, 8, seq), lambda b, p: (b, p // 4, 0)),
        ],
        out_specs=pl.BlockSpec((seq, LANES), lambda b, p: (b, p)),
        out_shape=jax.ShapeDtypeStruct((t, npair * LANES), BF16),
        scratch_shapes=_attention_scratch(ATT_TILE, seq) + [pltpu.VMEM((2, seq, LANES), BF16)],
        compiler_params=_params("parallel", "parallel"),
        name="fox_attention",
    )(qkvm, qkvm, qkvm, cum_t)


def _mem_kernel(q_ref, mk_ref, mv_ref, o_ref):
    q = q_ref[...]
    mk = mk_ref[...]
    mv = mv_ref[...]
    lane = lax.broadcasted_iota(jnp.int32, (1, MEM_WIDTH), 1)
    zero = jnp.zeros_like(q)
    out = jnp.zeros(q.shape, F32)
    for h in range(MEM_HEADS):
        hm = (lane >= h * HEAD_DIM) & (lane < (h + 1) * HEAD_DIM)
        s = _nt_dot(jnp.where(hm, q, zero), mk)
        p = jnp.exp(s - jnp.max(s, axis=1, keepdims=True))
        l = jnp.sum(p, axis=1, keepdims=True)
        out = jnp.where(hm, _dot(p.astype(BF16), mv) / l, out)
    o_ref[...] = out.astype(BF16)


def _mem_attention(q_arr, q_col_block, mkv, batch, seq, mem_tokens):
    nq = seq // MEM_Q_TILE
    return pl.pallas_call(
        _mem_kernel,
        grid=(batch, nq),
        in_specs=[
            pl.BlockSpec((MEM_Q_TILE, MEM_WIDTH), lambda b, i: (b * nq + i, q_col_block)),
            pl.BlockSpec((mem_tokens, MEM_WIDTH), lambda b, i: (b, 0)),
            pl.BlockSpec((mem_tokens, MEM_WIDTH), lambda b, i: (b, 1)),
        ],
        out_specs=pl.BlockSpec((MEM_Q_TILE, MEM_WIDTH), lambda b, i: (b * nq + i, 0)),
        out_shape=jax.ShapeDtypeStruct((batch * seq, MEM_WIDTH), BF16),
        compiler_params=_params("parallel", "parallel"),
        name="mem_attention",
    )(q_arr, mkv, mkv)


def _norm_matmul_kernel(x_ref, g_ref, w_ref, o_ref):
    h = _rms(x_ref[...], g_ref[...]).astype(BF16)
    o_ref[...] = _dot(h, w_ref[...]).astype(o_ref.dtype)


def _norm_matmul(x, g, w, name):
    t, d = x.shape
    n = w.shape[1]
    return pl.pallas_call(
        _norm_matmul_kernel,
        grid=(t // ROW_TILE,),
        in_specs=[
            pl.BlockSpec((ROW_TILE, d), lambda i: (i, 0)),
            pl.BlockSpec((1, d), lambda i: (0, 0)),
            pl.BlockSpec((d, n), lambda i: (0, 0)),
        ],
        out_specs=pl.BlockSpec((ROW_TILE, n), lambda i: (i, 0)),
        out_shape=jax.ShapeDtypeStruct((t, n), BF16),
        compiler_params=_params("parallel"),
        name=name,
    )(x, g, w)


def _mixed_residual(x_ref, ys_ref, ym_ref, wo_ref):
    sw = ys_ref.shape[1]
    return x_ref[...] + _dot(ys_ref[...], wo_ref[:sw, :]) + _dot(ym_ref[...], wo_ref[sw:, :])


def _swiglu_hidden(g, u):
    return (g * jax.nn.sigmoid(g) * u).astype(BF16)


def _ffn_kernel(x_ref, ys_ref, ym_ref, wo_ref, g_ref, wgu_ref, wd_ref, o_ref, h_sc, *, col):
    ff = wd_ref.shape[0]
    x = _mixed_residual(x_ref, ys_ref, ym_ref, wo_ref)
    h_sc[...] = _rms(x, g_ref[...]).astype(BF16)
    o_ref[...] = x
    for c in range(ff // col):
        h = h_sc[...]
        gate = _dot(h, wgu_ref[:, c * col:(c + 1) * col])
        up = _dot(h, wgu_ref[:, ff + c * col:ff + (c + 1) * col])
        o_ref[...] += _dot(_swiglu_hidden(gate, up), wd_ref[c * col:(c + 1) * col, :])


def _dense_ffn(x, y_self, y_mem, w_out, g, w_gate_up, w_down):
    t, d = x.shape
    tm = FFN_ROW_TILE
    row = lambda i: (i, 0)
    resident = dict(index_map=lambda i: (0, 0), pipeline_mode=pl.Buffered(1))
    return pl.pallas_call(
        functools.partial(_ffn_kernel, col=FFN_COL_TILE),
        grid=(t // tm,),
        in_specs=[
            pl.BlockSpec((tm, d), row),
            pl.BlockSpec((tm, y_self.shape[1]), row),
            pl.BlockSpec((tm, y_mem.shape[1]), row),
            pl.BlockSpec(w_out.shape, **resident),
            pl.BlockSpec((1, d), lambda i: (0, 0)),
            pl.BlockSpec(w_gate_up.shape, **resident),
            pl.BlockSpec(w_down.shape, **resident),
        ],
        out_specs=pl.BlockSpec((tm, d), row),
        out_shape=jax.ShapeDtypeStruct((t, d), F32),
        scratch_shapes=[pltpu.VMEM((tm, d), BF16)],
        compiler_params=pltpu.CompilerParams(dimension_semantics=("parallel",),
                                             vmem_limit_bytes=FFN_VMEM_LIMIT),
        name="dense_ffn",
    )(x, y_self, y_mem, w_out, g, w_gate_up, w_down)


def _rope_kernel(pos_ref, inv_ref, cos_ref, sin_ref):
    pos = pos_ref[0].astype(F32)
    ang = (inv_ref[...] * pos).T
    lane = lax.broadcasted_iota(jnp.int32, ang.shape, 1)
    rot = (lane & (HEAD_DIM - 1)) < ROT_DIM
    first = (lane & (ROT_DIM - 1)) < (ROT_DIM // 2)
    sn = jnp.sin(ang)
    cos_ref[...] = jnp.where(rot, jnp.cos(ang), 1.0)
    sin_ref[...] = jnp.where(rot, jnp.where(first, -sn, sn), 0.0)


def _rope_tables(pos_rows, inv_col):
    steps = pos_rows.shape[0]
    t = steps * ROW_TILE
    return pl.pallas_call(
        _rope_kernel,
        grid=(steps,),
        in_specs=[
            pl.BlockSpec((1, 1, ROW_TILE), lambda i: (i, 0, 0)),
            pl.BlockSpec((LANES, 1), lambda i: (0, 0)),
        ],
        out_specs=[pl.BlockSpec((ROW_TILE, LANES), lambda i: (i, 0))] * 2,
        out_shape=[jax.ShapeDtypeStruct((t, LANES), F32)] * 2,
        compiler_params=_params("parallel"),
        name="rope_tables",
    )(pos_rows, inv_col)


def _proj_rope_kernel(x_ref, gk_ref, gq_ref, wk_ref, wq_ref, cos_ref, sin_ref, ok_ref, oq_ref, *,
                      n_rope, col_chunk):
    x = x_ref[...]
    xn = x * lax.rsqrt(jnp.mean(x * x, axis=-1, keepdims=True) + EPS)
    cos = cos_ref[...]
    sin = sin_ref[...]
    lane = lax.broadcasted_iota(jnp.int32, (1, LANES), 1)
    first = (lane & (ROT_DIM - 1)) < (ROT_DIM // 2)
    half = ROT_DIM // 2
    for g_ref, w_ref, o_ref in ((gk_ref, wk_ref, ok_ref), (gq_ref, wq_ref, oq_ref)):
        h = (xn * g_ref[...]).astype(BF16)
        for c in range(w_ref.shape[1] // col_chunk):
            a = _dot(h, w_ref[:, c * col_chunk:(c + 1) * col_chunk])
            for s in range(col_chunk // LANES):
                col = c * col_chunk + s * LANES
                blk = a[:, s * LANES:(s + 1) * LANES]
                if col < n_rope:
                    partner = jnp.where(first, pltpu.roll(blk, LANES - half, 1),
                                        pltpu.roll(blk, half, 1))
                    blk = blk * cos + partner * sin
                o_ref[:, col:col + LANES] = blk.astype(BF16)


def _proj_rope(x, g_kv, g_q, w_kv, w_q, cos, sin, n_rope):
    t, d = x.shape
    nk, nq = w_kv.shape[1], w_q.shape[1]
    row = lambda i: (i, 0)
    fixed = lambda i: (0, 0)
    return pl.pallas_call(
        functools.partial(_proj_rope_kernel, n_rope=n_rope, col_chunk=256),
        grid=(t // ROW_TILE,),
        in_specs=[
            pl.BlockSpec((ROW_TILE, d), row),
            pl.BlockSpec((1, d), fixed),
            pl.BlockSpec((1, d), fixed),
            pl.BlockSpec((d, nk), fixed),
            pl.BlockSpec((d, nq), fixed),
            pl.BlockSpec((ROW_TILE, LANES), row),
            pl.BlockSpec((ROW_TILE, LANES), row),
        ],
        out_specs=[pl.BlockSpec((ROW_TILE, nk), row), pl.BlockSpec((ROW_TILE, nq), row)],
        out_shape=[jax.ShapeDtypeStruct((t, nk), BF16), jax.ShapeDtypeStruct((t, nq), BF16)],
        compiler_params=_params("parallel"),
        name="proj_kv_q",
    )(x, g_kv, g_q, w_kv, w_q, cos, sin)


def _diff_kernel(lam_ref, sg_ref, q_ref, k_ref, v_ref, o_ref, s_sc, p_sc, m_sc, va_sc, *, tq,
                 lam_init):
    lp = lam_ref[...]
    lam = (jnp.exp(jnp.sum(lp[0:1] * lp[1:2], axis=1, keepdims=True))
           - jnp.exp(jnp.sum(lp[2:3] * lp[3:4], axis=1, keepdims=True)) + lam_init)
    va_sc[:, :LANES] = v_ref[...]
    va_sc[:, LANES:] = jnp.ones(v_ref.shape, BF16)

    def keep_fn(row, col):
        return (col >> CHUNK_SHIFT) <= (row >> CHUNK_SHIFT)

    def emit(qi, r1, r2):
        o = r1[:, :LANES] / r1[:, LANES:] - lam * (r2[:, :LANES] / r2[:, LANES:])
        o = _rms(o, sg_ref[...]) * (1.0 - lam_init)
        o_ref[qi * tq:(qi + 1) * tq, :] = o.astype(BF16)

    _attention_sweep(q_ref, k_ref, lambda hh: va_sc, (s_sc, p_sc, m_sc), tq, None, keep_fn, emit)


def _diff_attention(lam_rows, subln_g, qb, kv, batch, seq, lam_init):
    t = batch * seq
    return pl.pallas_call(
        functools.partial(_diff_kernel, tq=ATT_TILE, lam_init=lam_init),
        grid=(batch, DIFF_HEADS),
        in_specs=[
            pl.BlockSpec((8, LANES), lambda b, h: (0, 0)),
            pl.BlockSpec((1, LANES), lambda b, h: (0, 0)),
            pl.BlockSpec((seq, LANES), lambda b, h: (b, h)),
            pl.BlockSpec((seq, LANES), lambda b, h: (b, h)),
            pl.BlockSpec((seq, LANES), lambda b, h: (b, DIFF_HEADS + h)),
        ],
        out_specs=pl.BlockSpec((seq, LANES), lambda b, h: (b, h)),
        out_shape=jax.ShapeDtypeStruct((t, DIFF_HEADS * LANES), BF16),
        scratch_shapes=_attention_scratch(ATT_TILE, seq) + [pltpu.VMEM((seq, 2 * LANES), BF16)],
        compiler_params=_params("parallel", "parallel"),
        name="diff_attention",
    )(lam_rows, subln_g, qb, kv, kv)


def _moe_pre_kernel(x_ref, ys_ref, ym_ref, wo_ref, g_ref, wr_ref, tri_ref, x3_ref, hp_ref, route_ref,
                    route_t_ref, cnt_ref, run_sc):
    @pl.when(pl.program_id(0) == 0)
    def _():
        run_sc[...] = jnp.zeros_like(run_sc)

    x = _mixed_residual(x_ref, ys_ref, ym_ref, wo_ref)
    x3_ref[...] = x
    hf = _rms(x, g_ref[...])
    hp_ref[...] = hf

    h_hi = hf.astype(BF16)
    h_lo = (hf - h_hi.astype(F32)).astype(BF16)
    logits = _dot(jnp.concatenate([h_hi, h_lo, h_hi], axis=1), wr_ref[...])
    lane = lax.broadcasted_iota(jnp.int32, logits.shape, 1)
    lanef = lane.astype(F32)
    lg = jnp.where(lane < N_EXPERTS, logits, NEG)
    v1 = jnp.max(lg, axis=1, keepdims=True)
    i1 = jnp.min(jnp.where(lg == v1, lanef, float(LANES)), axis=1, keepdims=True)
    lg2 = jnp.where(lanef == i1, NEG, lg)
    v2 = jnp.max(lg2, axis=1, keepdims=True)
    i2 = jnp.min(jnp.where(lg2 == v2, lanef, float(LANES)), axis=1, keepdims=True)
    e = jnp.exp(v2 - v1)
    g1 = 1.0 / (1.0 + e)
    g2 = e / (1.0 + e)

    oh1 = lanef == i1
    oh2 = lanef == i2
    oh = jnp.where(oh1 | oh2, 1.0, 0.0)
    before = _dot(tri_ref[...], oh.astype(BF16)) + run_sc[...]
    r1 = jnp.sum(jnp.where(oh1, before, 0.0), axis=1, keepdims=True)
    r2 = jnp.sum(jnp.where(oh2, before, 0.0), axis=1, keepdims=True)
    run_sc[...] += jnp.sum(oh, axis=0, keepdims=True)
    cnt_ref[...] = run_sc[...]

    route = jnp.where(lane == 0, i1, 0.0)
    for ln, val in ((1, i2), (2, r1), (3, r2), (4, g1), (5, g2)):
        route = jnp.where(lane == ln, val, route)
    route_ref[...] = route
    route_t_ref[...] = route.T[:SUBLANES, :]


def _moe_pre(x, y_self, y_mem, w_out, g, w_router_pad, tri):
    t, d = x.shape
    tm = tri.shape[0]
    row = lambda i: (i, 0)
    fixed = lambda i: (0, 0)
    return pl.pallas_call(
        _moe_pre_kernel,
        grid=(t // tm,),
        in_specs=[
            pl.BlockSpec((tm, d), row),
            pl.BlockSpec((tm, y_self.shape[1]), row),
            pl.BlockSpec((tm, y_mem.shape[1]), row),
            pl.BlockSpec(w_out.shape, fixed),
            pl.BlockSpec((1, d), fixed),
            pl.BlockSpec((3 * d, LANES), fixed),
            pl.BlockSpec((tm, tm), fixed),
        ],
        out_specs=[
            pl.BlockSpec((tm, d), row),
            pl.BlockSpec((tm, d), row),
            pl.BlockSpec((tm, LANES), row),
            pl.BlockSpec((SUBLANES, tm), lambda i: (0, i)),
            pl.BlockSpec((1, LANES), fixed),
        ],
        out_shape=[
            jax.ShapeDtypeStruct((t, d), F32),
            jax.ShapeDtypeStruct((t, d), F32),
            jax.ShapeDtypeStruct((t, LANES), F32),
            jax.ShapeDtypeStruct((SUBLANES, t), F32),
            jax.ShapeDtypeStruct((1, LANES), F32),
        ],
        scratch_shapes=[pltpu.VMEM((1, LANES), F32)],
        compiler_params=_params("arbitrary"),
        name="moe_pre",
    )(x, y_self, y_mem, w_out, g, w_router_pad, tri)


def _slot_table_kernel(idx_ref, tbl_ref, *, n_tokens, n_tiles, stride):
    def fill(j, c):
        bank = 2 * n_tokens + (j & 1) * stride
        for r in range(stride):
            tbl_ref[j * stride + r] = bank + r
        return c

    lax.fori_loop(0, n_tiles + 1, fill, 0)

    def body(tok, c):
        for k in range(2):
            tbl_ref[idx_ref[k * n_tokens + tok]] = k * n_tokens + tok
        return c

    lax.fori_loop(0, n_tokens, body, 0, unroll=8)


def _slot_table(pos, n_tiles, n_tokens):
    tm = MOE_ROW_TILE
    table_idx = (pos // tm + 1) * MOE_SRC_STRIDE + pos % tm
    smem = pl.BlockSpec(memory_space=pltpu.SMEM)
    return pl.pallas_call(
        functools.partial(_slot_table_kernel, n_tokens=n_tokens, n_tiles=n_tiles,
                          stride=MOE_SRC_STRIDE),
        in_specs=[smem],
        out_specs=smem,
        out_shape=jax.ShapeDtypeStruct(((n_tiles + 1) * MOE_SRC_STRIDE,), jnp.int32),
        name="moe_slot_table",
    )(table_idx)


def _moe_kernel(te_ref, na_ref, nv_ref, tbl_ref, h_ref, wg_ref, wu_ref, wd_ref, y_ref, xg_sc, acc_sc,
                xb_sc, gsem, ssem, *, chunk, stride, tm):
    del te_ref
    i = pl.program_id(0)
    f = pl.program_id(1)
    n_tokens = h_ref.shape[0]
    n_copied = pl.num_programs(1) * chunk
    slot = i & 1
    na = na_ref[0]

    def gather_copy(tile, sl, r):
        src = tbl_ref[(tile + 1) * stride + r] & (n_tokens - 1)
        return pltpu.make_async_copy(h_ref.at[pl.ds(src, 1)], xg_sc.at[sl, pl.ds(r, 1)],
                                     gsem.at[sl])

    def scatter_copy(tile, sl, r):
        dst = tbl_ref[(tile + 1) * stride + r]
        return pltpu.make_async_copy(acc_sc.at[sl, pl.ds(r, 1)], y_ref.at[pl.ds(dst, 1)],
                                     ssem.at[sl])

    def for_rows(fn):
        def body(r, c):
            fn(r)
            return c
        lax.fori_loop(0, n_copied, body, 0, unroll=8)

    @pl.when((i == 0) & (f == 0))
    def _():
        acc_sc[...] = jnp.zeros_like(acc_sc)
        for bank in range(2):
            zero = pltpu.make_async_copy(
                acc_sc.at[bank], y_ref.at[pl.ds(2 * n_tokens + bank * stride, stride)], ssem.at[0])
            zero.start()
            zero.wait()
        for_rows(lambda r: gather_copy(0, 0, r).start())

    @pl.when((i == na) & (f == 0))
    def _():
        for_rows(lambda r: gather_copy(i, slot, r).wait())

        @pl.when(i >= 1)
        def _():
            for_rows(lambda r: scatter_copy(i - 2, slot, r).wait())

        for_rows(lambda r: scatter_copy(i - 1, 1 - slot, r).start())
        for_rows(lambda r: scatter_copy(i - 1, 1 - slot, r).wait())

    @pl.when(i < na)
    def _():
        @pl.when(f == 0)
        def _():
            for_rows(lambda r: gather_copy(i, slot, r).wait())
            xb_sc[...] = xg_sc[slot, :tm, :].astype(BF16)

            @pl.when(i >= 1)
            def _():
                for_rows(lambda r: scatter_copy(i - 2, slot, r).wait())

            acc_sc[slot, :tm, :] = jnp.zeros((tm, acc_sc.shape[2]), F32)

        def move_rows(lo, hi):
            for r in range(lo, hi):
                gather_copy(i + 1, 1 - slot, f * chunk + r).start()
                scatter_copy(i - 1, 1 - slot, f * chunk + r).start()

        def expert_rows(n_rows):
            xb = xb_sc[:n_rows, :]
            g = _dot(xb, wg_ref[0].astype(BF16))
            move_rows(0, chunk // 2)
            u = _dot(xb, wu_ref[0].astype(BF16))
            move_rows(chunk // 2, chunk)
            acc_sc[slot, :n_rows, :] += _dot(_swiglu_hidden(g, u), wd_ref[0].astype(BF16))

        @pl.when(nv_ref[i] > tm // 2)
        def _():
            expert_rows(tm)

        @pl.when(nv_ref[i] <= tm // 2)
        def _():
            expert_rows(tm // 2)


def _moe_experts(tile_expert, n_active, n_valid, table, h_rows, w_gate_up, w_down, n_tiles):
    n_tokens = h_rows.shape[0]
    assert n_tokens & (n_tokens - 1) == 0
    d = w_down.shape[2]
    ff = w_down.shape[1]
    ct = MOE_COL_TILE
    nf = ff // ct
    tm = MOE_ROW_TILE
    stride = MOE_SRC_STRIDE
    chunk = -(-tm // nf)
    assert nf * chunk <= stride

    def col(i, f, na):
        return jnp.where(i < na[0], f, nf - 1)

    return pl.pallas_call(
        functools.partial(_moe_kernel, chunk=chunk, stride=stride, tm=tm),
        grid_spec=pltpu.PrefetchScalarGridSpec(
            num_scalar_prefetch=4,
            grid=(n_tiles, nf),
            in_specs=[
                pl.BlockSpec(memory_space=pl.ANY),
                pl.BlockSpec((1, d, ct), lambda i, f, te, na, *_: (te[i], 0, col(i, f, na))),
                pl.BlockSpec((1, d, ct), lambda i, f, te, na, *_: (te[i], 0, nf + col(i, f, na))),
                pl.BlockSpec((1, ct, d), lambda i, f, te, na, *_: (te[i], col(i, f, na), 0)),
            ],
            out_specs=pl.BlockSpec(memory_space=pl.ANY),
            scratch_shapes=[pltpu.VMEM((2, stride, d), F32), pltpu.VMEM((2, stride, d), F32),
                            pltpu.VMEM((tm, d), BF16),
                            pltpu.SemaphoreType.DMA((2,)), pltpu.SemaphoreType.DMA((2,))],
        ),
        out_shape=jax.ShapeDtypeStruct((2 * n_tokens + 2 * stride, d), F32),
        compiler_params=_params("arbitrary", "arbitrary"),
        name="moe_experts",
    )(tile_expert, n_active, n_valid, table, h_rows, w_gate_up, w_gate_up, w_down)


def _combine_kernel(x_ref, route_ref, g_ref, y0_ref, y1_ref, o_ref):
    route = route_ref[...]
    y = x_ref[...] + route[:, 4:5] * y0_ref[...] + route[:, 5:6] * y1_ref[...]
    o_ref[...] = _rms(y, g_ref[...])


def _combine(x, route, g, y_pairs):
    t, d = x.shape
    tile = COMBINE_TILE
    nt = t // tile
    return pl.pallas_call(
        _combine_kernel,
        grid=(nt,),
        in_specs=[
            pl.BlockSpec((tile, d), lambda i: (i, 0)),
            pl.BlockSpec((tile, LANES), lambda i: (i, 0)),
            pl.BlockSpec((1, d), lambda i: (0, 0)),
            pl.BlockSpec((tile, d), lambda i: (i, 0)),
            pl.BlockSpec((tile, d), lambda i: (nt + i, 0)),
        ],
        out_specs=pl.BlockSpec((tile, d), lambda i: (i, 0)),
        out_shape=jax.ShapeDtypeStruct((t, d), F32),
        compiler_params=_params("parallel"),
        name="moe_combine",
    )(x, route, g, y_pairs, y_pairs)


def _row(v):
    return v.reshape(1, -1).astype(F32)


def _pad_lanes(v):
    return jnp.pad(v.astype(F32), (0, LANES - v.shape[0])).reshape(1, LANES)


def kernel(x, mem, positions, mix_norm_g, ffn_norm_g, mem_norm_g, w_mem_kv, w_out, w_in_a, b_forget, w_q_b, lambda_q1, lambda_k1, lambda_q2, lambda_k2, subln_g, kv_norm_g, w_kv_shared, w_gate_up_dense, w_down_dense, w_router, w_gate_up_moe, w_down_moe, final_norm_g):
    batch, seq, d = x.shape
    mem_tokens = mem.shape[1]
    t = batch * seq
    assert w_in_a.shape[0] == 1 and w_q_b.shape[0] == 1 and w_out.shape[0] == 2
    fox_w = FOX_HEADS * HEAD_DIM
    diff_w = DIFF_HEADS * 2 * HEAD_DIM
    scale = HEAD_DIM ** -0.5

    xf = x.reshape(t, d)
    memf = mem.reshape(batch * mem_tokens, d)

    wa = w_in_a[0]
    n_a = wa.shape[1]
    n_a_pad = -(-n_a // LANES) * LANES
    col = jnp.arange(n_a_pad)
    col_scale = jnp.where((col < fox_w) | (col >= 3 * fox_w + FOX_HEADS), scale, 1.0).astype(F32)
    w_a = (jnp.pad(wa, ((0, 0), (0, n_a_pad - n_a))) * col_scale).astype(BF16)
    w_b = (w_q_b[0] * scale).astype(BF16)
    w_kv = w_kv_shared.astype(BF16)
    w_o = w_out.astype(BF16)
    w_mkv = w_mem_kv.astype(BF16)
    w_gu_d = w_gate_up_dense[0].astype(BF16)
    w_dn_d = w_down_dense[0].astype(BF16)
    w_gu_m = w_gate_up_moe[0]
    w_dn_m = w_down_moe[0]
    w_r = jnp.pad(w_router[0].astype(F32), ((0, 0), (0, LANES - N_EXPERTS)))
    w_r_hi = w_r.astype(BF16)
    w_r_lo = (w_r - w_r_hi.astype(F32)).astype(BF16)
    w_r = jnp.concatenate([w_r_hi, w_r_hi, w_r_lo], axis=0)

    qkvm, log_f = _proj_a(xf, _row(mix_norm_g[0]), w_a, _pad_lanes(b_forget[0]), 3 * fox_w,
                          FOX_HEADS)
    cum_t = _cumsum(log_f, batch, seq)
    y_self = _fox_attention(qkvm, cum_t, batch, seq)
    mkv0 = _norm_matmul(memf, _row(mem_norm_g[0]), w_mkv[0], "mem_kv0")
    y_mem = _mem_attention(qkvm, 3 * fox_w // MEM_WIDTH, mkv0, batch, seq, mem_tokens)
    x2 = _dense_ffn(xf, y_self, y_mem, w_o[0], _row(ffn_norm_g[0]), w_gu_d, w_dn_d)

    half = ROT_DIM // 2
    inv_freq = ROPE_THETA ** (-(jnp.arange(half, dtype=F32) * 2.0 / ROT_DIM))
    inv_col = jnp.tile(inv_freq, LANES // half).reshape(LANES, 1)
    pos_rows = positions.astype(jnp.int32).reshape(t // ROW_TILE, 1, ROW_TILE)
    cos_t, sin_t = _rope_tables(pos_rows, inv_col)
    kv, qb = _proj_rope(x2, _row(kv_norm_g), _row(mix_norm_g[1]), w_kv, w_b, cos_t, sin_t, diff_w)

    lam_init = 0.8 - 0.6 * math.exp(-0.3 * 1)
    lam_rows = jnp.concatenate(
        [jnp.pad(v[0].astype(F32), (0, LANES - HEAD_DIM)).reshape(1, LANES)
         for v in (lambda_q1, lambda_k1, lambda_q2, lambda_k2)]
        + [jnp.zeros((4, LANES), F32)], axis=0)
    y_self = _diff_attention(lam_rows, _row(subln_g[0]), qb, kv, batch, seq, lam_init)
    mkv1 = _norm_matmul(memf, _row(mem_norm_g[1]), w_mkv[1], "mem_kv1")
    y_mem = _mem_attention(qb, diff_w // MEM_WIDTH, mkv1, batch, seq, mem_tokens)

    tri = jnp.tril(jnp.ones((MOE_PRE_TILE, MOE_PRE_TILE), BF16), -1)
    x3, h_rows, route, route_t, cnt = _moe_pre(x2, y_self, y_mem, w_o[1], _row(ffn_norm_g[1]), w_r, tri)
    tm = MOE_ROW_TILE
    n_tiles = 2 * t // tm + N_EXPERTS
    counts = cnt[0, :N_EXPERTS].astype(jnp.int32)
    tiles_per = (counts + tm - 1) // tm
    tile_end = jnp.cumsum(tiles_per)
    group_off = (tile_end - tiles_per) * tm
    n_active = tile_end[-1:]
    tile_id = jnp.minimum(jnp.arange(n_tiles, dtype=jnp.int32), n_active - 1)
    tile_expert = jnp.sum(tile_id[:, None] >= tile_end[None, :], axis=1).astype(jnp.int32)
    idx = route_t[0:2].astype(jnp.int32)
    rank = route_t[2:4].astype(jnp.int32)
    expert_ids = jnp.arange(N_EXPERTS, dtype=jnp.int32).reshape(N_EXPERTS, 1, 1)
    off = jnp.sum(jnp.where(idx[None] == expert_ids, group_off.reshape(N_EXPERTS, 1, 1), 0), axis=0)
    pos = (off + rank).reshape(2 * t)

    table = _slot_table(pos, n_tiles, t)
    group_end = group_off + counts
    n_valid = jnp.clip(group_end[tile_expert] - tile_id * tm, 0, tm).astype(jnp.int32)
    y_pairs = _moe_experts(tile_expert, n_active.astype(jnp.int32), n_valid, table, h_rows,
                           w_gu_m, w_dn_m, n_tiles)
    out = _combine(x3, route, _row(final_norm_g), y_pairs)
    return out.reshape(batch, seq, d)
```

```python
import functools
import math

import jax
import jax.numpy as jnp
from jax import lax
from jax.experimental import pallas as pl
from jax.experimental.pallas import tpu as pltpu

F32 = jnp.float32
BF16 = jnp.bfloat16

HEAD_DIM = 64
LANES = 128
SUBLANES = 8
CHUNK_SHIFT = 6
FOX_HEADS = 12
DIFF_HEADS = 6
MEM_HEADS = 4
MEM_WIDTH = MEM_HEADS * HEAD_DIM
ROPE_THETA = 500000.0
ROT_DIM = HEAD_DIM // 4
N_EXPERTS = 8
EPS = 1e-5
NEG = -1e30
VMEM_LIMIT = 48 * 1024 * 1024
FFN_VMEM_LIMIT = 56 * 1024 * 1024

ROW_TILE = 1024
FFN_ROW_TILE = 1024
FFN_COL_TILE = 256
ATT_TILE = 256
ROW_CHUNK = 16
ATT_SLOTS = 2
MEM_Q_TILE = 512
MOE_PRE_TILE = 1024
MOE_ROW_TILE = 1024
MOE_COL_TILE = 512
MOE_SRC_STRIDE = 1032
COMBINE_TILE = 512


def _params(*sem):
    return pltpu.CompilerParams(dimension_semantics=sem, vmem_limit_bytes=VMEM_LIMIT)


def _rms(x, g):
    ms = jnp.mean(x * x, axis=-1, keepdims=True)
    return x * lax.rsqrt(ms + EPS) * g


def _nt_dot(a, b):
    return lax.dot_general(a, b, (((1,), (1,)), ((), ())), preferred_element_type=F32)


def _dot(a, b):
    return jnp.dot(a, b, preferred_element_type=F32)


def _proj_a_kernel(x_ref, g_ref, w_ref, b_ref, o_ref, lf_ref, *, n_qkv, n_gate, col_chunk):
    h = _rms(x_ref[...], g_ref[...]).astype(BF16)
    for c in range(n_qkv // col_chunk):
        cs = slice(c * col_chunk, (c + 1) * col_chunk)
        o_ref[:, cs] = _dot(h, w_ref[:, cs]).astype(BF16)
    tail = _dot(h, w_ref[:, n_qkv:])
    blocks = [tail[:, c * LANES:(c + 1) * LANES] for c in range(tail.shape[1] // LANES)]
    rolled = [pltpu.roll(b, LANES - n_gate, 1) for b in blocks]
    keep = lax.broadcasted_iota(jnp.int32, (1, LANES), 1) < LANES - n_gate
    for c in range(len(blocks) - 1):
        o_ref[:, n_qkv + c * LANES:n_qkv + (c + 1) * LANES] = jnp.where(
            keep, rolled[c], rolled[c + 1]).astype(BF16)
    z = blocks[0] + b_ref[...]
    lf_ref[...] = jnp.minimum(z, 0.0) - jnp.log1p(jnp.exp(-jnp.abs(z)))


def _proj_a(x, g, w, b_pad, n_qkv, n_gate):
    t, d = x.shape
    n_all = w.shape[1]
    n_main = n_all - LANES
    return pl.pallas_call(
        functools.partial(_proj_a_kernel, n_qkv=n_qkv, n_gate=n_gate, col_chunk=768),
        grid=(t // ROW_TILE,),
        in_specs=[
            pl.BlockSpec((ROW_TILE, d), lambda i: (i, 0)),
            pl.BlockSpec((1, d), lambda i: (0, 0)),
            pl.BlockSpec((d, n_all), lambda i: (0, 0)),
            pl.BlockSpec((1, LANES), lambda i: (0, 0)),
        ],
        out_specs=[
            pl.BlockSpec((ROW_TILE, n_main), lambda i: (i, 0)),
            pl.BlockSpec((ROW_TILE, LANES), lambda i: (i, 0)),
        ],
        out_shape=[
            jax.ShapeDtypeStruct((t, n_main), BF16),
            jax.ShapeDtypeStruct((t, LANES), F32),
        ],
        compiler_params=_params("parallel"),
        name="proj_a",
    )(x, g, w, b_pad)


def _cumsum_kernel(lf_ref, ct_ref):
    x = lf_ref[...]
    s = x.shape[0]
    row = lax.broadcasted_iota(jnp.int32, x.shape, 0)
    sh = 1
    while sh < s:
        x = x + jnp.where(row >= sh, pltpu.roll(x, sh, 0), 0.0)
        sh *= 2
    ct_ref[0] = x.T


def _cumsum(lf, batch, seq):
    return pl.pallas_call(
        _cumsum_kernel,
        grid=(batch,),
        in_specs=[pl.BlockSpec((seq, LANES), lambda b: (b, 0))],
        out_specs=pl.BlockSpec((1, LANES, seq), lambda b: (b, 0, 0)),
        out_shape=jax.ShapeDtypeStruct((batch, LANES, seq), F32),
        compiler_params=_params("parallel"),
        name="cumsum",
    )(lf)


def _softmax_rows(s_ref, p_ref, m_ref, hi, tq, col_bias, keep_fn):
    lo = hi - tq
    shape = (ROW_CHUNK, LANES)

    def visibility(r, c):
        r0, c0 = r * ROW_CHUNK, c * LANES - lo
        if c0 < 0 or keep_fn(r0, c0 + LANES - 1):
            return "all"
        if not keep_fn(r0 + ROW_CHUNK - 1, c0):
            return "none"
        ri = lax.broadcasted_iota(jnp.int32, shape, 0) + r0
        ci = lax.broadcasted_iota(jnp.int32, shape, 1) + c0
        return keep_fn(ri, ci)

    def load(r, c, vis):
        t = s_ref[r * ROW_CHUNK:(r + 1) * ROW_CHUNK, c * LANES:(c + 1) * LANES]
        if col_bias is not None:
            t = t - col_bias[:, c * LANES:(c + 1) * LANES]
        return t if isinstance(vis, str) else jnp.where(vis, t, NEG)

    for r in range(tq // ROW_CHUNK):
        m_acc = None
        for c in range(hi // LANES):
            vis = visibility(r, c)
            if isinstance(vis, str) and vis == "none":
                continue
            t = load(r, c, vis)
            m_acc = t if m_acc is None else jnp.maximum(m_acc, t)
        m_ref[r * ROW_CHUNK:(r + 1) * ROW_CHUNK, :] = jnp.broadcast_to(
            jnp.max(m_acc, axis=1, keepdims=True), shape)
    for r in range(tq // ROW_CHUNK):
        rows = slice(r * ROW_CHUNK, (r + 1) * ROW_CHUNK)
        m = m_ref[rows, :]
        for c in range(hi // LANES):
            cols = slice(c * LANES, (c + 1) * LANES)
            vis = visibility(r, c)
            if isinstance(vis, str) and vis == "none":
                p_ref[rows, cols] = jnp.zeros(shape, BF16)
            else:
                p_ref[rows, cols] = jnp.exp((load(r, c, vis) - m).astype(BF16))


def _attention_sweep(q_ref, k_ref, v_aug, scratch, tq, col_bias_fn, keep_fn, emit):
    seq = q_ref.shape[0]
    n_items = 2 * (seq // tq)
    n_slots = scratch[0].shape[0]
    qh, res = {}, {}

    def bufs(n):
        return [sc.at[n % n_slots] for sc in scratch]

    def score(n):
        qi, hh = divmod(n, 2)
        hi = (qi + 1) * tq
        if hh == 0:
            qh[qi] = _split_heads(q_ref[qi * tq:hi, :])
        bufs(n)[0][:, :hi] = _nt_dot(qh[qi][hh], k_ref[:hi, :])

    def softmax(n):
        qi, hh = divmod(n, 2)
        hi = (qi + 1) * tq
        bias = None if col_bias_fn is None else col_bias_fn(hh, hi)
        _softmax_rows(*bufs(n), hi, tq, bias, keep_fn)

    def values(n):
        qi, hh = divmod(n, 2)
        hi = (qi + 1) * tq
        res[hh] = _dot(bufs(n)[1][:, :hi], v_aug(hh)[:hi, :])
        if hh == 1:
            emit(qi, res[0], res[1])

    score(0)
    for n in range(n_items):
        if n + 1 < n_items:
            score(n + 1)
        softmax(n)
        if n:
            values(n - 1)
    values(n_items - 1)


def _split_heads(q):
    low = lax.broadcasted_iota(jnp.int32, (1, LANES), 1) < HEAD_DIM
    zero = jnp.zeros_like(q)
    return jnp.where(low, q, zero), jnp.where(low, zero, q)


def _fox_kernel(q_ref, k_ref, v_ref, ct_ref, o_ref, s_sc, p_sc, m_sc, va_sc, *, tq):
    sub = (2 * pl.program_id(1)) & 7
    low = lax.broadcasted_iota(jnp.int32, (1, LANES), 1) < HEAD_DIM
    v = v_ref[...]
    one = jnp.ones_like(v)
    va_sc[0] = jnp.where(low, v, one)
    va_sc[1] = jnp.where(low, one, v)

    def keep_fn(row, col):
        return col <= row

    def col_bias(hh, hi):
        return ct_ref[0, pl.ds(sub + hh, 1), :hi]

    def emit(qi, r0, r1):
        num = jnp.where(low, r0, r1)
        den = pltpu.roll(jnp.where(low, r1, r0), HEAD_DIM, 1)
        o_ref[qi * tq:(qi + 1) * tq, :] = (num / den).astype(BF16)

    _attention_sweep(q_ref, k_ref, lambda hh: va_sc.at[hh], (s_sc, p_sc, m_sc), tq, col_bias,
                     keep_fn, emit)


def _attention_scratch(tq, seq):
    n = 2 * ATT_SLOTS
    return [pltpu.VMEM((n, tq, seq), F32), pltpu.VMEM((n, tq, seq), BF16),
            pltpu.VMEM((n, tq, LANES), F32)]


def _fox_attention(qkvm, cum_t, batch, seq):
    t = batch * seq
    npair = FOX_HEADS // 2
    return pl.pallas_call(
        functools.partial(_fox_kernel, tq=ATT_TILE),
        grid=(batch, npair),
        in_specs=[
            pl.BlockSpec((seq, LANES), lambda b, p: (b, p)),
            pl.BlockSpec((seq, LANES), lambda b, p: (b, npair + p)),
            pl.BlockSpec((seq, LANES), lambda b, p: (b, 2 * npair + p)),
            pl.BlockSpec((1, 8, seq), lambda b, p: (b, p // 4, 0)),
        ],
        out_specs=pl.BlockSpec((seq, LANES), lambda b, p: (b, p)),
        out_shape=jax.ShapeDtypeStruct((t, npair * LANES), BF16),
        scratch_shapes=_attention_scratch(ATT_TILE, seq) + [pltpu.VMEM((2, seq, LANES), BF16)],
        compiler_params=_params("parallel", "parallel"),
        name="fox_attention",
    )(qkvm, qkvm, qkvm, cum_t)


def _mem_kernel(q_ref, mk_ref, mv_ref, o_ref):
    q = q_ref[...]
    mk = mk_ref[...]
    mv = mv_ref[...]
    lane = lax.broadcasted_iota(jnp.int32, (1, MEM_WIDTH), 1)
    zero = jnp.zeros_like(q)
    out = jnp.zeros(q.shape, F32)
    for h in range(MEM_HEADS):
        hm = (lane >= h * HEAD_DIM) & (lane < (h + 1) * HEAD_DIM)
        s = _nt_dot(jnp.where(hm, q, zero), mk)
        p = jnp.exp(s - jnp.max(s, axis=1, keepdims=True))
        l = jnp.sum(p, axis=1, keepdims=True)
        out = jnp.where(hm, _dot(p.astype(BF16), mv) / l, out)
    o_ref[...] = out.astype(BF16)


def _mem_attention(q_arr, q_col_block, mkv, batch, seq, mem_tokens):
    nq = seq // MEM_Q_TILE
    return pl.pallas_call(
        _mem_kernel,
        grid=(batch, nq),
        in_specs=[
            pl.BlockSpec((MEM_Q_TILE, MEM_WIDTH), lambda b, i: (b * nq + i, q_col_block)),
            pl.BlockSpec((mem_tokens, MEM_WIDTH), lambda b, i: (b, 0)),
            pl.BlockSpec((mem_tokens, MEM_WIDTH), lambda b, i: (b, 1)),
        ],
        out_specs=pl.BlockSpec((MEM_Q_TILE, MEM_WIDTH), lambda b, i: (b * nq + i, 0)),
        out_shape=jax.ShapeDtypeStruct((batch * seq, MEM_WIDTH), BF16),
        compiler_params=_params("parallel", "parallel"),
        name="mem_attention",
    )(q_arr, mkv, mkv)


def _norm_matmul_kernel(x_ref, g_ref, w_ref, o_ref):
    h = _rms(x_ref[...], g_ref[...]).astype(BF16)
    o_ref[...] = _dot(h, w_ref[...]).astype(o_ref.dtype)


def _norm_matmul(x, g, w, name):
    t, d = x.shape
    n = w.shape[1]
    return pl.pallas_call(
        _norm_matmul_kernel,
        grid=(t // ROW_TILE,),
        in_specs=[
            pl.BlockSpec((ROW_TILE, d), lambda i: (i, 0)),
            pl.BlockSpec((1, d), lambda i: (0, 0)),
            pl.BlockSpec((d, n), lambda i: (0, 0)),
        ],
        out_specs=pl.BlockSpec((ROW_TILE, n), lambda i: (i, 0)),
        out_shape=jax.ShapeDtypeStruct((t, n), BF16),
        compiler_params=_params("parallel"),
        name=name,
    )(x, g, w)


def _mixed_residual(x_ref, ys_ref, ym_ref, wo_ref):
    sw = ys_ref.shape[1]
    return x_ref[...] + _dot(ys_ref[...], wo_ref[:sw, :]) + _dot(ym_ref[...], wo_ref[sw:, :])


def _swiglu_hidden(g, u):
    return (g * jax.nn.sigmoid(g) * u).astype(BF16)


def _ffn_kernel(x_ref, ys_ref, ym_ref, wo_ref, g_ref, wgu_ref, wd_ref, o_ref, h_sc, *, col):
    ff = wd_ref.shape[0]
    x = _mixed_residual(x_ref, ys_ref, ym_ref, wo_ref)
    h_sc[...] = _rms(x, g_ref[...]).astype(BF16)
    o_ref[...] = x
    for c in range(ff // col):
        h = h_sc[...]
        gate = _dot(h, wgu_ref[:, c * col:(c + 1) * col])
        up = _dot(h, wgu_ref[:, ff + c * col:ff + (c + 1) * col])
        o_ref[...] += _dot(_swiglu_hidden(gate, up), wd_ref[c * col:(c + 1) * col, :])


def _dense_ffn(x, y_self, y_mem, w_out, g, w_gate_up, w_down):
    t, d = x.shape
    tm = FFN_ROW_TILE
    row = lambda i: (i, 0)
    resident = dict(index_map=lambda i: (0, 0), pipeline_mode=pl.Buffered(1))
    return pl.pallas_call(
        functools.partial(_ffn_kernel, col=FFN_COL_TILE),
        grid=(t // tm,),
        in_specs=[
            pl.BlockSpec((tm, d), row),
            pl.BlockSpec((tm, y_self.shape[1]), row),
            pl.BlockSpec((tm, y_mem.shape[1]), row),
            pl.BlockSpec(w_out.shape, **resident),
            pl.BlockSpec((1, d), lambda i: (0, 0)),
            pl.BlockSpec(w_gate_up.shape, **resident),
            pl.BlockSpec(w_down.shape, **resident),
        ],
        out_specs=pl.BlockSpec((tm, d), row),
        out_shape=jax.ShapeDtypeStruct((t, d), F32),
        scratch_shapes=[pltpu.VMEM((tm, d), BF16)],
        compiler_params=pltpu.CompilerParams(dimension_semantics=("parallel",),
                                             vmem_limit_bytes=FFN_VMEM_LIMIT),
        name="dense_ffn",
    )(x, y_self, y_mem, w_out, g, w_gate_up, w_down)


def _rope_kernel(pos_ref, inv_ref, cos_ref, sin_ref):
    pos = pos_ref[0].astype(F32)
    ang = inv_ref[...] * pos
    reps = LANES // ang.shape[0]
    cs = jnp.concatenate([jnp.cos(ang)] * reps, axis=0)
    sn = jnp.concatenate([jnp.sin(ang)] * reps, axis=0)
    row = lax.broadcasted_iota(jnp.int32, cs.shape, 0)
    rot = (row & (HEAD_DIM - 1)) < ROT_DIM
    first = (row & (ROT_DIM - 1)) < (ROT_DIM // 2)
    cos_ref[...] = jnp.where(rot, cs, 1.0).T
    sin_ref[...] = jnp.where(rot, jnp.where(first, -sn, sn), 0.0).T


def _rope_tables(pos_rows, inv_col):
    steps = pos_rows.shape[0]
    t = steps * ROW_TILE
    return pl.pallas_call(
        _rope_kernel,
        grid=(steps,),
        in_specs=[
            pl.BlockSpec((1, 1, ROW_TILE), lambda i: (i, 0, 0)),
            pl.BlockSpec(inv_col.shape, lambda i: (0, 0)),
        ],
        out_specs=[pl.BlockSpec((ROW_TILE, LANES), lambda i: (i, 0))] * 2,
        out_shape=[jax.ShapeDtypeStruct((t, LANES), F32)] * 2,
        compiler_params=_params("parallel"),
        name="rope_tables",
    )(pos_rows, inv_col)


def _proj_rope_kernel(x_ref, gk_ref, gq_ref, wk_ref, wq_ref, cos_ref, sin_ref, ok_ref, oq_ref, *,
                      n_rope, col_chunk):
    x = x_ref[...]
    xn = x * lax.rsqrt(jnp.mean(x * x, axis=-1, keepdims=True) + EPS)
    cos = cos_ref[...]
    sin = sin_ref[...]
    lane = lax.broadcasted_iota(jnp.int32, (1, LANES), 1)
    first = (lane & (ROT_DIM - 1)) < (ROT_DIM // 2)
    half = ROT_DIM // 2
    for g_ref, w_ref, o_ref in ((gk_ref, wk_ref, ok_ref), (gq_ref, wq_ref, oq_ref)):
        h = (xn * g_ref[...]).astype(BF16)
        for c in range(w_ref.shape[1] // col_chunk):
            a = _dot(h, w_ref[:, c * col_chunk:(c + 1) * col_chunk])
            for s in range(col_chunk // LANES):
                col = c * col_chunk + s * LANES
                blk = a[:, s * LANES:(s + 1) * LANES]
                if col < n_rope:
                    partner = jnp.where(first, pltpu.roll(blk, LANES - half, 1),
                                        pltpu.roll(blk, half, 1))
                    blk = blk * cos + partner * sin
                o_ref[:, col:col + LANES] = blk.astype(BF16)


def _proj_rope(x, g_kv, g_q, w_kv, w_q, cos, sin, n_rope):
    t, d = x.shape
    nk, nq = w_kv.shape[1], w_q.shape[1]
    row = lambda i: (i, 0)
    fixed = lambda i: (0, 0)
    return pl.pallas_call(
        functools.partial(_proj_rope_kernel, n_rope=n_rope, col_chunk=256),
        grid=(t // ROW_TILE,),
        in_specs=[
            pl.BlockSpec((ROW_TILE, d), row),
            pl.BlockSpec((1, d), fixed),
            pl.BlockSpec((1, d), fixed),
            pl.BlockSpec((d, nk), fixed),
            pl.BlockSpec((d, nq), fixed),
            pl.BlockSpec((ROW_TILE, LANES), row),
            pl.BlockSpec((ROW_TILE, LANES), row),
        ],
        out_specs=[pl.BlockSpec((ROW_TILE, nk), row), pl.BlockSpec((ROW_TILE, nq), row)],
        out_shape=[jax.ShapeDtypeStruct((t, nk), BF16), jax.ShapeDtypeStruct((t, nq), BF16)],
        compiler_params=_params("parallel"),
        name="proj_kv_q",
    )(x, g_kv, g_q, w_kv, w_q, cos, sin)


def _diff_kernel(lam_ref, sg_ref, q_ref, k_ref, v_ref, o_ref, s_sc, p_sc, m_sc, va_sc, *, tq,
                 lam_init):
    lp = lam_ref[...]
    lam = (jnp.exp(jnp.sum(lp[0:1] * lp[1:2], axis=1, keepdims=True))
           - jnp.exp(jnp.sum(lp[2:3] * lp[3:4], axis=1, keepdims=True)) + lam_init)
    va_sc[:, :LANES] = v_ref[...]
    va_sc[:, LANES:] = jnp.ones(v_ref.shape, BF16)

    def keep_fn(row, col):
        return (col >> CHUNK_SHIFT) <= (row >> CHUNK_SHIFT)

    def emit(qi, r1, r2):
        o = r1[:, :LANES] / r1[:, LANES:] - lam * (r2[:, :LANES] / r2[:, LANES:])
        o = _rms(o, sg_ref[...]) * (1.0 - lam_init)
        o_ref[qi * tq:(qi + 1) * tq, :] = o.astype(BF16)

    _attention_sweep(q_ref, k_ref, lambda hh: va_sc, (s_sc, p_sc, m_sc), tq, None, keep_fn, emit)


def _diff_attention(lam_rows, subln_g, qb, kv, batch, seq, lam_init):
    t = batch * seq
    return pl.pallas_call(
        functools.partial(_diff_kernel, tq=ATT_TILE, lam_init=lam_init),
        grid=(batch, DIFF_HEADS),
        in_specs=[
            pl.BlockSpec((8, LANES), lambda b, h: (0, 0)),
            pl.BlockSpec((1, LANES), lambda b, h: (0, 0)),
            pl.BlockSpec((seq, LANES), lambda b, h: (b, h)),
            pl.BlockSpec((seq, LANES), lambda b, h: (b, h)),
            pl.BlockSpec((seq, LANES), lambda b, h: (b, DIFF_HEADS + h)),
        ],
        out_specs=pl.BlockSpec((seq, LANES), lambda b, h: (b, h)),
        out_shape=jax.ShapeDtypeStruct((t, DIFF_HEADS * LANES), BF16),
        scratch_shapes=_attention_scratch(ATT_TILE, seq) + [pltpu.VMEM((seq, 2 * LANES), BF16)],
        compiler_params=_params("parallel", "parallel"),
        name="diff_attention",
    )(lam_rows, subln_g, qb, kv, kv)


def _moe_pre_kernel(x_ref, ys_ref, ym_ref, wo_ref, g_ref, wr_ref, tri_ref, x3_ref, hp_ref, route_ref,
                    route_t_ref, cnt_ref, run_sc):
    @pl.when(pl.program_id(0) == 0)
    def _():
        run_sc[...] = jnp.zeros_like(run_sc)

    x = _mixed_residual(x_ref, ys_ref, ym_ref, wo_ref)
    x3_ref[...] = x
    hf = _rms(x, g_ref[...])
    hp_ref[...] = hf

    h_hi = hf.astype(BF16)
    h_lo = (hf - h_hi.astype(F32)).astype(BF16)
    logits = _dot(jnp.concatenate([h_hi, h_lo, h_hi], axis=1), wr_ref[...])
    lane = lax.broadcasted_iota(jnp.int32, logits.shape, 1)
    lanef = lane.astype(F32)
    lg = jnp.where(lane < N_EXPERTS, logits, NEG)
    v1 = jnp.max(lg, axis=1, keepdims=True)
    i1 = jnp.min(jnp.where(lg == v1, lanef, float(LANES)), axis=1, keepdims=True)
    lg2 = jnp.where(lanef == i1, NEG, lg)
    v2 = jnp.max(lg2, axis=1, keepdims=True)
    i2 = jnp.min(jnp.where(lg2 == v2, lanef, float(LANES)), axis=1, keepdims=True)
    e = jnp.exp(v2 - v1)
    g1 = 1.0 / (1.0 + e)
    g2 = e / (1.0 + e)

    oh1 = lanef == i1
    oh2 = lanef == i2
    oh = jnp.where(oh1 | oh2, 1.0, 0.0)
    before = _dot(tri_ref[...], oh.astype(BF16)) + run_sc[...]
    r1 = jnp.sum(jnp.where(oh1, before, 0.0), axis=1, keepdims=True)
    r2 = jnp.sum(jnp.where(oh2, before, 0.0), axis=1, keepdims=True)
    run_sc[...] += jnp.sum(oh, axis=0, keepdims=True)
    cnt_ref[...] = run_sc[...]

    route = jnp.where(lane == 0, i1, 0.0)
    for ln, val in ((1, i2), (2, r1), (3, r2), (4, g1), (5, g2)):
        route = jnp.where(lane == ln, val, route)
    route_ref[...] = route
    route_t_ref[...] = route.T[:SUBLANES, :]


def _moe_pre(x, y_self, y_mem, w_out, g, w_router_pad, tri):
    t, d = x.shape
    tm = tri.shape[0]
    row = lambda i: (i, 0)
    fixed = lambda i: (0, 0)
    return pl.pallas_call(
        _moe_pre_kernel,
        grid=(t // tm,),
        in_specs=[
            pl.BlockSpec((tm, d), row),
            pl.BlockSpec((tm, y_self.shape[1]), row),
            pl.BlockSpec((tm, y_mem.shape[1]), row),
            pl.BlockSpec(w_out.shape, fixed),
            pl.BlockSpec((1, d), fixed),
            pl.BlockSpec((3 * d, LANES), fixed),
            pl.BlockSpec((tm, tm), fixed),
        ],
        out_specs=[
            pl.BlockSpec((tm, d), row),
            pl.BlockSpec((tm, d), row),
            pl.BlockSpec((tm, LANES), row),
            pl.BlockSpec((SUBLANES, tm), lambda i: (0, i)),
            pl.BlockSpec((1, LANES), fixed),
        ],
        out_shape=[
            jax.ShapeDtypeStruct((t, d), F32),
            jax.ShapeDtypeStruct((t, d), F32),
            jax.ShapeDtypeStruct((t, LANES), F32),
            jax.ShapeDtypeStruct((SUBLANES, t), F32),
            jax.ShapeDtypeStruct((1, LANES), F32),
        ],
        scratch_shapes=[pltpu.VMEM((1, LANES), F32)],
        compiler_params=_params("arbitrary"),
        name="moe_pre",
    )(x, y_self, y_mem, w_out, g, w_router_pad, tri)


def _slot_table_kernel(idx_ref, tbl_ref, *, n_tokens, n_tiles, stride):
    def fill(j, c):
        bank = 2 * n_tokens + (j & 1) * stride
        for r in range(stride):
            tbl_ref[j * stride + r] = bank + r
        return c

    lax.fori_loop(0, n_tiles + 1, fill, 0)

    def body(tok, c):
        for k in range(2):
            tbl_ref[idx_ref[k * n_tokens + tok]] = k * n_tokens + tok
        return c

    lax.fori_loop(0, n_tokens, body, 0, unroll=8)


def _slot_table(pos, n_tiles, n_tokens):
    tm = MOE_ROW_TILE
    table_idx = (pos // tm + 1) * MOE_SRC_STRIDE + pos % tm
    smem = pl.BlockSpec(memory_space=pltpu.SMEM)
    return pl.pallas_call(
        functools.partial(_slot_table_kernel, n_tokens=n_tokens, n_tiles=n_tiles,
                          stride=MOE_SRC_STRIDE),
        in_specs=[smem],
        out_specs=smem,
        out_shape=jax.ShapeDtypeStruct(((n_tiles + 1) * MOE_SRC_STRIDE,), jnp.int32),
        name="moe_slot_table",
    )(table_idx)


def _moe_kernel(te_ref, na_ref, nv_ref, tbl_ref, h_ref, wg_ref, wu_ref, wd_ref, y_ref, xg_sc, acc_sc,
                xb_sc, gsem, ssem, *, chunk, stride, tm):
    del te_ref
    i = pl.program_id(0)
    f = pl.program_id(1)
    n_tokens = h_ref.shape[0]
    n_copied = pl.num_programs(1) * chunk
    slot = i & 1
    na = na_ref[0]

    def gather_copy(tile, sl, r):
        src = tbl_ref[(tile + 1) * stride + r] & (n_tokens - 1)
        return pltpu.make_async_copy(h_ref.at[pl.ds(src, 1)], xg_sc.at[sl, pl.ds(r, 1)],
                                     gsem.at[sl])

    def scatter_copy(tile, sl, r):
        dst = tbl_ref[(tile + 1) * stride + r]
        return pltpu.make_async_copy(acc_sc.at[sl, pl.ds(r, 1)], y_ref.at[pl.ds(dst, 1)],
                                     ssem.at[sl])

    def for_rows(fn):
        def body(r, c):
            fn(r)
            return c
        lax.fori_loop(0, n_copied, body, 0, unroll=8)

    @pl.when((i == 0) & (f == 0))
    def _():
        acc_sc[...] = jnp.zeros_like(acc_sc)
        for bank in range(2):
            zero = pltpu.make_async_copy(
                acc_sc.at[bank], y_ref.at[pl.ds(2 * n_tokens + bank * stride, stride)], ssem.at[0])
            zero.start()
            zero.wait()
        for_rows(lambda r: gather_copy(0, 0, r).start())

    @pl.when((i == na) & (f == 0))
    def _():
        for_rows(lambda r: gather_copy(i, slot, r).wait())

        @pl.when(i >= 1)
        def _():
            for_rows(lambda r: scatter_copy(i - 2, slot, r).wait())

        for_rows(lambda r: scatter_copy(i - 1, 1 - slot, r).start())
        for_rows(lambda r: scatter_copy(i - 1, 1 - slot, r).wait())

    @pl.when(i < na)
    def _():
        @pl.when(f == 0)
        def _():
            for_rows(lambda r: gather_copy(i, slot, r).wait())
            xb_sc[...] = xg_sc[slot, :tm, :].astype(BF16)

            @pl.when(i >= 1)
            def _():
                for_rows(lambda r: scatter_copy(i - 2, slot, r).wait())

            acc_sc[slot, :tm, :] = jnp.zeros((tm, acc_sc.shape[2]), F32)

        def move_rows(lo, hi):
            for r in range(lo, hi):
                gather_copy(i + 1, 1 - slot, f * chunk + r).start()
                scatter_copy(i - 1, 1 - slot, f * chunk + r).start()

        def expert_rows(n_rows):
            xb = xb_sc[:n_rows, :]
            g = _dot(xb, wg_ref[0].astype(BF16))
            move_rows(0, chunk // 2)
            u = _dot(xb, wu_ref[0].astype(BF16))
            move_rows(chunk // 2, chunk)
            acc_sc[slot, :n_rows, :] += _dot(_swiglu_hidden(g, u), wd_ref[0].astype(BF16))

        @pl.when(nv_ref[i] > tm // 2)
        def _():
            expert_rows(tm)

        @pl.when(nv_ref[i] <= tm // 2)
        def _():
            expert_rows(tm // 2)


def _moe_experts(tile_expert, n_active, n_valid, table, h_rows, w_gate_up, w_down, n_tiles):
    n_tokens = h_rows.shape[0]
    assert n_tokens & (n_tokens - 1) == 0
    d = w_down.shape[2]
    ff = w_down.shape[1]
    ct = MOE_COL_TILE
    nf = ff // ct
    tm = MOE_ROW_TILE
    stride = MOE_SRC_STRIDE
    chunk = -(-tm // nf)
    assert nf * chunk <= stride

    def col(i, f, na):
        return jnp.where(i < na[0], f, nf - 1)

    return pl.pallas_call(
        functools.partial(_moe_kernel, chunk=chunk, stride=stride, tm=tm),
        grid_spec=pltpu.PrefetchScalarGridSpec(
            num_scalar_prefetch=4,
            grid=(n_tiles, nf),
            in_specs=[
                pl.BlockSpec(memory_space=pl.ANY),
                pl.BlockSpec((1, d, ct), lambda i, f, te, na, *_: (te[i], 0, col(i, f, na))),
                pl.BlockSpec((1, d, ct), lambda i, f, te, na, *_: (te[i], 0, nf + col(i, f, na))),
                pl.BlockSpec((1, ct, d), lambda i, f, te, na, *_: (te[i], col(i, f, na), 0)),
            ],
            out_specs=pl.BlockSpec(memory_space=pl.ANY),
            scratch_shapes=[pltpu.VMEM((2, stride, d), F32), pltpu.VMEM((2, stride, d), F32),
                            pltpu.VMEM((tm, d), BF16),
                            pltpu.SemaphoreType.DMA((2,)), pltpu.SemaphoreType.DMA((2,))],
        ),
        out_shape=jax.ShapeDtypeStruct((2 * n_tokens + 2 * stride, d), F32),
        compiler_params=_params("arbitrary", "arbitrary"),
        name="moe_experts",
    )(tile_expert, n_active, n_valid, table, h_rows, w_gate_up, w_gate_up, w_down)


def _combine_kernel(x_ref, route_ref, g_ref, y0_ref, y1_ref, o_ref):
    route = route_ref[...]
    y = x_ref[...] + route[:, 4:5] * y0_ref[...] + route[:, 5:6] * y1_ref[...]
    o_ref[...] = _rms(y, g_ref[...])


def _combine(x, route, g, y_pairs):
    t, d = x.shape
    tile = COMBINE_TILE
    nt = t // tile
    return pl.pallas_call(
        _combine_kernel,
        grid=(nt,),
        in_specs=[
            pl.BlockSpec((tile, d), lambda i: (i, 0)),
            pl.BlockSpec((tile, LANES), lambda i: (i, 0)),
            pl.BlockSpec((1, d), lambda i: (0, 0)),
            pl.BlockSpec((tile, d), lambda i: (i, 0)),
            pl.BlockSpec((tile, d), lambda i: (nt + i, 0)),
        ],
        out_specs=pl.BlockSpec((tile, d), lambda i: (i, 0)),
        out_shape=jax.ShapeDtypeStruct((t, d), F32),
        compiler_params=_params("parallel"),
        name="moe_combine",
    )(x, route, g, y_pairs, y_pairs)


def _row(v):
    return v.reshape(1, -1).astype(F32)


def _pad_lanes(v):
    return jnp.pad(v.astype(F32), (0, LANES - v.shape[0])).reshape(1, LANES)


def kernel(x, mem, positions, mix_norm_g, ffn_norm_g, mem_norm_g, w_mem_kv, w_out, w_in_a, b_forget, w_q_b, lambda_q1, lambda_k1, lambda_q2, lambda_k2, subln_g, kv_norm_g, w_kv_shared, w_gate_up_dense, w_down_dense, w_router, w_gate_up_moe, w_down_moe, final_norm_g):
    batch, seq, d = x.shape
    mem_tokens = mem.shape[1]
    t = batch * seq
    assert w_in_a.shape[0] == 1 and w_q_b.shape[0] == 1 and w_out.shape[0] == 2
    fox_w = FOX_HEADS * HEAD_DIM
    diff_w = DIFF_HEADS * 2 * HEAD_DIM
    scale = HEAD_DIM ** -0.5

    xf = x.reshape(t, d)
    memf = mem.reshape(batch * mem_tokens, d)

    wa = w_in_a[0]
    n_a = wa.shape[1]
    n_a_pad = -(-n_a // LANES) * LANES
    col = jnp.arange(n_a_pad)
    col_scale = jnp.where((col < fox_w) | (col >= 3 * fox_w + FOX_HEADS), scale, 1.0).astype(F32)
    w_a = (jnp.pad(wa, ((0, 0), (0, n_a_pad - n_a))) * col_scale).astype(BF16)
    w_b = (w_q_b[0] * scale).astype(BF16)
    w_kv = w_kv_shared.astype(BF16)
    w_o = w_out.astype(BF16)
    w_mkv = w_mem_kv.astype(BF16)
    w_gu_d = w_gate_up_dense[0].astype(BF16)
    w_dn_d = w_down_dense[0].astype(BF16)
    w_gu_m = w_gate_up_moe[0]
    w_dn_m = w_down_moe[0]
    w_r = jnp.pad(w_router[0].astype(F32), ((0, 0), (0, LANES - N_EXPERTS)))
    w_r_hi = w_r.astype(BF16)
    w_r_lo = (w_r - w_r_hi.astype(F32)).astype(BF16)
    w_r = jnp.concatenate([w_r_hi, w_r_hi, w_r_lo], axis=0)

    qkvm, log_f = _proj_a(xf, _row(mix_norm_g[0]), w_a, _pad_lanes(b_forget[0]), 3 * fox_w,
                          FOX_HEADS)
    cum_t = _cumsum(log_f, batch, seq)
    y_self = _fox_attention(qkvm, cum_t, batch, seq)
    mkv0 = _norm_matmul(memf, _row(mem_norm_g[0]), w_mkv[0], "mem_kv0")
    y_mem = _mem_attention(qkvm, 3 * fox_w // MEM_WIDTH, mkv0, batch, seq, mem_tokens)
    x2 = _dense_ffn(xf, y_self, y_mem, w_o[0], _row(ffn_norm_g[0]), w_gu_d, w_dn_d)

    half = ROT_DIM // 2
    inv_freq = ROPE_THETA ** (-(jnp.arange(half, dtype=F32) * 2.0 / ROT_DIM))
    inv_col = inv_freq.reshape(half, 1)
    pos_rows = positions.astype(jnp.int32).reshape(t // ROW_TILE, 1, ROW_TILE)
    cos_t, sin_t = _rope_tables(pos_rows, inv_col)
    kv, qb = _proj_rope(x2, _row(kv_norm_g), _row(mix_norm_g[1]), w_kv, w_b, cos_t, sin_t, diff_w)

    lam_init = 0.8 - 0.6 * math.exp(-0.3 * 1)
    lam_rows = jnp.concatenate(
        [jnp.pad(v[0].astype(F32), (0, LANES - HEAD_DIM)).reshape(1, LANES)
         for v in (lambda_q1, lambda_k1, lambda_q2, lambda_k2)]
        + [jnp.zeros((4, LANES), F32)], axis=0)
    y_self = _diff_attention(lam_rows, _row(subln_g[0]), qb, kv, batch, seq, lam_init)
    mkv1 = _norm_matmul(memf, _row(mem_norm_g[1]), w_mkv[1], "mem_kv1")
    y_mem = _mem_attention(qb, diff_w // MEM_WIDTH, mkv1, batch, seq, mem_tokens)

    tri = jnp.tril(jnp.ones((MOE_PRE_TILE, MOE_PRE_TILE), BF16), -1)
    x3, h_rows, route, route_t, cnt = _moe_pre(x2, y_self, y_mem, w_o[1], _row(ffn_norm_g[1]), w_r, tri)
    tm = MOE_ROW_TILE
    n_tiles = 2 * t // tm + N_EXPERTS
    counts = cnt[0, :N_EXPERTS].astype(jnp.int32)
    tiles_per = (counts + tm - 1) // tm
    tile_end = jnp.cumsum(tiles_per)
    group_off = (tile_end - tiles_per) * tm
    n_active = tile_end[-1:]
    tile_id = jnp.minimum(jnp.arange(n_tiles, dtype=jnp.int32), n_active - 1)
    tile_expert = jnp.sum(tile_id[:, None] >= tile_end[None, :], axis=1).astype(jnp.int32)
    idx = route_t[0:2].astype(jnp.int32)
    rank = route_t[2:4].astype(jnp.int32)
    expert_ids = jnp.arange(N_EXPERTS, dtype=jnp.int32).reshape(N_EXPERTS, 1, 1)
    off = jnp.sum(jnp.where(idx[None] == expert_ids, group_off.reshape(N_EXPERTS, 1, 1), 0), axis=0)
    pos = (off + rank).reshape(2 * t)

    table = _slot_table(pos, n_tiles, t)
    group_end = group_off + counts
    n_valid = jnp.clip(group_end[tile_expert] - tile_id * tm, 0, tm).astype(jnp.int32)
    y_pairs = _moe_experts(tile_expert, n_active.astype(jnp.int32), n_valid, table, h_rows,
                           w_gu_m, w_dn_m, n_tiles)
    out = _combine(x3, route, _row(final_norm_g), y_pairs)
    return out.reshape(batch, seq, d)
```

```python
import functools
import math

import jax
import jax.numpy as jnp
from jax import lax
from jax.experimental import pallas as pl
from jax.experimental.pallas import tpu as pltpu

F32 = jnp.float32
BF16 = jnp.bfloat16

HEAD_DIM = 64
LANES = 128
SUBLANES = 8
CHUNK_SHIFT = 6
FOX_HEADS = 12
DIFF_HEADS = 6
MEM_HEADS = 4
MEM_WIDTH = MEM_HEADS * HEAD_DIM
ROPE_THETA = 500000.0
ROT_DIM = HEAD_DIM // 4
N_EXPERTS = 8
EPS = 1e-5
NEG = -1e30
VMEM_LIMIT = 48 * 1024 * 1024
FFN_VMEM_LIMIT = 56 * 1024 * 1024

ROW_TILE = 1024
FFN_ROW_TILE = 1024
FFN_COL_TILE = 256
ATT_TILE = 256
ROW_CHUNK = 16
ATT_SLOTS = 2
MEM_Q_TILE = 512
MOE_PRE_TILE = 1024
MOE_ROW_TILE = 1024
MOE_COL_TILE = 512
MOE_SRC_STRIDE = 1032
COMBINE_TILE = 512


def _params(*sem):
    return pltpu.CompilerParams(dimension_semantics=sem, vmem_limit_bytes=VMEM_LIMIT)


def _rms(x, g):
    ms = jnp.mean(x * x, axis=-1, keepdims=True)
    return x * lax.rsqrt(ms + EPS) * g


def _nt_dot(a, b):
    return lax.dot_general(a, b, (((1,), (1,)), ((), ())), preferred_element_type=F32)


def _dot(a, b):
    return jnp.dot(a, b, preferred_element_type=F32)


def _proj_a_kernel(x_ref, g_ref, w_ref, b_ref, o_ref, lf_ref, *, n_qkv, n_gate, col_chunk):
    h = _rms(x_ref[...], g_ref[...]).astype(BF16)
    for c in range(n_qkv // col_chunk):
        cs = slice(c * col_chunk, (c + 1) * col_chunk)
        o_ref[:, cs] = _dot(h, w_ref[:, cs]).astype(BF16)
    tail = _dot(h, w_ref[:, n_qkv:])
    blocks = [tail[:, c * LANES:(c + 1) * LANES] for c in range(tail.shape[1] // LANES)]
    rolled = [pltpu.roll(b, LANES - n_gate, 1) for b in blocks]
    keep = lax.broadcasted_iota(jnp.int32, (1, LANES), 1) < LANES - n_gate
    for c in range(len(blocks) - 1):
        o_ref[:, n_qkv + c * LANES:n_qkv + (c + 1) * LANES] = jnp.where(
            keep, rolled[c], rolled[c + 1]).astype(BF16)
    z = blocks[0] + b_ref[...]
    lf_ref[...] = jnp.minimum(z, 0.0) - jnp.log1p(jnp.exp(-jnp.abs(z)))


def _proj_a(x, g, w, b_pad, n_qkv, n_gate):
    t, d = x.shape
    n_all = w.shape[1]
    n_main = n_all - LANES
    return pl.pallas_call(
        functools.partial(_proj_a_kernel, n_qkv=n_qkv, n_gate=n_gate, col_chunk=768),
        grid=(t // ROW_TILE,),
        in_specs=[
            pl.BlockSpec((ROW_TILE, d), lambda i: (i, 0)),
            pl.BlockSpec((1, d), lambda i: (0, 0)),
            pl.BlockSpec((d, n_all), lambda i: (0, 0)),
            pl.BlockSpec((1, LANES), lambda i: (0, 0)),
        ],
        out_specs=[
            pl.BlockSpec((ROW_TILE, n_main), lambda i: (i, 0)),
            pl.BlockSpec((ROW_TILE, LANES), lambda i: (i, 0)),
        ],
        out_shape=[
            jax.ShapeDtypeStruct((t, n_main), BF16),
            jax.ShapeDtypeStruct((t, LANES), F32),
        ],
        compiler_params=_params("parallel"),
        name="proj_a",
    )(x, g, w, b_pad)


def _cumsum_kernel(lf_ref, ct_ref):
    x = lf_ref[...]
    s = x.shape[0]
    row = lax.broadcasted_iota(jnp.int32, x.shape, 0)
    sh = 1
    while sh < s:
        x = x + jnp.where(row >= sh, pltpu.roll(x, sh, 0), 0.0)
        sh *= 2
    ct_ref[0] = x.T


def _cumsum(lf, batch, seq):
    return pl.pallas_call(
        _cumsum_kernel,
        grid=(batch,),
        in_specs=[pl.BlockSpec((seq, LANES), lambda b: (b, 0))],
        out_specs=pl.BlockSpec((1, LANES, seq), lambda b: (b, 0, 0)),
        out_shape=jax.ShapeDtypeStruct((batch, LANES, seq), F32),
        compiler_params=_params("parallel"),
        name="cumsum",
    )(lf)


def _softmax_rows(s_ref, p_ref, m_ref, hi, tq, col_bias, keep_fn):
    lo = hi - tq
    shape = (ROW_CHUNK, LANES)

    def visibility(r, c):
        r0, c0 = r * ROW_CHUNK, c * LANES - lo
        if c0 < 0 or keep_fn(r0, c0 + LANES - 1):
            return "all"
        if not keep_fn(r0 + ROW_CHUNK - 1, c0):
            return "none"
        ri = lax.broadcasted_iota(jnp.int32, shape, 0) + r0
        ci = lax.broadcasted_iota(jnp.int32, shape, 1) + c0
        return keep_fn(ri, ci)

    def load(r, c, vis):
        t = s_ref[r * ROW_CHUNK:(r + 1) * ROW_CHUNK, c * LANES:(c + 1) * LANES]
        if col_bias is not None:
            t = t - col_bias[:, c * LANES:(c + 1) * LANES]
        return t if isinstance(vis, str) else jnp.where(vis, t, NEG)

    for r in range(tq // ROW_CHUNK):
        m_acc = None
        for c in range(hi // LANES):
            vis = visibility(r, c)
            if isinstance(vis, str) and vis == "none":
                continue
            t = load(r, c, vis)
            m_acc = t if m_acc is None else jnp.maximum(m_acc, t)
        m_ref[r * ROW_CHUNK:(r + 1) * ROW_CHUNK, :] = jnp.broadcast_to(
            jnp.max(m_acc, axis=1, keepdims=True), shape)
    for r in range(tq // ROW_CHUNK):
        rows = slice(r * ROW_CHUNK, (r + 1) * ROW_CHUNK)
        m = m_ref[rows, :]
        for c in range(hi // LANES):
            cols = slice(c * LANES, (c + 1) * LANES)
            vis = visibility(r, c)
            if isinstance(vis, str) and vis == "none":
                p_ref[rows, cols] = jnp.zeros(shape, BF16)
            else:
                p_ref[rows, cols] = jnp.exp((load(r, c, vis) - m).astype(BF16))


def _attention_sweep(q_ref, k_ref, v_aug, scratch, tq, col_bias_fn, keep_fn, emit):
    seq = q_ref.shape[0]
    n_items = 2 * (seq // tq)
    n_slots = scratch[0].shape[0]
    qh, res = {}, {}

    def bufs(n):
        return [sc.at[n % n_slots] for sc in scratch]

    def score(n):
        qi, hh = divmod(n, 2)
        hi = (qi + 1) * tq
        if hh == 0:
            qh[qi] = _split_heads(q_ref[qi * tq:hi, :])
        bufs(n)[0][:, :hi] = _nt_dot(qh[qi][hh], k_ref[:hi, :])

    def softmax(n):
        qi, hh = divmod(n, 2)
        hi = (qi + 1) * tq
        bias = None if col_bias_fn is None else col_bias_fn(hh, hi)
        _softmax_rows(*bufs(n), hi, tq, bias, keep_fn)

    def values(n):
        qi, hh = divmod(n, 2)
        hi = (qi + 1) * tq
        res[hh] = _dot(bufs(n)[1][:, :hi], v_aug(hh)[:hi, :])
        if hh == 1:
            emit(qi, res[0], res[1])

    score(0)
    for n in range(n_items):
        if n + 1 < n_items:
            score(n + 1)
        softmax(n)
        if n:
            values(n - 1)
    values(n_items - 1)


def _split_heads(q):
    low = lax.broadcasted_iota(jnp.int32, (1, LANES), 1) < HEAD_DIM
    zero = jnp.zeros_like(q)
    return jnp.where(low, q, zero), jnp.where(low, zero, q)


def _fox_kernel(q_ref, k_ref, v_ref, ct_ref, o_ref, s_sc, p_sc, m_sc, va_sc, *, tq):
    sub = (2 * pl.program_id(1)) & 7
    low = lax.broadcasted_iota(jnp.int32, (1, LANES), 1) < HEAD_DIM
    v = v_ref[...]
    one = jnp.ones_like(v)
    va_sc[0] = jnp.where(low, v, one)
    va_sc[1] = jnp.where(low, one, v)

    def keep_fn(row, col):
        return col <= row

    def col_bias(hh, hi):
        return ct_ref[0, pl.ds(sub + hh, 1), :hi]

    def emit(qi, r0, r1):
        num = jnp.where(low, r0, r1)
        den = pltpu.roll(jnp.where(low, r1, r0), HEAD_DIM, 1)
        o_ref[qi * tq:(qi + 1) * tq, :] = (num / den).astype(BF16)

    _attention_sweep(q_ref, k_ref, lambda hh: va_sc.at[hh], (s_sc, p_sc, m_sc), tq, col_bias,
                     keep_fn, emit)


def _attention_scratch(tq, seq):
    n = 2 * ATT_SLOTS
    return [pltpu.VMEM((n, tq, seq), F32), pltpu.VMEM((n, tq, seq), BF16),
            pltpu.VMEM((n, tq, LANES), F32)]


def _fox_attention(qkvm, cum_t, batch, seq):
    t = batch * seq
    npair = FOX_HEADS // 2
    return pl.pallas_call(
        functools.partial(_fox_kernel, tq=ATT_TILE),
        grid=(batch, npair),
        in_specs=[
            pl.BlockSpec((seq, LANES), lambda b, p: (b, p)),
            pl.BlockSpec((seq, LANES), lambda b, p: (b, npair + p)),
            pl.BlockSpec((seq, LANES), lambda b, p: (b, 2 * npair + p)),
            pl.BlockSpec((1, 8, seq), lambda b, p: (b, p // 4, 0)),
        ],
        out_specs=pl.BlockSpec((seq, LANES), lambda b, p: (b, p)),
        out_shape=jax.ShapeDtypeStruct((t, npair * LANES), BF16),
        scratch_shapes=_attention_scratch(ATT_TILE, seq) + [pltpu.VMEM((2, seq, LANES), BF16)],
        compiler_params=_params("parallel", "parallel"),
        name="fox_attention",
    )(qkvm, qkvm, qkvm, cum_t)


def _mem_kernel(q_ref, mk_ref, mv_ref, o_ref):
    q = q_ref[...]
    mk = mk_ref[...]
    mv = mv_ref[...]
    m = mk.shape[0]
    lane = lax.broadcasted_iota(jnp.int32, (1, MEM_WIDTH), 1)
    zero = jnp.zeros_like(mk)
    masks = [(lane >= h * HEAD_DIM) & (lane < (h + 1) * HEAD_DIM) for h in range(MEM_HEADS)]
    k_stack = jnp.concatenate([jnp.where(hm, mk, zero) for hm in masks], axis=0)
    s = _nt_dot(q, k_stack)
    out = jnp.zeros(q.shape, F32)
    for h, hm in enumerate(masks):
        sh = s[:, h * m:(h + 1) * m]
        p = jnp.exp(sh - jnp.max(sh, axis=1, keepdims=True))
        l = jnp.sum(p, axis=1, keepdims=True)
        out = jnp.where(hm, _dot(p.astype(BF16), mv) / l, out)
    o_ref[...] = out.astype(BF16)


def _mem_attention(q_arr, q_col_block, mkv, batch, seq, mem_tokens):
    nq = seq // MEM_Q_TILE
    return pl.pallas_call(
        _mem_kernel,
        grid=(batch, nq),
        in_specs=[
            pl.BlockSpec((MEM_Q_TILE, MEM_WIDTH), lambda b, i: (b * nq + i, q_col_block)),
            pl.BlockSpec((mem_tokens, MEM_WIDTH), lambda b, i: (b, 0)),
            pl.BlockSpec((mem_tokens, MEM_WIDTH), lambda b, i: (b, 1)),
        ],
        out_specs=pl.BlockSpec((MEM_Q_TILE, MEM_WIDTH), lambda b, i: (b * nq + i, 0)),
        out_shape=jax.ShapeDtypeStruct((batch * seq, MEM_WIDTH), BF16),
        compiler_params=_params("parallel", "parallel"),
        name="mem_attention",
    )(q_arr, mkv, mkv)


def _norm_matmul_kernel(x_ref, g_ref, w_ref, o_ref):
    h = _rms(x_ref[...], g_ref[...]).astype(BF16)
    o_ref[...] = _dot(h, w_ref[...]).astype(o_ref.dtype)


def _norm_matmul(x, g, w, name):
    t, d = x.shape
    n = w.shape[1]
    return pl.pallas_call(
        _norm_matmul_kernel,
        grid=(t // ROW_TILE,),
        in_specs=[
            pl.BlockSpec((ROW_TILE, d), lambda i: (i, 0)),
            pl.BlockSpec((1, d), lambda i: (0, 0)),
            pl.BlockSpec((d, n), lambda i: (0, 0)),
        ],
        out_specs=pl.BlockSpec((ROW_TILE, n), lambda i: (i, 0)),
        out_shape=jax.ShapeDtypeStruct((t, n), BF16),
        compiler_params=_params("parallel"),
        name=name,
    )(x, g, w)


def _mixed_residual(x_ref, ys_ref, ym_ref, wo_ref):
    sw = ys_ref.shape[1]
    return x_ref[...] + _dot(ys_ref[...], wo_ref[:sw, :]) + _dot(ym_ref[...], wo_ref[sw:, :])


def _swiglu_hidden(g, u):
    return (g * jax.nn.sigmoid(g) * u).astype(BF16)


def _ffn_kernel(x_ref, ys_ref, ym_ref, wo_ref, g_ref, wgu_ref, wd_ref, o_ref, h_sc, *, col):
    ff = wd_ref.shape[0]
    x = _mixed_residual(x_ref, ys_ref, ym_ref, wo_ref)
    h_sc[...] = _rms(x, g_ref[...]).astype(BF16)
    o_ref[...] = x
    for c in range(ff // col):
        h = h_sc[...]
        gate = _dot(h, wgu_ref[:, c * col:(c + 1) * col])
        up = _dot(h, wgu_ref[:, ff + c * col:ff + (c + 1) * col])
        o_ref[...] += _dot(_swiglu_hidden(gate, up), wd_ref[c * col:(c + 1) * col, :])


def _dense_ffn(x, y_self, y_mem, w_out, g, w_gate_up, w_down):
    t, d = x.shape
    tm = FFN_ROW_TILE
    row = lambda i: (i, 0)
    resident = dict(index_map=lambda i: (0, 0), pipeline_mode=pl.Buffered(1))
    return pl.pallas_call(
        functools.partial(_ffn_kernel, col=FFN_COL_TILE),
        grid=(t // tm,),
        in_specs=[
            pl.BlockSpec((tm, d), row),
            pl.BlockSpec((tm, y_self.shape[1]), row),
            pl.BlockSpec((tm, y_mem.shape[1]), row),
            pl.BlockSpec(w_out.shape, **resident),
            pl.BlockSpec((1, d), lambda i: (0, 0)),
            pl.BlockSpec(w_gate_up.shape, **resident),
            pl.BlockSpec(w_down.shape, **resident),
        ],
        out_specs=pl.BlockSpec((tm, d), row),
        out_shape=jax.ShapeDtypeStruct((t, d), F32),
        scratch_shapes=[pltpu.VMEM((tm, d), BF16)],
        compiler_params=pltpu.CompilerParams(dimension_semantics=("parallel",),
                                             vmem_limit_bytes=FFN_VMEM_LIMIT),
        name="dense_ffn",
    )(x, y_self, y_mem, w_out, g, w_gate_up, w_down)


def _rope_kernel(pos_ref, inv_ref, cos_ref, sin_ref):
    pos = pos_ref[0].astype(F32)
    ang = inv_ref[...] * pos
    reps = LANES // ang.shape[0]
    cs = jnp.concatenate([jnp.cos(ang)] * reps, axis=0)
    sn = jnp.concatenate([jnp.sin(ang)] * reps, axis=0)
    row = lax.broadcasted_iota(jnp.int32, cs.shape, 0)
    rot = (row & (HEAD_DIM - 1)) < ROT_DIM
    first = (row & (ROT_DIM - 1)) < (ROT_DIM // 2)
    cos_ref[...] = jnp.where(rot, cs, 1.0).T
    sin_ref[...] = jnp.where(rot, jnp.where(first, -sn, sn), 0.0).T


def _rope_tables(pos_rows, inv_col):
    steps = pos_rows.shape[0]
    t = steps * ROW_TILE
    return pl.pallas_call(
        _rope_kernel,
        grid=(steps,),
        in_specs=[
            pl.BlockSpec((1, 1, ROW_TILE), lambda i: (i, 0, 0)),
            pl.BlockSpec(inv_col.shape, lambda i: (0, 0)),
        ],
        out_specs=[pl.BlockSpec((ROW_TILE, LANES), lambda i: (i, 0))] * 2,
        out_shape=[jax.ShapeDtypeStruct((t, LANES), F32)] * 2,
        compiler_params=_params("parallel"),
        name="rope_tables",
    )(pos_rows, inv_col)


def _proj_rope_kernel(x_ref, gk_ref, gq_ref, wk_ref, wq_ref, cos_ref, sin_ref, ok_ref, oq_ref, *,
                      n_rope, col_chunk):
    x = x_ref[...]
    xn = x * lax.rsqrt(jnp.mean(x * x, axis=-1, keepdims=True) + EPS)
    cos = cos_ref[...]
    sin = sin_ref[...]
    lane = lax.broadcasted_iota(jnp.int32, (1, LANES), 1)
    first = (lane & (ROT_DIM - 1)) < (ROT_DIM // 2)
    half = ROT_DIM // 2
    for g_ref, w_ref, o_ref in ((gk_ref, wk_ref, ok_ref), (gq_ref, wq_ref, oq_ref)):
        h = (xn * g_ref[...]).astype(BF16)
        for c in range(w_ref.shape[1] // col_chunk):
            a = _dot(h, w_ref[:, c * col_chunk:(c + 1) * col_chunk])
            for s in range(col_chunk // LANES):
                col = c * col_chunk + s * LANES
                blk = a[:, s * LANES:(s + 1) * LANES]
                if col < n_rope:
                    partner = jnp.where(first, pltpu.roll(blk, LANES - half, 1),
                                        pltpu.roll(blk, half, 1))
                    blk = blk * cos + partner * sin
                o_ref[:, col:col + LANES] = blk.astype(BF16)


def _proj_rope(x, g_kv, g_q, w_kv, w_q, cos, sin, n_rope):
    t, d = x.shape
    nk, nq = w_kv.shape[1], w_q.shape[1]
    row = lambda i: (i, 0)
    fixed = lambda i: (0, 0)
    return pl.pallas_call(
        functools.partial(_proj_rope_kernel, n_rope=n_rope, col_chunk=256),
        grid=(t // ROW_TILE,),
        in_specs=[
            pl.BlockSpec((ROW_TILE, d), row),
            pl.BlockSpec((1, d), fixed),
            pl.BlockSpec((1, d), fixed),
            pl.BlockSpec((d, nk), fixed),
            pl.BlockSpec((d, nq), fixed),
            pl.BlockSpec((ROW_TILE, LANES), row),
            pl.BlockSpec((ROW_TILE, LANES), row),
        ],
        out_specs=[pl.BlockSpec((ROW_TILE, nk), row), pl.BlockSpec((ROW_TILE, nq), row)],
        out_shape=[jax.ShapeDtypeStruct((t, nk), BF16), jax.ShapeDtypeStruct((t, nq), BF16)],
        compiler_params=_params("parallel"),
        name="proj_kv_q",
    )(x, g_kv, g_q, w_kv, w_q, cos, sin)


def _diff_kernel(lam_ref, sg_ref, q_ref, k_ref, v_ref, o_ref, s_sc, p_sc, m_sc, va_sc, *, tq,
                 lam_init):
    lp = lam_ref[...]
    lam = (jnp.exp(jnp.sum(lp[0:1] * lp[1:2], axis=1, keepdims=True))
           - jnp.exp(jnp.sum(lp[2:3] * lp[3:4], axis=1, keepdims=True)) + lam_init)
    va_sc[:, :LANES] = v_ref[...]
    va_sc[:, LANES:] = jnp.ones(v_ref.shape, BF16)

    def keep_fn(row, col):
        return (col >> CHUNK_SHIFT) <= (row >> CHUNK_SHIFT)

    def emit(qi, r1, r2):
        o = r1[:, :LANES] / r1[:, LANES:] - lam * (r2[:, :LANES] / r2[:, LANES:])
        o = _rms(o, sg_ref[...]) * (1.0 - lam_init)
        o_ref[qi * tq:(qi + 1) * tq, :] = o.astype(BF16)

    _attention_sweep(q_ref, k_ref, lambda hh: va_sc, (s_sc, p_sc, m_sc), tq, None, keep_fn, emit)


def _diff_attention(lam_rows, subln_g, qb, kv, batch, seq, lam_init):
    t = batch * seq
    return pl.pallas_call(
        functools.partial(_diff_kernel, tq=ATT_TILE, lam_init=lam_init),
        grid=(batch, DIFF_HEADS),
        in_specs=[
            pl.BlockSpec((8, LANES), lambda b, h: (0, 0)),
            pl.BlockSpec((1, LANES), lambda b, h: (0, 0)),
            pl.BlockSpec((seq, LANES), lambda b, h: (b, h)),
            pl.BlockSpec((seq, LANES), lambda b, h: (b, h)),
            pl.BlockSpec((seq, LANES), lambda b, h: (b, DIFF_HEADS + h)),
        ],
        out_specs=pl.BlockSpec((seq, LANES), lambda b, h: (b, h)),
        out_shape=jax.ShapeDtypeStruct((t, DIFF_HEADS * LANES), BF16),
        scratch_shapes=_attention_scratch(ATT_TILE, seq) + [pltpu.VMEM((seq, 2 * LANES), BF16)],
        compiler_params=_params("parallel", "parallel"),
        name="diff_attention",
    )(lam_rows, subln_g, qb, kv, kv)


def _moe_pre_kernel(x_ref, ys_ref, ym_ref, wo_ref, g_ref, wr_ref, tri_ref, x3_ref, hp_ref, route_ref,
                    route_t_ref, cnt_ref, run_sc):
    @pl.when(pl.program_id(0) == 0)
    def _():
        run_sc[...] = jnp.zeros_like(run_sc)

    x = _mixed_residual(x_ref, ys_ref, ym_ref, wo_ref)
    x3_ref[...] = x
    hf = _rms(x, g_ref[...])
    hp_ref[...] = hf

    h_hi = hf.astype(BF16)
    h_lo = (hf - h_hi.astype(F32)).astype(BF16)
    logits = _dot(jnp.concatenate([h_hi, h_lo, h_hi], axis=1), wr_ref[...])
    lane = lax.broadcasted_iota(jnp.int32, logits.shape, 1)
    lanef = lane.astype(F32)
    lg = jnp.where(lane < N_EXPERTS, logits, NEG)
    v1 = jnp.max(lg, axis=1, keepdims=True)
    i1 = jnp.min(jnp.where(lg == v1, lanef, float(LANES)), axis=1, keepdims=True)
    lg2 = jnp.where(lanef == i1, NEG, lg)
    v2 = jnp.max(lg2, axis=1, keepdims=True)
    i2 = jnp.min(jnp.where(lg2 == v2, lanef, float(LANES)), axis=1, keepdims=True)
    e = jnp.exp(v2 - v1)
    g1 = 1.0 / (1.0 + e)
    g2 = e / (1.0 + e)

    oh1 = lanef == i1
    oh2 = lanef == i2
    oh = jnp.where(oh1 | oh2, 1.0, 0.0)
    before = _dot(tri_ref[...], oh.astype(BF16)) + run_sc[...]
    r1 = jnp.sum(jnp.where(oh1, before, 0.0), axis=1, keepdims=True)
    r2 = jnp.sum(jnp.where(oh2, before, 0.0), axis=1, keepdims=True)
    run_sc[...] += jnp.sum(oh, axis=0, keepdims=True)
    cnt_ref[...] = run_sc[...]

    route = jnp.where(lane == 0, i1, 0.0)
    for ln, val in ((1, i2), (2, r1), (3, r2), (4, g1), (5, g2)):
        route = jnp.where(lane == ln, val, route)
    route_ref[...] = route
    route_t_ref[...] = route.T[:SUBLANES, :]


def _moe_pre(x, y_self, y_mem, w_out, g, w_router_pad, tri):
    t, d = x.shape
    tm = tri.shape[0]
    row = lambda i: (i, 0)
    fixed = lambda i: (0, 0)
    return pl.pallas_call(
        _moe_pre_kernel,
        grid=(t // tm,),
        in_specs=[
            pl.BlockSpec((tm, d), row),
            pl.BlockSpec((tm, y_self.shape[1]), row),
            pl.BlockSpec((tm, y_mem.shape[1]), row),
            pl.BlockSpec(w_out.shape, fixed),
            pl.BlockSpec((1, d), fixed),
            pl.BlockSpec((3 * d, LANES), fixed),
            pl.BlockSpec((tm, tm), fixed),
        ],
        out_specs=[
            pl.BlockSpec((tm, d), row),
            pl.BlockSpec((tm, d), row),
            pl.BlockSpec((tm, LANES), row),
            pl.BlockSpec((SUBLANES, tm), lambda i: (0, i)),
            pl.BlockSpec((1, LANES), fixed),
        ],
        out_shape=[
            jax.ShapeDtypeStruct((t, d), F32),
            jax.ShapeDtypeStruct((t, d), F32),
            jax.ShapeDtypeStruct((t, LANES), F32),
            jax.ShapeDtypeStruct((SUBLANES, t), F32),
            jax.ShapeDtypeStruct((1, LANES), F32),
        ],
        scratch_shapes=[pltpu.VMEM((1, LANES), F32)],
        compiler_params=_params("arbitrary"),
        name="moe_pre",
    )(x, y_self, y_mem, w_out, g, w_router_pad, tri)


def _slot_table_kernel(idx_ref, tbl_ref, *, n_tokens, n_tiles, stride):
    def fill(j, c):
        bank = 2 * n_tokens + (j & 1) * stride
        for r in range(stride):
            tbl_ref[j * stride + r] = bank + r
        return c

    lax.fori_loop(0, n_tiles + 1, fill, 0)

    def body(tok, c):
        for k in range(2):
            tbl_ref[idx_ref[k * n_tokens + tok]] = k * n_tokens + tok
        return c

    lax.fori_loop(0, n_tokens, body, 0, unroll=8)


def _slot_table(pos, n_tiles, n_tokens):
    tm = MOE_ROW_TILE
    table_idx = (pos // tm + 1) * MOE_SRC_STRIDE + pos % tm
    smem = pl.BlockSpec(memory_space=pltpu.SMEM)
    return pl.pallas_call(
        functools.partial(_slot_table_kernel, n_tokens=n_tokens, n_tiles=n_tiles,
                          stride=MOE_SRC_STRIDE),
        in_specs=[smem],
        out_specs=smem,
        out_shape=jax.ShapeDtypeStruct(((n_tiles + 1) * MOE_SRC_STRIDE,), jnp.int32),
        name="moe_slot_table",
    )(table_idx)


def _moe_kernel(te_ref, na_ref, nv_ref, tbl_ref, h_ref, wg_ref, wu_ref, wd_ref, y_ref, xg_sc, acc_sc,
                xb_sc, gsem, ssem, *, chunk, stride, tm):
    del te_ref
    i = pl.program_id(0)
    f = pl.program_id(1)
    n_tokens = h_ref.shape[0]
    n_copied = pl.num_programs(1) * chunk
    slot = i & 1
    na = na_ref[0]

    def gather_copy(tile, sl, r):
        src = tbl_ref[(tile + 1) * stride + r] & (n_tokens - 1)
        return pltpu.make_async_copy(h_ref.at[pl.ds(src, 1)], xg_sc.at[sl, pl.ds(r, 1)],
                                     gsem.at[sl])

    def scatter_copy(tile, sl, r):
        dst = tbl_ref[(tile + 1) * stride + r]
        return pltpu.make_async_copy(acc_sc.at[sl, pl.ds(r, 1)], y_ref.at[pl.ds(dst, 1)],
                                     ssem.at[sl])

    def for_rows(fn):
        def body(r, c):
            fn(r)
            return c
        lax.fori_loop(0, n_copied, body, 0, unroll=8)

    @pl.when((i == 0) & (f == 0))
    def _():
        acc_sc[...] = jnp.zeros_like(acc_sc)
        for bank in range(2):
            zero = pltpu.make_async_copy(
                acc_sc.at[bank], y_ref.at[pl.ds(2 * n_tokens + bank * stride, stride)], ssem.at[0])
            zero.start()
            zero.wait()
        for_rows(lambda r: gather_copy(0, 0, r).start())

    @pl.when((i == na) & (f == 0))
    def _():
        for_rows(lambda r: gather_copy(i, slot, r).wait())

        @pl.when(i >= 1)
        def _():
            for_rows(lambda r: scatter_copy(i - 2, slot, r).wait())

        for_rows(lambda r: scatter_copy(i - 1, 1 - slot, r).start())
        for_rows(lambda r: scatter_copy(i - 1, 1 - slot, r).wait())

    @pl.when(i < na)
    def _():
        @pl.when(f == 0)
        def _():
            for_rows(lambda r: gather_copy(i, slot, r).wait())
            xb_sc[...] = xg_sc[slot, :tm, :].astype(BF16)

            @pl.when(i >= 1)
            def _():
                for_rows(lambda r: scatter_copy(i - 2, slot, r).wait())

            acc_sc[slot, :tm, :] = jnp.zeros((tm, acc_sc.shape[2]), F32)

        def move_rows(lo, hi):
            for r in range(lo, hi):
                gather_copy(i + 1, 1 - slot, f * chunk + r).start()
                scatter_copy(i - 1, 1 - slot, f * chunk + r).start()

        def expert_rows(n_rows):
            xb = xb_sc[:n_rows, :]
            g = _dot(xb, wg_ref[0].astype(BF16))
            move_rows(0, chunk // 2)
            u = _dot(xb, wu_ref[0].astype(BF16))
            move_rows(chunk // 2, chunk)
            acc_sc[slot, :n_rows, :] += _dot(_swiglu_hidden(g, u), wd_ref[0].astype(BF16))

        @pl.when(nv_ref[i] > tm // 2)
        def _():
            expert_rows(tm)

        @pl.when(nv_ref[i] <= tm // 2)
        def _():
            expert_rows(tm // 2)


def _moe_experts(tile_expert, n_active, n_valid, table, h_rows, w_gate_up, w_down, n_tiles):
    n_tokens = h_rows.shape[0]
    assert n_tokens & (n_tokens - 1) == 0
    d = w_down.shape[2]
    ff = w_down.shape[1]
    ct = MOE_COL_TILE
    nf = ff // ct
    tm = MOE_ROW_TILE
    stride = MOE_SRC_STRIDE
    chunk = -(-tm // nf)
    assert nf * chunk <= stride

    def col(i, f, na):
        return jnp.where(i < na[0], f, nf - 1)

    return pl.pallas_call(
        functools.partial(_moe_kernel, chunk=chunk, stride=stride, tm=tm),
        grid_spec=pltpu.PrefetchScalarGridSpec(
            num_scalar_prefetch=4,
            grid=(n_tiles, nf),
            in_specs=[
                pl.BlockSpec(memory_space=pl.ANY),
                pl.BlockSpec((1, d, ct), lambda i, f, te, na, *_: (te[i], 0, col(i, f, na))),
                pl.BlockSpec((1, d, ct), lambda i, f, te, na, *_: (te[i], 0, nf + col(i, f, na))),
                pl.BlockSpec((1, ct, d), lambda i, f, te, na, *_: (te[i], col(i, f, na), 0)),
            ],
            out_specs=pl.BlockSpec(memory_space=pl.ANY),
            scratch_shapes=[pltpu.VMEM((2, stride, d), F32), pltpu.VMEM((2, stride, d), F32),
                            pltpu.VMEM((tm, d), BF16),
                            pltpu.SemaphoreType.DMA((2,)), pltpu.SemaphoreType.DMA((2,))],
        ),
        out_shape=jax.ShapeDtypeStruct((2 * n_tokens + 2 * stride, d), F32),
        compiler_params=_params("arbitrary", "arbitrary"),
        name="moe_experts",
    )(tile_expert, n_active, n_valid, table, h_rows, w_gate_up, w_gate_up, w_down)


def _combine_kernel(x_ref, route_ref, g_ref, y0_ref, y1_ref, o_ref):
    route = route_ref[...]
    y = x_ref[...] + route[:, 4:5] * y0_ref[...] + route[:, 5:6] * y1_ref[...]
    o_ref[...] = _rms(y, g_ref[...])


def _combine(x, route, g, y_pairs):
    t, d = x.shape
    tile = COMBINE_TILE
    nt = t // tile
    return pl.pallas_call(
        _combine_kernel,
        grid=(nt,),
        in_specs=[
            pl.BlockSpec((tile, d), lambda i: (i, 0)),
            pl.BlockSpec((tile, LANES), lambda i: (i, 0)),
            pl.BlockSpec((1, d), lambda i: (0, 0)),
            pl.BlockSpec((tile, d), lambda i: (i, 0)),
            pl.BlockSpec((tile, d), lambda i: (nt + i, 0)),
        ],
        out_specs=pl.BlockSpec((tile, d), lambda i: (i, 0)),
        out_shape=jax.ShapeDtypeStruct((t, d), F32),
        compiler_params=_params("parallel"),
        name="moe_combine",
    )(x, route, g, y_pairs, y_pairs)


def _row(v):
    return v.reshape(1, -1).astype(F32)


def _pad_lanes(v):
    return jnp.pad(v.astype(F32), (0, LANES - v.shape[0])).reshape(1, LANES)


def kernel(x, mem, positions, mix_norm_g, ffn_norm_g, mem_norm_g, w_mem_kv, w_out, w_in_a, b_forget, w_q_b, lambda_q1, lambda_k1, lambda_q2, lambda_k2, subln_g, kv_norm_g, w_kv_shared, w_gate_up_dense, w_down_dense, w_router, w_gate_up_moe, w_down_moe, final_norm_g):
    batch, seq, d = x.shape
    mem_tokens = mem.shape[1]
    t = batch * seq
    assert w_in_a.shape[0] == 1 and w_q_b.shape[0] == 1 and w_out.shape[0] == 2
    fox_w = FOX_HEADS * HEAD_DIM
    diff_w = DIFF_HEADS * 2 * HEAD_DIM
    scale = HEAD_DIM ** -0.5

    xf = x.reshape(t, d)
    memf = mem.reshape(batch * mem_tokens, d)

    wa = w_in_a[0]
    n_a = wa.shape[1]
    n_a_pad = -(-n_a // LANES) * LANES
    col = jnp.arange(n_a_pad)
    col_scale = jnp.where((col < fox_w) | (col >= 3 * fox_w + FOX_HEADS), scale, 1.0).astype(F32)
    w_a = (jnp.pad(wa, ((0, 0), (0, n_a_pad - n_a))) * col_scale).astype(BF16)
    w_b = (w_q_b[0] * scale).astype(BF16)
    w_kv = w_kv_shared.astype(BF16)
    w_o = w_out.astype(BF16)
    w_mkv = w_mem_kv.astype(BF16)
    w_gu_d = w_gate_up_dense[0].astype(BF16)
    w_dn_d = w_down_dense[0].astype(BF16)
    w_gu_m = w_gate_up_moe[0]
    w_dn_m = w_down_moe[0]
    w_r = jnp.pad(w_router[0].astype(F32), ((0, 0), (0, LANES - N_EXPERTS)))
    w_r_hi = w_r.astype(BF16)
    w_r_lo = (w_r - w_r_hi.astype(F32)).astype(BF16)
    w_r = jnp.concatenate([w_r_hi, w_r_hi, w_r_lo], axis=0)

    qkvm, log_f = _proj_a(xf, _row(mix_norm_g[0]), w_a, _pad_lanes(b_forget[0]), 3 * fox_w,
                          FOX_HEADS)
    cum_t = _cumsum(log_f, batch, seq)
    y_self = _fox_attention(qkvm, cum_t, batch, seq)
    mkv0 = _norm_matmul(memf, _row(mem_norm_g[0]), w_mkv[0], "mem_kv0")
    y_mem = _mem_attention(qkvm, 3 * fox_w // MEM_WIDTH, mkv0, batch, seq, mem_tokens)
    x2 = _dense_ffn(xf, y_self, y_mem, w_o[0], _row(ffn_norm_g[0]), w_gu_d, w_dn_d)

    half = ROT_DIM // 2
    inv_freq = ROPE_THETA ** (-(jnp.arange(half, dtype=F32) * 2.0 / ROT_DIM))
    inv_col = inv_freq.reshape(half, 1)
    pos_rows = positions.astype(jnp.int32).reshape(t // ROW_TILE, 1, ROW_TILE)
    cos_t, sin_t = _rope_tables(pos_rows, inv_col)
    kv, qb = _proj_rope(x2, _row(kv_norm_g), _row(mix_norm_g[1]), w_kv, w_b, cos_t, sin_t, diff_w)

    lam_init = 0.8 - 0.6 * math.exp(-0.3 * 1)
    lam_rows = jnp.concatenate(
        [jnp.pad(v[0].astype(F32), (0, LANES - HEAD_DIM)).reshape(1, LANES)
         for v in (lambda_q1, lambda_k1, lambda_q2, lambda_k2)]
        + [jnp.zeros((4, LANES), F32)], axis=0)
    y_self = _diff_attention(lam_rows, _row(subln_g[0]), qb, kv, batch, seq, lam_init)
    mkv1 = _norm_matmul(memf, _row(mem_norm_g[1]), w_mkv[1], "mem_kv1")
    y_mem = _mem_attention(qb, diff_w // MEM_WIDTH, mkv1, batch, seq, mem_tokens)

    tri = jnp.tril(jnp.ones((MOE_PRE_TILE, MOE_PRE_TILE), BF16), -1)
    x3, h_rows, route, route_t, cnt = _moe_pre(x2, y_self, y_mem, w_o[1], _row(ffn_norm_g[1]), w_r, tri)
    tm = MOE_ROW_TILE
    n_tiles = 2 * t // tm + N_EXPERTS
    counts = cnt[0, :N_EXPERTS].astype(jnp.int32)
    tiles_per = (counts + tm - 1) // tm
    tile_end = jnp.cumsum(tiles_per)
    group_off = (tile_end - tiles_per) * tm
    n_active = tile_end[-1:]
    tile_id = jnp.minimum(jnp.arange(n_tiles, dtype=jnp.int32), n_active - 1)
    tile_expert = jnp.sum(tile_id[:, None] >= tile_end[None, :], axis=1).astype(jnp.int32)
    idx = route_t[0:2].astype(jnp.int32)
    rank = route_t[2:4].astype(jnp.int32)
    expert_ids = jnp.arange(N_EXPERTS, dtype=jnp.int32).reshape(N_EXPERTS, 1, 1)
    off = jnp.sum(jnp.where(idx[None] == expert_ids, group_off.reshape(N_EXPERTS, 1, 1), 0), axis=0)
    pos = (off + rank).reshape(2 * t)

    table = _slot_table(pos, n_tiles, t)
    group_end = group_off + counts
    n_valid = jnp.clip(group_end[tile_expert] - tile_id * tm, 0, tm).astype(jnp.int32)
    y_pairs = _moe_experts(tile_expert, n_active.astype(jnp.int32), n_valid, table, h_rows,
                           w_gu_m, w_dn_m, n_tiles)
    out = _combine(x3, route, _row(final_norm_g), y_pairs)
    return out.reshape(batch, seq, d)
```
